```python
import jax, jax.numpy as jnp
from jax import lax
import numpy as np

D_MODEL = 4096
BATCH = 4
SEQ = 4096
DEPTH = 4

CHUNK = 64
Q_BLOCK = 128
HEAD_DIM = 128
D_FF = 4 * D_MODEL
EPS = 1e-6
NEG_INF = -1e30

A_HEADS = D_MODEL // (2 * HEAD_DIM)
A_LEFT_CHUNKS = 8
A_MAX_REL = 256
B_HEADS = D_MODEL // (2 * HEAD_DIM)
IDX_HEADS = D_MODEL // 128
IDX_DIM = 64
TOPK_MAX = 256
C_HEAD_DIM = 64
C_HEADS = D_MODEL // (2 * C_HEAD_DIM)
C_WIDTH = C_HEADS * C_HEAD_DIM
DECAY_LORA = 128
ICLR_LORA = 128
GATE_LORA = 256
C_GN_EPS = 64e-5
D_HEADS = D_MODEL // (2 * HEAD_DIM)
Q_LORA = 3 * D_MODEL // 16
KV_LORA = D_MODEL // 8
NOPE_DIM = 128
ROPE_DIM = 64
V_DIM = 128
ROPE_THETA = 10000.0

N_EVEN = (DEPTH + 1) // 2
N_ODD = DEPTH // 2

EVEN_SPLITS = (A_HEADS * HEAD_DIM, A_HEADS * HEAD_DIM, A_HEADS * HEAD_DIM,
               B_HEADS * HEAD_DIM, HEAD_DIM, HEAD_DIM,
               IDX_HEADS * IDX_DIM, IDX_DIM, IDX_HEADS)
C_SPLITS = (C_WIDTH, C_WIDTH, C_WIDTH, DECAY_LORA, ICLR_LORA, GATE_LORA)
D_SPLITS = (Q_LORA, KV_LORA, ROPE_DIM)
EVEN_COLS = sum(EVEN_SPLITS)
C_COLS = sum(C_SPLITS)
ODD_COLS = C_COLS + sum(D_SPLITS)
EVEN_OUT = A_HEADS * HEAD_DIM + B_HEADS * HEAD_DIM
ODD_OUT = C_WIDTH + D_HEADS * V_DIM

kernel_name = 'hybrid_chunk_causal_encoder'


def split_cols(u, sizes):
    offs = np.cumsum(sizes)[:-1].tolist()
    return jnp.split(u, offs, axis=-1)


def rms_norm(x, g):
    xf = x.astype(jnp.float32)
    y = xf * lax.rsqrt(jnp.mean(xf * xf, axis=-1, keepdims=True) + EPS)
    return (y * g.astype(jnp.float32)).astype(x.dtype)


def rope(z, pos):
    half = ROPE_DIM // 2
    freqs = ROPE_THETA ** (-jnp.arange(half, dtype=jnp.float32) / half)
    ang = pos.astype(jnp.float32)[:, None] * freqs[None, :]
    cos = jnp.cos(ang)[None, :, None, :]
    sin = jnp.sin(ang)[None, :, None, :]
    zf = z.astype(jnp.float32)
    z1, z2 = zf[..., :half], zf[..., half:]
    return jnp.concatenate([z1 * cos - z2 * sin, z1 * sin + z2 * cos], axis=-1).astype(z.dtype)


def chunk_band_attention(q, k, v, rel_bias):
    b, t, h, d = q.shape
    n_chunks = t // CHUNK
    pad = A_LEFT_CHUNKS * CHUNK
    band = pad + CHUNK
    kp = jnp.pad(k, ((0, 0), (pad, 0), (0, 0), (0, 0)))
    vp = jnp.pad(v, ((0, 0), (pad, 0), (0, 0), (0, 0)))
    i = jnp.arange(CHUNK)[:, None]
    j = jnp.arange(band)[None, :]
    rel = jnp.clip(i + pad - j, -A_MAX_REL, A_MAX_REL) + A_MAX_REL
    bias = rel_bias.astype(jnp.float32)[:, rel]
    scale = d ** -0.5

    def one_chunk(n):
        start = n * CHUNK
        qc = lax.dynamic_slice_in_dim(q, start, CHUNK, axis=1)
        kc = lax.dynamic_slice_in_dim(kp, start, band, axis=1)
        vc = lax.dynamic_slice_in_dim(vp, start, band, axis=1)
        s = jnp.einsum('bqhd,bkhd->bhqk', qc, kc).astype(jnp.float32) * scale + bias[None]
        valid = (j + start) >= pad
        s = jnp.where(valid[None, None], s, NEG_INF)
        p = jax.nn.softmax(s, axis=-1)
        return jnp.einsum('bhqk,bkhd->bqhd', p.astype(vc.dtype), vc)

    out = lax.map(one_chunk, jnp.arange(n_chunks))
    return jnp.moveaxis(out, 0, 1).reshape(b, t, h, d)


def dsa_attention(q, k, v, qi, ki, wi):
    b, t, h, d = q.shape
    topk = min(TOPK_MAX, t // 4)
    spos = jnp.arange(t)
    scale = d ** -0.5

    def one_block(n):
        start = n * Q_BLOCK
        qb = lax.dynamic_slice_in_dim(q, start, Q_BLOCK, axis=1)
        qib = lax.dynamic_slice_in_dim(qi, start, Q_BLOCK, axis=1)
        wib = lax.dynamic_slice_in_dim(wi, start, Q_BLOCK, axis=1)
        limit = ((start + jnp.arange(Q_BLOCK)) // CHUNK + 1) * CHUNK
        admissible = spos[None, :] < limit[:, None]
        logits = jnp.einsum('bqhd,bsd->bqhs', qib, ki).astype(jnp.float32)
        index_score = jnp.einsum('bqh,bqhs->bqs', wib.astype(jnp.float32), jax.nn.relu(logits))
        index_score = jnp.where(admissible[None], index_score, NEG_INF)
        _, sel = lax.top_k(index_score, topk)
        sel_ok = sel < limit[None, :, None]
        k_sel = jax.vmap(lambda kk, ii: kk[ii])(k, sel)
        v_sel = jax.vmap(lambda vv, ii: vv[ii])(v, sel)
        s = jnp.einsum('bqhd,bqkd->bhqk', qb, k_sel).astype(jnp.float32) * scale
        s = jnp.where(sel_ok[:, None], s, NEG_INF)
        p = jax.nn.softmax(s, axis=-1)
        return jnp.einsum('bhqk,bqkd->bqhd', p.astype(v_sel.dtype), v_sel)

    out = lax.map(one_block, jnp.arange(t // Q_BLOCK))
    return jnp.moveaxis(out, 0, 1).reshape(b, t, h, d)


def wkv7_scan(r, w, k, v, a, bb):
    b, t, h, n = r.shape

    def step(state, inp):
        r_t, w_t, k_t, v_t, a_t, b_t = inp
        sa = jnp.einsum('bhvk,bhk->bhv', state, a_t)
        state = (state * w_t[:, :, None, :] + sa[..., None] * b_t[:, :, None, :]
                 + v_t[..., None] * k_t[:, :, None, :])
        y = jnp.einsum('bhvk,bhk->bhv', state, r_t)
        return state, y

    xs = tuple(jnp.moveaxis(z, 1, 0) for z in (r, w, k, v, a, bb))
    state0 = jnp.zeros((b, h, n, n), jnp.float32)
    _, ys = lax.scan(step, state0, xs)
    return jnp.moveaxis(ys, 0, 1)


def rwkv7_time_mix(u_c, w0, w_up, a0, a_up, g_up, k_k, k_a, r_k, ln_w, ln_b):
    b, t, _ = u_c.shape
    f32 = jnp.float32
    r, k, v, xw, xa, xg = split_cols(u_c, C_SPLITS)
    logw = -jax.nn.softplus(-(w0 + jnp.tanh(xw) @ w_up)) - 0.5
    decay = jnp.exp(-jnp.exp(logw.astype(f32)))
    a = jax.nn.sigmoid(a0 + xa @ a_up)
    g = jax.nn.sigmoid(xg) @ g_up
    heads = lambda z: z.reshape(b, t, C_HEADS, C_HEAD_DIM).astype(f32)
    kk = heads(k * k_k)
    kk = kk / jnp.maximum(jnp.sqrt(jnp.sum(kk * kk, axis=-1, keepdims=True)), 1e-12)
    k = k * (1 + (a - 1) * k_a)
    r_h, k_h, v_h, a_h = heads(r), heads(k), heads(v), heads(a)
    y = wkv7_scan(r_h, heads(decay), k_h, v_h, -kk, kk * a_h)
    mean = jnp.mean(y, axis=-1, keepdims=True)
    var = jnp.mean(jnp.square(y - mean), axis=-1, keepdims=True)
    y = (y - mean) * lax.rsqrt(var + C_GN_EPS)
    y = y.reshape(b, t, C_WIDTH) * ln_w.astype(f32) + ln_b.astype(f32)
    bonus = jnp.sum(r_h * k_h * r_k.astype(f32), axis=-1, keepdims=True) * v_h
    y = (y + bonus.reshape(b, t, C_WIDTH)) * g.astype(f32)
    return y.astype(u_c.dtype)


def mla_attention(u_d, q_norm, kv_norm, w_uq, w_ukv):
    b, t, _ = u_d.shape
    cq, ckv, k_pe = split_cols(u_d, D_SPLITS)
    q = (rms_norm(cq, q_norm) @ w_uq).reshape(b, t, D_HEADS, NOPE_DIM + ROPE_DIM)
    kv = (rms_norm(ckv, kv_norm) @ w_ukv).reshape(b, t, D_HEADS, NOPE_DIM + V_DIM)
    q_nope, q_pe = q[..., :NOPE_DIM], q[..., NOPE_DIM:]
    k_nope, v = kv[..., :NOPE_DIM], kv[..., NOPE_DIM:]
    pos = jnp.arange(t)
    q_pe = rope(q_pe, pos)
    k_pe = rope(k_pe[:, :, None, :], pos)[:, :, 0]
    spos = jnp.arange(t)
    scale = (NOPE_DIM + ROPE_DIM) ** -0.5

    def one_block(n):
        start = n * Q_BLOCK
        qn = lax.dynamic_slice_in_dim(q_nope, start, Q_BLOCK, axis=1)
        qp = lax.dynamic_slice_in_dim(q_pe, start, Q_BLOCK, axis=1)
        s = (jnp.einsum('bqhd,bshd->bhqs', qn, k_nope)
             + jnp.einsum('bqhd,bsd->bhqs', qp, k_pe)).astype(jnp.float32) * scale
        limit = ((start + jnp.arange(Q_BLOCK)) // CHUNK + 1) * CHUNK
        mask = spos[None, :] < limit[:, None]
        s = jnp.where(mask[None, None], s, NEG_INF)
        p = jax.nn.softmax(s, axis=-1)
        return jnp.einsum('bhqs,bshd->bqhd', p.astype(v.dtype), v)

    out = lax.map(one_block, jnp.arange(t // Q_BLOCK))
    return jnp.moveaxis(out, 0, 1).reshape(b, t, D_HEADS * V_DIM)


def even_mixer(h, w_in, w_out, rel_bias):
    b, t, _ = h.shape
    u = h @ w_in
    qa, ka, va, qb, kb, vb, qi, ki, wi = split_cols(u, EVEN_SPLITS)
    ya = chunk_band_attention(qa.reshape(b, t, A_HEADS, HEAD_DIM), ka.reshape(b, t, A_HEADS, HEAD_DIM),
                              va.reshape(b, t, A_HEADS, HEAD_DIM), rel_bias)
    yb = dsa_attention(qb.reshape(b, t, B_HEADS, HEAD_DIM), kb, vb,
                       qi.reshape(b, t, IDX_HEADS, IDX_DIM), ki, wi)
    y = jnp.concatenate([ya.reshape(b, t, -1), yb.reshape(b, t, -1)], axis=-1)
    return y @ w_out


def odd_mixer(h, w_in, w_out, c_mu, c_w0, c_w_up, c_a0, c_a_up, c_g_up, c_k_k, c_k_a, c_r_k,
              c_ln_w, c_ln_b, d_q_norm, d_kv_norm, d_w_uq, d_w_ukv):
    u = h @ w_in
    u_c, u_d = u[..., :C_COLS], u[..., C_COLS:]
    prev = jnp.pad(u_c, ((0, 0), (1, 0), (0, 0)))[:, :-1]
    u_c = u_c + c_mu * (prev - u_c)
    yc = rwkv7_time_mix(u_c, c_w0, c_w_up, c_a0, c_a_up, c_g_up, c_k_k, c_k_a, c_r_k, c_ln_w, c_ln_b)
    yd = mla_attention(u_d, d_q_norm, d_kv_norm, d_w_uq, d_w_ukv)
    y = jnp.concatenate([yc, yd.astype(yc.dtype)], axis=-1)
    return y @ w_out


def squared_relu_mlp(h, w1, w2):
    return jnp.square(jax.nn.relu(h @ w1)) @ w2


def setup_inputs(seed: int = 0) -> dict:
    key = jax.random.key(seed)
    ks = jax.random.split(key, 32)
    f32 = jnp.float32
    nrm = lambda k, shape, s: jax.random.normal(k, shape, f32) * s
    return {
        'x': nrm(ks[0], (BATCH, SEQ, D_MODEL), 1.0),
        'c': nrm(ks[1], (BATCH, D_MODEL), 1.0),
        'ada_w': nrm(ks[2], (D_MODEL, 6 * D_MODEL), 0.5 * D_MODEL ** -0.5),
        'ada_table': nrm(ks[3], (DEPTH, 6, D_MODEL), 0.1),
        'norm_mix': 1.0 + nrm(ks[4], (DEPTH, D_MODEL), 0.05),
        'norm_ffn': 1.0 + nrm(ks[5], (DEPTH, D_MODEL), 0.05),
        'norm_final': 1.0 + nrm(ks[6], (D_MODEL,), 0.05),
        'ffn_w1': nrm(ks[7], (DEPTH, D_MODEL, D_FF), D_MODEL ** -0.5),
        'ffn_w2': nrm(ks[8], (DEPTH, D_FF, D_MODEL), D_FF ** -0.5),
        'ev_w_in': nrm(ks[9], (N_EVEN, D_MODEL, EVEN_COLS), D_MODEL ** -0.5),
        'ev_w_out': nrm(ks[10], (N_EVEN, EVEN_OUT, D_MODEL), EVEN_OUT ** -0.5),
        'a_rel_bias': nrm(ks[11], (N_EVEN, A_HEADS, 2 * A_MAX_REL + 1), 0.2),
        'od_w_in': nrm(ks[12], (N_ODD, D_MODEL, ODD_COLS), D_MODEL ** -0.5),
        'od_w_out': nrm(ks[13], (N_ODD, ODD_OUT, D_MODEL), ODD_OUT ** -0.5),
        'c_mu': jax.random.uniform(ks[14], (N_ODD, C_COLS), f32, 0.05, 0.95),
        'c_w0': nrm(ks[15], (N_ODD, C_WIDTH), 0.5),
        'c_w_up': nrm(ks[16], (N_ODD, DECAY_LORA, C_WIDTH), 0.5 * DECAY_LORA ** -0.5),
        'c_a0': nrm(ks[17], (N_ODD, C_WIDTH), 0.5),
        'c_a_up': nrm(ks[18], (N_ODD, ICLR_LORA, C_WIDTH), 0.5 * ICLR_LORA ** -0.5),
        'c_g_up': nrm(ks[19], (N_ODD, GATE_LORA, C_WIDTH), GATE_LORA ** -0.5),
        'c_k_k': 0.85 + nrm(ks[20], (N_ODD, C_WIDTH), 0.1),
        'c_k_a': 1.0 + nrm(ks[21], (N_ODD, C_WIDTH), 0.1),
        'c_r_k': nrm(ks[22], (N_ODD, C_HEADS, C_HEAD_DIM), 0.1),
        'c_ln_w': 1.0 + nrm(ks[23], (N_ODD, C_WIDTH), 0.1),
        'c_ln_b': nrm(ks[24], (N_ODD, C_WIDTH), 0.02),
        'd_q_norm': 1.0 + nrm(ks[25], (N_ODD, Q_LORA), 0.05),
        'd_kv_norm': 1.0 + nrm(ks[26], (N_ODD, KV_LORA), 0.05),
        'd_w_uq': nrm(ks[27], (N_ODD, Q_LORA, D_HEADS * (NOPE_DIM + ROPE_DIM)), Q_LORA ** -0.5),
        'd_w_ukv': nrm(ks[28], (N_ODD, KV_LORA, D_HEADS * (NOPE_DIM + V_DIM)), KV_LORA ** -0.5),
    }


def reference(x, c, ada_w, ada_table, norm_mix, norm_ffn, norm_final, ffn_w1, ffn_w2,
              ev_w_in, ev_w_out, a_rel_bias, od_w_in, od_w_out, c_mu, c_w0, c_w_up, c_a0,
              c_a_up, c_g_up, c_k_k, c_k_a, c_r_k, c_ln_w, c_ln_b, d_q_norm, d_kv_norm,
              d_w_uq, d_w_ukv):
    b = x.shape[0]
    ada = (jax.nn.silu(c) @ ada_w).reshape(b, 6, D_MODEL)
    for layer in range(DEPTH):
        mod = ada + ada_table[layer]
        shift_m, scale_m, gate_m, shift_f, scale_f, gate_f = (mod[:, i, None, :] for i in range(6))
        h = rms_norm(x, norm_mix[layer]) * (1 + scale_m) + shift_m
        if layer % 2 == 0:
            e = layer // 2
            y = even_mixer(h, ev_w_in[e], ev_w_out[e], a_rel_bias[e])
        else:
            o = layer // 2
            y = odd_mixer(h, od_w_in[o], od_w_out[o], c_mu[o], c_w0[o], c_w_up[o], c_a0[o],
                          c_a_up[o], c_g_up[o], c_k_k[o], c_k_a[o], c_r_k[o], c_ln_w[o],
                          c_ln_b[o], d_q_norm[o], d_kv_norm[o], d_w_uq[o], d_w_ukv[o])
        x = x + gate_m * y
        h = rms_norm(x, norm_ffn[layer]) * (1 + scale_f) + shift_f
        x = x + gate_f * squared_relu_mlp(h, ffn_w1[layer], ffn_w2[layer])
    return rms_norm(x, norm_final)
```

```python
import functools

import numpy as np
import jax
import jax.numpy as jnp
from jax import lax
from jax.experimental import pallas as pl
from jax.experimental.pallas import tpu as pltpu

F32 = jnp.float32
BF16 = jnp.bfloat16

CHUNK = 64
Q_BLOCK = 128
HEAD_DIM = 128
EPS = 1e-6
NEG_INF = -1e30
A_LEFT_CHUNKS = 8
A_MAX_REL = 256
IDX_DIM = 64
TOPK_MAX = 256
C_HEAD_DIM = 64
DECAY_LORA = 128
ICLR_LORA = 128
GATE_LORA = 256
C_GN_EPS = 64e-5
NOPE_DIM = 128
ROPE_DIM = 64
V_DIM = 128
ROPE_THETA = 10000.0

V7X_VMEM_LIMIT_BYTES = 56 * 1024 * 1024
LANES = 128
INT32_MIN = -(2 ** 31)


def _cparams(*sem):
    return pltpu.CompilerParams(dimension_semantics=sem, vmem_limit_bytes=V7X_VMEM_LIMIT_BYTES)


def _dot_nt(a, b):
    return lax.dot_general(a, b, (((1,), (1,)), ((), ())), preferred_element_type=F32)


def _dot(a, b):
    return jnp.dot(a, b, preferred_element_type=F32)


def _dot_f32(a, b):
    return jnp.dot(a, b, preferred_element_type=F32, precision=lax.Precision.HIGHEST)


def _pick(n, prefs):
    for p in prefs:
        if n % p == 0:
            return p
    return n


def _mm_kernel(*refs, nk, epilogue):
    if epilogue == "resid":
        a_ref, w_ref, x_ref, g_ref, o_ref = refs[:5]
        scratch = refs[5:]
    else:
        a_ref, w_ref, o_ref = refs[:3]
        scratch = refs[3:]

    def finish(acc):
        if epilogue == "relu2":
            r = jnp.maximum(acc, 0.0)
            acc = r * r
        elif epilogue == "resid":
            acc = x_ref[...] + g_ref[0] * acc
        o_ref[...] = acc.astype(o_ref.dtype)

    part = _dot(a_ref[...], w_ref[...])
    if nk == 1:
        finish(part)
    else:
        acc_ref = scratch[0]
        k = pl.program_id(2)

        @pl.when(k == 0)
        def _():
            acc_ref[...] = part

        @pl.when(k > 0)
        def _():
            acc_ref[...] += part

        @pl.when(k == nk - 1)
        def _():
            finish(acc_ref[...])


def _matmul(a, w, *, out_dtype, epilogue="none", x=None, gate=None, rows_per_batch=None,
            tm=None, tn=None, tk=None):
    m, kd = a.shape
    n = w.shape[1]
    tm = tm or _pick(m, (1024, 512, 256, 128, 64, 32, 16, 8))
    tn = tn or _pick(n, (512, 384, 256, 128))
    tk = tk or (kd if kd <= 4096 else _pick(kd, (2048, 1024, 512)))
    nk = kd // tk
    assert m % tm == 0 and n % tn == 0 and kd % tk == 0
    in_specs = [pl.BlockSpec((tm, tk), lambda i, j, k: (i, k)),
                pl.BlockSpec((tk, tn), lambda i, j, k: (k, j))]
    args = [a, w]
    if epilogue == "resid":
        assert rows_per_batch % tm == 0
        per = rows_per_batch // tm
        in_specs += [pl.BlockSpec((tm, tn), lambda i, j, k: (i, j)),
                     pl.BlockSpec((1, 1, tn), lambda i, j, k: (i // per, 0, j))]
        args += [x, gate.reshape(gate.shape[0], 1, n)]
    return pl.pallas_call(
        functools.partial(_mm_kernel, nk=nk, epilogue=epilogue),
        grid=(m // tm, n // tn, nk),
        in_specs=in_specs,
        out_specs=pl.BlockSpec((tm, tn), lambda i, j, k: (i, j)),
        out_shape=jax.ShapeDtypeStruct((m, n), out_dtype),
        scratch_shapes=[pltpu.VMEM((tm, tn), F32)] if nk > 1 else [],
        compiler_params=_cparams("parallel", "parallel", "arbitrary"),
        name="matmul_" + epilogue,
    )(*args)


def _norm_mod_kernel(x_ref, g_ref, sc_ref, sh_ref, o_ref):
    x = x_ref[...]
    ms = jnp.mean(x * x, axis=-1, keepdims=True)
    y = x * lax.rsqrt(ms + EPS) * g_ref[...]
    o_ref[...] = (y * (1.0 + sc_ref[0]) + sh_ref[0]).astype(o_ref.dtype)


def _norm_plain_kernel(x_ref, g_ref, o_ref):
    x = x_ref[...]
    ms = jnp.mean(x * x, axis=-1, keepdims=True)
    o_ref[...] = (x * lax.rsqrt(ms + EPS) * g_ref[...]).astype(o_ref.dtype)


def _norm_mod(x, g, scale, shift, rows_per_batch, out_dtype=BF16):
    m, d = x.shape
    tr = _pick(rows_per_batch, (256, 128, 64, 32, 16, 8))
    per = rows_per_batch // tr
    b = scale.shape[0]
    return pl.pallas_call(
        _norm_mod_kernel,
        grid=(m // tr,),
        in_specs=[pl.BlockSpec((tr, d), lambda i: (i, 0)),
                  pl.BlockSpec((1, d), lambda i: (0, 0)),
                  pl.BlockSpec((1, 1, d), lambda i: (i // per, 0, 0)),
                  pl.BlockSpec((1, 1, d), lambda i: (i // per, 0, 0))],
        out_specs=pl.BlockSpec((tr, d), lambda i: (i, 0)),
        out_shape=jax.ShapeDtypeStruct((m, d), out_dtype),
        compiler_params=_cparams("parallel"),
        name="norm_mod",
    )(x, g.reshape(1, d), scale.reshape(b, 1, d), shift.reshape(b, 1, d))


def _norm_plain(x, g, out_dtype):
    m, d = x.shape
    tr = _pick(m, (256, 128, 64, 32, 16, 8))
    return pl.pallas_call(
        _norm_plain_kernel,
        grid=(m // tr,),
        in_specs=[pl.BlockSpec((tr, d), lambda i: (i, 0)),
                  pl.BlockSpec((1, d), lambda i: (0, 0))],
        out_specs=pl.BlockSpec((tr, d), lambda i: (i, 0)),
        out_shape=jax.ShapeDtypeStruct((m, d), out_dtype),
        compiler_params=_cparams("parallel"),
        name="norm_plain",
    )(x, g.reshape(1, d))


A_PAD = A_LEFT_CHUNKS * CHUNK
A_WIN = A_PAD + Q_BLOCK
A_VARIANTS = A_PAD // Q_BLOCK + 1


def _band_bias_tiles(rel_bias):
    qi = np.arange(Q_BLOCK)[:, None]
    kj = np.arange(A_WIN)[None, :]
    tiles = []
    for v in range(A_VARIANTS):
        rel = np.clip(qi - kj + v * Q_BLOCK, -A_MAX_REL, A_MAX_REL) + A_MAX_REL
        tiles.append(rel_bias.astype(F32)[:, rel])
    return jnp.stack(tiles, axis=1)


def _attn_a_kernel(q_ref, k_ref, v_ref, b_ref, o_ref, *, scale):
    i = pl.program_id(2)
    ks = pl.multiple_of(jnp.maximum(i * Q_BLOCK - A_PAD, 0), Q_BLOCK)
    q = q_ref[0]
    k = k_ref[0, pl.ds(ks, A_WIN), :]
    v = v_ref[0, pl.ds(ks, A_WIN), :]
    s = _dot_nt(q, k) * scale + b_ref[0, 0]
    qpos = i * Q_BLOCK + lax.broadcasted_iota(jnp.int32, (Q_BLOCK, A_WIN), 0)
    kpos = ks + lax.broadcasted_iota(jnp.int32, (Q_BLOCK, A_WIN), 1)
    chunk_start = (qpos // CHUNK) * CHUNK
    valid = (kpos >= chunk_start - A_PAD) & (kpos < chunk_start + CHUNK)
    s = jnp.where(valid, s, NEG_INF)
    m = jnp.max(s, axis=-1, keepdims=True)
    p = jnp.exp(s - m)
    l = jnp.sum(p, axis=-1, keepdims=True)
    o = _dot(p.astype(v.dtype), v) / l
    o_ref[0] = o.astype(o_ref.dtype)


def _attn_a(u, bias_tiles, heads, q_col, k_col, v_col):
    b, t, _ = u.shape
    assert t % Q_BLOCK == 0 and t >= A_WIN
    nq = t // Q_BLOCK
    return pl.pallas_call(
        functools.partial(_attn_a_kernel, scale=HEAD_DIM ** -0.5),
        grid=(b, heads, nq),
        in_specs=[pl.BlockSpec((1, Q_BLOCK, HEAD_DIM), lambda bi, h, i: (bi, i, q_col + h)),
                  pl.BlockSpec((1, t, HEAD_DIM), lambda bi, h, i: (bi, 0, k_col + h)),
                  pl.BlockSpec((1, t, HEAD_DIM), lambda bi, h, i: (bi, 0, v_col + h)),
                  pl.BlockSpec((1, 1, Q_BLOCK, A_WIN),
                               lambda bi, h, i: (h, jnp.minimum(i, A_VARIANTS - 1), 0, 0))],
        out_specs=pl.BlockSpec((1, Q_BLOCK, HEAD_DIM), lambda bi, h, i: (bi, i, h)),
        out_shape=jax.ShapeDtypeStruct((b, t, heads * HEAD_DIM), BF16),
        compiler_params=_cparams("parallel", "parallel", "arbitrary"),
        name="band_attention",
    )(u, u, u, bias_tiles)


DSA_TK = 512


def _sortable_key(x):
    bits = lax.bitcast_convert_type(x, jnp.int32)
    return jnp.where(bits < 0, bits ^ jnp.int32(0x7FFFFFFF), bits)


def _dsa_kernel(qb_ref, qi_ref, wi_ref, k_ref, v_ref, ki_ref, o_ref,
                key_ref, m_ref, l_ref, acc_ref, *, heads, idx_heads, topk, scale):
    i = pl.program_id(1)
    tk = DSA_TK
    n_kt = (i * Q_BLOCK + Q_BLOCK + tk - 1) // tk
    row = lax.broadcasted_iota(jnp.int32, (Q_BLOCK, 1), 0)
    limit = ((i * Q_BLOCK + row) // CHUNK + 1) * CHUNK

    def score_tile(kt, carry):
        off = pl.multiple_of(kt * tk, tk)
        ki_t = ki_ref[0, pl.ds(off, tk), :]
        acc = jnp.zeros((Q_BLOCK, tk), F32)
        for h in range(idx_heads):
            logits = _dot_nt(qi_ref[0, :, h * IDX_DIM:(h + 1) * IDX_DIM], ki_t)
            acc = acc + wi_ref[0, :, h:h + 1] * jnp.maximum(logits, 0.0)
        kpos = off + lax.broadcasted_iota(jnp.int32, (Q_BLOCK, tk), 1)
        key_ref[:, pl.ds(off, tk)] = _sortable_key(jnp.where(kpos < limit, acc, NEG_INF))
        return carry

    lax.fori_loop(0, n_kt, score_tile, 0)

    def bit_step(bi, t):
        cand = t + lax.shift_left(jnp.int32(1), 31 - bi)

        def count_tile(kt, c):
            off = pl.multiple_of(kt * tk, tk)
            ge = (key_ref[:, pl.ds(off, tk)] >= cand).astype(F32)
            return c + jnp.sum(ge, axis=-1, keepdims=True)

        cnt = lax.fori_loop(0, n_kt, count_tile, jnp.zeros((Q_BLOCK, 1), F32))
        return jnp.where(cnt >= float(topk), cand, t)

    thr = lax.fori_loop(0, 32, bit_step, jnp.full((Q_BLOCK, 1), INT32_MIN, jnp.int32))

    m_ref[...] = jnp.full(m_ref.shape, NEG_INF, F32)
    l_ref[...] = jnp.zeros(l_ref.shape, F32)
    acc_ref[...] = jnp.zeros(acc_ref.shape, F32)

    def attend_tile(kt, carry):
        off = pl.multiple_of(kt * tk, tk)
        k_t = k_ref[0, pl.ds(off, tk), :]
        v_t = v_ref[0, pl.ds(off, tk), :]
        kpos = off + lax.broadcasted_iota(jnp.int32, (Q_BLOCK, tk), 1)
        sel = (key_ref[:, pl.ds(off, tk)] >= thr) & (kpos < limit)
        for h in range(heads):
            q_h = qb_ref[0, :, h * HEAD_DIM:(h + 1) * HEAD_DIM]
            s = jnp.where(sel, _dot_nt(q_h, k_t) * scale, NEG_INF)
            m_old = m_ref[h]
            m_new = jnp.maximum(m_old, jnp.max(s, axis=-1, keepdims=True))
            alpha = jnp.exp(m_old - m_new)
            p = jnp.exp(s - m_new)
            l_ref[h] = alpha * l_ref[h] + jnp.sum(p, axis=-1, keepdims=True)
            acc_ref[h] = alpha * acc_ref[h] + _dot(p.astype(v_t.dtype), v_t)
            m_ref[h] = m_new
        return carry

    lax.fori_loop(0, n_kt, attend_tile, 0)
    for h in range(heads):
        o_ref[0, :, h * HEAD_DIM:(h + 1) * HEAD_DIM] = (acc_ref[h] / l_ref[h]).astype(o_ref.dtype)


def _dsa(u_q, wi, kb, vb, ki, heads, idx_heads):
    b, t, _ = u_q.shape
    assert heads * HEAD_DIM == idx_heads * IDX_DIM
    qw = heads * HEAD_DIM
    topk = min(TOPK_MAX, t // 4)
    assert t % DSA_TK == 0 and topk <= DSA_TK
    nq = t // Q_BLOCK
    return pl.pallas_call(
        functools.partial(_dsa_kernel, heads=heads, idx_heads=idx_heads, topk=topk,
                          scale=HEAD_DIM ** -0.5),
        grid=(b, nq),
        in_specs=[pl.BlockSpec((1, Q_BLOCK, qw), lambda bi, i: (bi, i, 0)),
                  pl.BlockSpec((1, Q_BLOCK, qw), lambda bi, i: (bi, i, 1)),
                  pl.BlockSpec((1, Q_BLOCK, idx_heads), lambda bi, i: (bi, i, 0)),
                  pl.BlockSpec((1, t, HEAD_DIM), lambda bi, i: (bi, 0, 0)),
                  pl.BlockSpec((1, t, HEAD_DIM), lambda bi, i: (bi, 0, 0)),
                  pl.BlockSpec((1, t, IDX_DIM), lambda bi, i: (bi, 0, 0))],
        out_specs=pl.BlockSpec((1, Q_BLOCK, qw), lambda bi, i: (bi, i, 0)),
        out_shape=jax.ShapeDtypeStruct((b, t, qw), BF16),
        scratch_shapes=[pltpu.VMEM((Q_BLOCK, t), jnp.int32),
                        pltpu.VMEM((heads, Q_BLOCK, 1), F32),
                        pltpu.VMEM((heads, Q_BLOCK, 1), F32),
                        pltpu.VMEM((heads, Q_BLOCK, HEAD_DIM), F32)],
        compiler_params=_cparams("parallel", "arbitrary"),
        name="dsa_attention",
    )(u_q, u_q, wi, kb, vb, ki)


def _shifted(x, prev_row, first):
    rolled = pltpu.roll(x, 1, axis=0)
    head = jnp.where(first, jnp.zeros_like(prev_row), prev_row)
    rid = lax.broadcasted_iota(jnp.int32, x.shape, 0)
    return jnp.where(rid == 0, head, rolled)


def _sigmoid(x):
    return 1.0 / (1.0 + jnp.exp(-x))


def _rwkv_prep_kernel(rkv_ref, rkv_prev_ref, lo_ref, lo_prev_ref, mu_rkv_ref, mu_lo_ref,
                      w0_ref, a0_ref, wup_ref, aup_ref, gup_ref,
                      r_ref, k_ref, v_ref, lw_ref, a_ref, g_ref, *, blocks_per_seq, width):
    first = (pl.program_id(0) % blocks_per_seq) == 0
    x = rkv_ref[...]
    prev = _shifted(x, rkv_prev_ref[7:8, :], first)
    x = x + mu_rkv_ref[...] * (prev - x)
    r_ref[...] = x[:, :width]
    k_ref[...] = x[:, width:2 * width]
    v_ref[...] = x[:, 2 * width:]

    lo = lo_ref[...]
    lo_prev = _shifted(lo, lo_prev_ref[7:8, :], first)
    lo = lo + mu_lo_ref[...] * (lo_prev - lo)
    xw = lo[:, :DECAY_LORA]
    xa = lo[:, DECAY_LORA:DECAY_LORA + ICLR_LORA]
    xg = lo[:, DECAY_LORA + ICLR_LORA:]
    z = -(w0_ref[...] + _dot(jnp.tanh(xw).astype(BF16), wup_ref[...]))
    softplus = jnp.maximum(z, 0.0) + jnp.log(1.0 + jnp.exp(-jnp.abs(z)))
    logw = -softplus - 0.5
    lw_ref[...] = -jnp.exp(logw)
    a_ref[...] = _sigmoid(a0_ref[...] + _dot(xa.astype(BF16), aup_ref[...]))
    g_ref[...] = _dot(_sigmoid(xg).astype(BF16), gup_ref[...])


def _rwkv_prep(u_rkv, u_lo, mu, w0, w_up, a0, a_up, g_up, rows_per_batch):
    m, w3 = u_rkv.shape
    width = w3 // 3
    lo_w = DECAY_LORA + ICLR_LORA + GATE_LORA
    tb = 128
    assert rows_per_batch % tb == 0
    blocks_per_seq = rows_per_batch // tb
    sub = tb // 8
    prev_map = lambda i: (jnp.maximum(i * sub - 1, 0), 0)
    row = lambda i: (i, 0)
    fixed = lambda i: (0, 0)
    out = jax.ShapeDtypeStruct((m, width), F32)
    return pl.pallas_call(
        functools.partial(_rwkv_prep_kernel, blocks_per_seq=blocks_per_seq, width=width),
        grid=(m // tb,),
        in_specs=[pl.BlockSpec((tb, w3), row),
                  pl.BlockSpec((8, w3), prev_map),
                  pl.BlockSpec((tb, lo_w), row),
                  pl.BlockSpec((8, lo_w), prev_map),
                  pl.BlockSpec((1, w3), fixed),
                  pl.BlockSpec((1, lo_w), fixed),
                  pl.BlockSpec((1, width), fixed),
                  pl.BlockSpec((1, width), fixed),
                  pl.BlockSpec((DECAY_LORA, width), fixed),
                  pl.BlockSpec((ICLR_LORA, width), fixed),
                  pl.BlockSpec((GATE_LORA, width), fixed)],
        out_specs=[pl.BlockSpec((tb, width), row)] * 6,
        out_shape=[out] * 6,
        compiler_params=_cparams("parallel"),
        name="rwkv_prep",
    )(u_rkv, u_rkv, u_lo, u_lo, mu[:w3].reshape(1, w3), mu[w3:].reshape(1, lo_w),
      w0.reshape(1, width), a0.reshape(1, width), w_up.astype(BF16), a_up.astype(BF16),
      g_up.astype(BF16))


WKV_L = 64
WKV_HEADS = 8


def _wkv_kernel(r_ref, k_ref, v_ref, lw_ref, a_ref, g_ref, kk_ref, ka_ref, rk_ref, lnw_ref,
                lnb_ref, o_ref, state_ref):
    L = WKV_L
    n = C_HEAD_DIM

    @pl.when(pl.program_id(2) == 0)
    def _():
        state_ref[...] = jnp.zeros(state_ref.shape, F32)

    ti = lax.broadcasted_iota(jnp.int32, (L, L), 0)
    si = lax.broadcasted_iota(jnp.int32, (L, L), 1)
    lower_strict = ti > si
    tri_ones = (ti >= si).astype(F32)
    eye = (ti == si).astype(F32)
    ti2 = lax.broadcasted_iota(jnp.int32, (L, 2 * L), 0)
    si2 = lax.broadcasted_iota(jnp.int32, (L, 2 * L), 1)
    lower_incl2 = ti2 >= jnp.where(si2 >= L, si2 - L, si2)

    for j in range(WKV_HEADS):
        sl = slice(j * n, (j + 1) * n)
        r = r_ref[0, :, sl]
        k = k_ref[0, :, sl]
        v = v_ref[0, :, sl]
        lw = lw_ref[0, :, sl]
        a_lr = a_ref[0, :, sl]
        kk = k * kk_ref[:, sl]
        kk = kk / jnp.maximum(jnp.sqrt(jnp.sum(kk * kk, axis=-1, keepdims=True)), 1e-12)
        k = k * (1.0 + (a_lr - 1.0) * ka_ref[:, sl])
        a_s = -kk
        b_s = kk * a_lr

        cum = _dot_f32(tri_ones, lw)
        c_in = jnp.exp(cum)
        c_ex = jnp.exp(cum - lw)
        c_inv = jnp.exp(-cum)
        c_out = jnp.exp(cum[L - 1:L, :] - cum)
        ar = jnp.concatenate([a_s * c_ex, r * c_in], axis=0).astype(BF16)
        bk = jnp.concatenate([b_s * c_inv, k * c_inv], axis=0).astype(BF16)
        bk_out = jnp.concatenate([b_s * c_out, k * c_out], axis=0).astype(BF16)
        v_b = v.astype(BF16)

        prod = _dot_nt(ar, bk)
        n_ab = jnp.where(lower_strict, prod[:L, :L], 0.0)
        n_ak = jnp.where(lower_strict, prod[:L, L:], 0.0)
        m_r = jnp.where(lower_incl2, prod[L:, :], 0.0)

        inv = eye + n_ab
        pw = n_ab
        span = 1
        while 2 * span < L:
            pw_b = pw.astype(BF16)
            pw = _dot(pw_b, pw_b)
            inv = inv + _dot(inv.astype(BF16), pw.astype(BF16))
            span *= 2

        s0 = state_ref[j]
        ah = _dot_nt(ar, s0.astype(BF16))
        rhs = ah[:L] + _dot(n_ak.astype(BF16), v_b)
        u = _dot(inv.astype(BF16), rhs.astype(BF16))
        uv = jnp.concatenate([u.astype(BF16), v_b], axis=0)
        y = ah[L:] + _dot(m_r.astype(BF16), uv)
        s_inc = lax.dot_general(uv, bk_out, (((0,), (0,)), ((), ())),
                                preferred_element_type=F32)
        state_ref[j] = s0 * jnp.exp(cum[L - 1:L, :]) + s_inc

        mean = jnp.mean(y, axis=-1, keepdims=True)
        yc = y - mean
        var = jnp.mean(yc * yc, axis=-1, keepdims=True)
        yn = yc * lax.rsqrt(var + C_GN_EPS) * lnw_ref[:, sl] + lnb_ref[:, sl]
        bonus = jnp.sum(r * k * rk_ref[:, sl], axis=-1, keepdims=True) * v
        o_ref[0, :, sl] = ((yn + bonus) * g_ref[0, :, sl]).astype(o_ref.dtype)


def _wkv(r, k, v, lw, a, g, k_k, k_a, r_k, ln_w, ln_b, batch):
    m, width = r.shape
    t = m // batch
    hb = WKV_HEADS * C_HEAD_DIM
    assert t % WKV_L == 0 and width % hb == 0
    seq = lambda z: z.reshape(batch, t, width)
    par = lambda z: z.reshape(1, width).astype(F32)
    tok = pl.BlockSpec((1, WKV_L, hb), lambda bi, hg, c: (bi, c, hg))
    vec = pl.BlockSpec((1, hb), lambda bi, hg, c: (0, hg))
    out = pl.pallas_call(
        _wkv_kernel,
        grid=(batch, width // hb, t // WKV_L),
        in_specs=[tok] * 6 + [vec] * 5,
        out_specs=tok,
        out_shape=jax.ShapeDtypeStruct((batch, t, width), BF16),
        scratch_shapes=[pltpu.VMEM((WKV_HEADS, C_HEAD_DIM, C_HEAD_DIM), F32)],
        compiler_params=_cparams("parallel", "parallel", "arbitrary"),
        name="wkv7_chunked",
    )(seq(r), seq(k), seq(v), seq(lw), seq(a), seq(g), par(k_k), par(k_a), par(r_k),
      par(ln_w), par(ln_b))
    return out.reshape(m, width)


def _rope_tables(t):
    half = ROPE_DIM // 2
    freqs = ROPE_THETA ** (-jnp.arange(half, dtype=F32) / half)
    ang = jnp.arange(t, dtype=F32)[:, None] * freqs[None, :]
    cos, sin = jnp.cos(ang), jnp.sin(ang)
    zeros = jnp.zeros((t, LANES - ROPE_DIM), F32)
    return (jnp.concatenate([cos, cos, zeros], axis=-1),
            jnp.concatenate([-sin, sin, zeros], axis=-1))


def _rope128(x, cos, sin):
    half = ROPE_DIM // 2
    swapped = pltpu.roll(x, LANES - half, axis=1) + pltpu.roll(x, half, axis=1)
    return x * cos + swapped * sin


def _mla_prep_kernel(cq_ref, ckv_ref, kpe_ref, qn_ref, kvn_ref, cos_ref, sin_ref,
                     cq_o, ckv_o, kpe_o):
    cq = cq_ref[...]
    cq_o[...] = (cq * lax.rsqrt(jnp.mean(cq * cq, axis=-1, keepdims=True) + EPS)
                 * qn_ref[...]).astype(cq_o.dtype)
    ckv = ckv_ref[...]
    ckv_o[...] = (ckv * lax.rsqrt(jnp.mean(ckv * ckv, axis=-1, keepdims=True) + EPS)
                  * kvn_ref[...]).astype(ckv_o.dtype)
    lane = lax.broadcasted_iota(jnp.int32, kpe_ref.shape, 1)
    kpe = jnp.where(lane < ROPE_DIM, kpe_ref[...], 0.0)
    kpe_o[...] = _rope128(kpe, cos_ref[...], sin_ref[...]).astype(kpe_o.dtype)


def _mla_prep(u_lo, q_norm, kv_norm, cos, sin, q_lora, kv_lora, col0, rows_per_batch):
    m = u_lo.shape[0]
    tb = _pick(rows_per_batch, (256, 128, 64, 32, 16, 8))
    per = rows_per_batch // tb
    row = lambda i: (i, 0)
    fixed = lambda i: (0, 0)
    return pl.pallas_call(
        _mla_prep_kernel,
        grid=(m // tb,),
        in_specs=[pl.BlockSpec((tb, q_lora), row),
                  pl.BlockSpec((tb, kv_lora), row),
                  pl.BlockSpec((tb, LANES), row),
                  pl.BlockSpec((1, q_lora), fixed),
                  pl.BlockSpec((1, kv_lora), fixed),
                  pl.BlockSpec((tb, LANES), lambda i: (i % per, 0)),
                  pl.BlockSpec((tb, LANES), lambda i: (i % per, 0))],
        out_specs=[pl.BlockSpec((tb, q_lora), row),
                   pl.BlockSpec((tb, kv_lora), row),
                   pl.BlockSpec((tb, LANES), row)],
        out_shape=[jax.ShapeDtypeStruct((m, q_lora), BF16),
                   jax.ShapeDtypeStruct((m, kv_lora), BF16),
                   jax.ShapeDtypeStruct((m, LANES), BF16)],
        compiler_params=_cparams("parallel"),
        name="mla_prep",
    )(u_lo[:, col0:col0 + q_lora], u_lo[:, col0 + q_lora:col0 + q_lora + kv_lora],
      u_lo[:, col0 + q_lora + kv_lora:col0 + q_lora + kv_lora + LANES],
      q_norm.reshape(1, q_lora), kv_norm.reshape(1, kv_lora), cos, sin)


MLA_TQ = 128
MLA_TK = 512


def _mla_kernel(q_ref, kv_ref, kpe_ref, cos_ref, sin_ref, o_ref, m_ref, l_ref, acc_ref, *, scale):
    i = pl.program_id(2)
    tq, tk = MLA_TQ, MLA_TK
    n_kt = (i * tq + tq + tk - 1) // tk
    row = lax.broadcasted_iota(jnp.int32, (tq, 1), 0)
    limit = ((i * tq + row) // CHUNK + 1) * CHUNK
    q_nope = q_ref[0, :, :NOPE_DIM]
    q_pe = _rope128(q_ref[0, :, NOPE_DIM:].astype(F32), cos_ref[...], sin_ref[...]).astype(BF16)

    m_ref[...] = jnp.full(m_ref.shape, NEG_INF, F32)
    l_ref[...] = jnp.zeros(l_ref.shape, F32)
    acc_ref[...] = jnp.zeros(acc_ref.shape, F32)

    def tile(kt, carry):
        off = pl.multiple_of(kt * tk, tk)
        k_nope = kv_ref[0, pl.ds(off, tk), :NOPE_DIM]
        v_t = kv_ref[0, pl.ds(off, tk), NOPE_DIM:]
        k_pe = kpe_ref[0, pl.ds(off, tk), :]
        s = (_dot_nt(q_nope, k_nope) + _dot_nt(q_pe, k_pe)) * scale
        kpos = off + lax.broadcasted_iota(jnp.int32, (tq, tk), 1)
        s = jnp.where(kpos < limit, s, NEG_INF)
        m_old = m_ref[...]
        m_new = jnp.maximum(m_old, jnp.max(s, axis=-1, keepdims=True))
        alpha = jnp.exp(m_old - m_new)
        p = jnp.exp(s - m_new)
        l_ref[...] = alpha * l_ref[...] + jnp.sum(p, axis=-1, keepdims=True)
        acc_ref[...] = alpha * acc_ref[...] + _dot(p.astype(v_t.dtype), v_t)
        m_ref[...] = m_new
        return carry

    lax.fori_loop(0, n_kt, tile, 0)
    o_ref[0] = (acc_ref[...] / l_ref[...]).astype(o_ref.dtype)


def _mla(q, kv, kpe, cos, sin, heads):
    b, t, _ = q.shape
    assert t % MLA_TK == 0
    hw = NOPE_DIM + LANES
    return pl.pallas_call(
        functools.partial(_mla_kernel, scale=(NOPE_DIM + ROPE_DIM) ** -0.5),
        grid=(b, heads, t // MLA_TQ),
        in_specs=[pl.BlockSpec((1, MLA_TQ, hw), lambda bi, h, i: (bi, i, h)),
                  pl.BlockSpec((1, t, NOPE_DIM + V_DIM), lambda bi, h, i: (bi, 0, h)),
                  pl.BlockSpec((1, t, LANES), lambda bi, h, i: (bi, 0, 0)),
                  pl.BlockSpec((MLA_TQ, LANES), lambda bi, h, i: (i, 0)),
                  pl.BlockSpec((MLA_TQ, LANES), lambda bi, h, i: (i, 0))],
        out_specs=pl.BlockSpec((1, MLA_TQ, V_DIM), lambda bi, h, i: (bi, i, h)),
        out_shape=jax.ShapeDtypeStruct((b, t, heads * V_DIM), BF16),
        scratch_shapes=[pltpu.VMEM((MLA_TQ, 1), F32),
                        pltpu.VMEM((MLA_TQ, 1), F32),
                        pltpu.VMEM((MLA_TQ, V_DIM), F32)],
        compiler_params=_cparams("parallel", "parallel", "arbitrary"),
        name="mla_attention",
    )(q, kv, kpe, cos, sin)


def _pad_cols(w, n):
    return jnp.pad(w, ((0, 0), (0, n - w.shape[1])))


def _even_mixer(h, x, gate, w_in, w_out, rel_bias, batch):
    m, d = h.shape
    t = m // batch
    heads = d // (2 * HEAD_DIM)
    idx_heads = d // 128
    hw = heads * HEAD_DIM
    o_kb = 4 * hw
    o_qi = o_kb + 2 * HEAD_DIM
    o_ki = o_qi + idx_heads * IDX_DIM
    w_in = w_in.astype(BF16)
    u_a = _matmul(h, w_in[:, :3 * hw], out_dtype=BF16)
    u_q = _matmul(h, jnp.concatenate([w_in[:, 3 * hw:4 * hw], w_in[:, o_qi:o_ki]], axis=1),
                  out_dtype=BF16)
    small = jnp.concatenate([w_in[:, o_kb:o_qi], w_in[:, o_ki:]], axis=1)
    u_s = _matmul(h, _pad_cols(small, 3 * LANES), out_dtype=F32, tn=3 * LANES)
    u_s = u_s.reshape(batch, t, 3 * LANES)
    kb = u_s[..., :HEAD_DIM].astype(BF16)
    vb = u_s[..., HEAD_DIM:2 * HEAD_DIM].astype(BF16)
    ki = u_s[..., 2 * HEAD_DIM:2 * HEAD_DIM + IDX_DIM].astype(BF16)
    wi = u_s[..., 2 * HEAD_DIM + IDX_DIM:2 * HEAD_DIM + IDX_DIM + idx_heads]
    ya = _attn_a(u_a.reshape(batch, t, 3 * hw), _band_bias_tiles(rel_bias), heads,
                 0, heads, 2 * heads)
    yb = _dsa(u_q.reshape(batch, t, 2 * hw), wi, kb, vb, ki, heads, idx_heads)
    y = jnp.concatenate([ya, yb], axis=-1).reshape(m, 2 * hw)
    return _matmul(y, w_out.astype(BF16), out_dtype=F32, epilogue="resid", x=x, gate=gate,
                   rows_per_batch=t)


def _odd_mixer(h, x, gate, w_in, w_out, c_mu, c_w0, c_w_up, c_a0, c_a_up, c_g_up, c_k_k, c_k_a,
               c_r_k, c_ln_w, c_ln_b, d_q_norm, d_kv_norm, d_w_uq, d_w_ukv, batch):
    m, d = h.shape
    t = m // batch
    width = c_w0.shape[0]
    heads = d // (2 * HEAD_DIM)
    q_lora = d_q_norm.shape[0]
    kv_lora = d_kv_norm.shape[0]
    lo_w = DECAY_LORA + ICLR_LORA + GATE_LORA
    w_in = w_in.astype(BF16)
    u_rkv = _matmul(h, w_in[:, :3 * width], out_dtype=F32)
    rest = w_in[:, 3 * width:]
    rest_cols = -(-(rest.shape[1] + LANES - ROPE_DIM) // 512) * 512
    u_lo = _matmul(h, _pad_cols(rest, rest_cols), out_dtype=F32)

    r, k, v, lw, a, g = _rwkv_prep(u_rkv, u_lo, c_mu, c_w0, c_w_up, c_a0, c_a_up, c_g_up, t)
    yc = _wkv(r, k, v, lw, a, g, c_k_k, c_k_a, c_r_k, c_ln_w, c_ln_b, batch)

    cos, sin = _rope_tables(t)
    cq, ckv, kpe = _mla_prep(u_lo, d_q_norm, d_kv_norm, cos, sin, q_lora, kv_lora, lo_w, t)
    w_uq = d_w_uq.reshape(q_lora, heads, NOPE_DIM + ROPE_DIM)
    w_uq = jnp.pad(w_uq, ((0, 0), (0, 0), (0, LANES - ROPE_DIM))).reshape(q_lora, -1)
    q = _matmul(cq, w_uq.astype(BF16), out_dtype=BF16)
    kv = _matmul(ckv, d_w_ukv.astype(BF16), out_dtype=BF16)
    yd = _mla(q.reshape(batch, t, -1), kv.reshape(batch, t, -1), kpe.reshape(batch, t, LANES),
              cos, sin, heads)
    y = jnp.concatenate([yc, yd.reshape(m, -1)], axis=-1)
    return _matmul(y, w_out.astype(BF16), out_dtype=F32, epilogue="resid", x=x, gate=gate,
                   rows_per_batch=t)


def kernel(x, c, ada_w, ada_table, norm_mix, norm_ffn, norm_final, ffn_w1, ffn_w2, ev_w_in,
           ev_w_out, a_rel_bias, od_w_in, od_w_out, c_mu, c_w0, c_w_up, c_a0, c_a_up, c_g_up,
           c_k_k, c_k_a, c_r_k, c_ln_w, c_ln_b, d_q_norm, d_kv_norm, d_w_uq, d_w_ukv):
    batch, t, d = x.shape
    depth = ada_table.shape[0]
    m = batch * t
    silu_c = (c * _sigmoid(c)).astype(BF16)
    silu_c = jnp.pad(silu_c, ((0, 8 - batch % 8 if batch % 8 else 0), (0, 0)))
    ada = _matmul(silu_c, ada_w.astype(BF16), out_dtype=F32, tn=2048)[:batch].reshape(batch, 6, d)
    x = x.reshape(m, d)
    for layer in range(depth):
        mod = ada + ada_table[layer]
        shift_m, scale_m, gate_m, shift_f, scale_f, gate_f = (mod[:, i, :] for i in range(6))
        h = _norm_mod(x, norm_mix[layer], scale_m, shift_m, t)
        if layer % 2 == 0:
            e = layer // 2
            x = _even_mixer(h, x, gate_m, ev_w_in[e], ev_w_out[e], a_rel_bias[e], batch)
        else:
            o = layer // 2
            x = _odd_mixer(h, x, gate_m, od_w_in[o], od_w_out[o], c_mu[o], c_w0[o], c_w_up[o],
                           c_a0[o], c_a_up[o], c_g_up[o], c_k_k[o], c_k_a[o], c_r_k[o],
                           c_ln_w[o], c_ln_b[o], d_q_norm[o], d_kv_norm[o], d_w_uq[o],
                           d_w_ukv[o], batch)
        h = _norm_mod(x, norm_ffn[layer], scale_f, shift_f, t)
        h1 = _matmul(h, ffn_w1[layer].astype(BF16), out_dtype=BF16, epilogue="relu2")
        x = _matmul(h1, ffn_w2[layer].astype(BF16), out_dtype=F32, epilogue="resid", x=x,
                    gate=gate_f, rows_per_batch=t)
    return _norm_plain(x, norm_final, F32).reshape(batch, t, d)
```

```python
import functools

import numpy as np
import jax
import jax.numpy as jnp
from jax import lax
from jax.experimental import pallas as pl
from jax.experimental.pallas import tpu as pltpu

F32 = jnp.float32
BF16 = jnp.bfloat16

CHUNK = 64
Q_BLOCK = 128
HEAD_DIM = 128
EPS = 1e-6
NEG_INF = -1e30
A_LEFT_CHUNKS = 8
A_MAX_REL = 256
IDX_DIM = 64
TOPK_MAX = 256
C_HEAD_DIM = 64
DECAY_LORA = 128
ICLR_LORA = 128
GATE_LORA = 256
C_GN_EPS = 64e-5
NOPE_DIM = 128
ROPE_DIM = 64
V_DIM = 128
ROPE_THETA = 10000.0

V7X_VMEM_LIMIT_BYTES = 56 * 1024 * 1024
LANES = 128
INT32_MIN = -(2 ** 31)
LOG2_E = 1.4426950408889634


def _cparams(*sem):
    return pltpu.CompilerParams(dimension_semantics=sem, vmem_limit_bytes=V7X_VMEM_LIMIT_BYTES)


def _dot_nt(a, b):
    return lax.dot_general(a, b, (((1,), (1,)), ((), ())), preferred_element_type=F32)


def _dot(a, b):
    return jnp.dot(a, b, preferred_element_type=F32)


def _dot_f32(a, b):
    return jnp.dot(a, b, preferred_element_type=F32, precision=lax.Precision.HIGHEST)


FLASH_ROWS = 128


def _flash_update(q_ref, row_starts, k_t, v_t, mask, m_ref, l_ref, acc_ref):
    tk = k_t.shape[0]
    reps = tk // LANES
    rows = [slice(r, r + FLASH_ROWS) for r in row_starts]
    s = [_dot_nt(q_ref[r, :], k_t) for r in rows]
    if mask is not None:
        masks = mask if isinstance(mask, list) else [mask] * len(rows)
        s = [jnp.where(mk, x, NEG_INF) for mk, x in zip(masks, s)]
    m_old = [m_ref[r, :] for r in rows]
    m_new = [jnp.maximum(mo, jnp.max(x, axis=-1, keepdims=True)) for mo, x in zip(m_old, s)]
    p = [jnp.exp2(x - jnp.concatenate([mn] * reps, axis=1)) for x, mn in zip(s, m_new)]
    alpha = [jnp.exp2(mo - mn) for mo, mn in zip(m_old, m_new)]
    pv = [_dot(x.astype(v_t.dtype), v_t) for x in p]
    for r, mn, al, x, y in zip(rows, m_new, alpha, p, pv):
        m_ref[r, :] = mn
        l_ref[r, :] = al * l_ref[r, :] + jnp.sum(x, axis=-1, keepdims=True)
        acc_ref[r, :] = al * acc_ref[r, :] + y


def _pick(n, prefs):
    for p in prefs:
        if n % p == 0:
            return p
    return n


def _mm_kernel(*refs, nk, epilogue):
    if epilogue == "resid":
        a_ref, w_ref, x_ref, g_ref, o_ref = refs[:5]
        scratch = refs[5:]
    else:
        a_ref, w_ref, o_ref = refs[:3]
        scratch = refs[3:]

    def finish(acc):
        if epilogue == "relu2":
            r = jnp.maximum(acc, 0.0)
            acc = r * r
        elif epilogue == "resid":
            acc = x_ref[...] + g_ref[0] * acc
        o_ref[...] = acc.astype(o_ref.dtype)

    part = _dot(a_ref[...], w_ref[...])
    if nk == 1:
        finish(part)
    else:
        acc_ref = scratch[0]
        k = pl.program_id(2)

        @pl.when(k == 0)
        def _():
            acc_ref[...] = part

        @pl.when(k > 0)
        def _():
            acc_ref[...] += part

        @pl.when(k == nk - 1)
        def _():
            finish(acc_ref[...])


def _matmul(a, w, *, out_dtype, epilogue="none", x=None, gate=None, rows_per_batch=None,
            tm=None, tn=None, tk=None):
    m, kd = a.shape
    n = w.shape[1]
    tm = tm or _pick(m, (1024, 512, 256, 128, 64, 32, 16, 8))
    tk = tk or (kd if kd <= 4096 else _pick(kd, (2048, 1024, 512)))
    tn = tn or _pick(n, (1024, 512, 384, 256, 128) if tk < kd else (512, 384, 256, 128))
    nk = kd // tk
    assert m % tm == 0 and n % tn == 0 and kd % tk == 0
    in_specs = [pl.BlockSpec((tm, tk), lambda i, j, k: (i, k)),
                pl.BlockSpec((tk, tn), lambda i, j, k: (k, j))]
    args = [a, w]
    if epilogue == "resid":
        assert rows_per_batch % tm == 0
        per = rows_per_batch // tm
        in_specs += [pl.BlockSpec((tm, tn), lambda i, j, k: (i, j)),
                     pl.BlockSpec((1, 1, tn), lambda i, j, k: (i // per, 0, j))]
        args += [x, gate.reshape(gate.shape[0], 1, n)]
    return pl.pallas_call(
        functools.partial(_mm_kernel, nk=nk, epilogue=epilogue),
        grid=(m // tm, n // tn, nk),
        in_specs=in_specs,
        out_specs=pl.BlockSpec((tm, tn), lambda i, j, k: (i, j)),
        out_shape=jax.ShapeDtypeStruct((m, n), out_dtype),
        scratch_shapes=[pltpu.VMEM((tm, tn), F32)] if nk > 1 else [],
        compiler_params=_cparams("parallel", "parallel", "arbitrary"),
        name="matmul_" + epilogue,
    )(*args)


def _norm_mod_kernel(x_ref, g_ref, sc_ref, sh_ref, o_ref):
    x = x_ref[...]
    ms = jnp.mean(x * x, axis=-1, keepdims=True)
    y = x * lax.rsqrt(ms + EPS) * g_ref[...]
    o_ref[...] = (y * (1.0 + sc_ref[0]) + sh_ref[0]).astype(o_ref.dtype)


def _norm_plain_kernel(x_ref, g_ref, o_ref):
    x = x_ref[...]
    ms = jnp.mean(x * x, axis=-1, keepdims=True)
    o_ref[...] = (x * lax.rsqrt(ms + EPS) * g_ref[...]).astype(o_ref.dtype)


def _norm_mod(x, g, scale, shift, rows_per_batch, out_dtype=BF16):
    m, d = x.shape
    tr = _pick(rows_per_batch, (256, 128, 64, 32, 16, 8))
    per = rows_per_batch // tr
    b = scale.shape[0]
    return pl.pallas_call(
        _norm_mod_kernel,
        grid=(m // tr,),
        in_specs=[pl.BlockSpec((tr, d), lambda i: (i, 0)),
                  pl.BlockSpec((1, d), lambda i: (0, 0)),
                  pl.BlockSpec((1, 1, d), lambda i: (i // per, 0, 0)),
                  pl.BlockSpec((1, 1, d), lambda i: (i // per, 0, 0))],
        out_specs=pl.BlockSpec((tr, d), lambda i: (i, 0)),
        out_shape=jax.ShapeDtypeStruct((m, d), out_dtype),
        compiler_params=_cparams("parallel"),
        name="norm_mod",
    )(x, g.reshape(1, d), scale.reshape(b, 1, d), shift.reshape(b, 1, d))


def _norm_plain(x, g, out_dtype):
    m, d = x.shape
    tr = _pick(m, (256, 128, 64, 32, 16, 8))
    return pl.pallas_call(
        _norm_plain_kernel,
        grid=(m // tr,),
        in_specs=[pl.BlockSpec((tr, d), lambda i: (i, 0)),
                  pl.BlockSpec((1, d), lambda i: (0, 0))],
        out_specs=pl.BlockSpec((tr, d), lambda i: (i, 0)),
        out_shape=jax.ShapeDtypeStruct((m, d), out_dtype),
        compiler_params=_cparams("parallel"),
        name="norm_plain",
    )(x, g.reshape(1, d))


A_PAD = A_LEFT_CHUNKS * CHUNK
A_WIN = A_PAD + Q_BLOCK
A_VARIANTS = A_PAD // Q_BLOCK + 1


def _band_bias_tiles(rel_bias):
    width = A_WIN + Q_BLOCK
    tiles = []
    for v in range(A_VARIANTS):
        rel = np.clip(v * Q_BLOCK + Q_BLOCK - 1 - np.arange(width), -A_MAX_REL, A_MAX_REL)
        diag = jnp.pad(rel_bias.astype(F32)[:, rel + A_MAX_REL] * LOG2_E, ((0, 0), (0, 1)))
        skew = jnp.tile(diag, (1, Q_BLOCK))[:, :Q_BLOCK * width].reshape(-1, Q_BLOCK, width)
        tiles.append(skew[:, :, Q_BLOCK - 1:Q_BLOCK - 1 + A_WIN])
    return jnp.stack(tiles, axis=1)


A_HEADS_PER_STEP = 4


def _attn_a_kernel(q_ref, k_ref, v_ref, b_ref, o_ref):
    i = pl.program_id(2)
    ks = pl.multiple_of(jnp.maximum(i * Q_BLOCK - A_PAD, 0), Q_BLOCK)
    qpos = i * Q_BLOCK + lax.broadcasted_iota(jnp.int32, (Q_BLOCK, A_WIN), 0)
    kpos = ks + lax.broadcasted_iota(jnp.int32, (Q_BLOCK, A_WIN), 1)
    chunk_start = (qpos // CHUNK) * CHUNK
    valid = (kpos >= chunk_start - A_PAD) & (kpos < chunk_start + CHUNK)
    cols = [slice(h * HEAD_DIM, (h + 1) * HEAD_DIM) for h in range(A_HEADS_PER_STEP)]
    vs = [v_ref[0, pl.ds(ks, A_WIN), c] for c in cols]
    s = [_dot_nt(q_ref[0, :, c], k_ref[0, pl.ds(ks, A_WIN), c]) + b_ref[h, 0]
         for h, c in enumerate(cols)]
    s = [jnp.where(valid, x, NEG_INF) for x in s]
    p = [jnp.exp2(x - jnp.max(x, axis=-1, keepdims=True)) for x in s]
    l = [jnp.sum(x, axis=-1, keepdims=True) for x in p]
    o = [_dot(x.astype(v.dtype), v) for x, v in zip(p, vs)]
    for c, x, y in zip(cols, o, l):
        o_ref[0, :, c] = (x / y).astype(o_ref.dtype)


def _attn_a(u, bias_tiles, heads, q_col, k_col, v_col):
    b, t, _ = u.shape
    hb = A_HEADS_PER_STEP
    assert t % Q_BLOCK == 0 and t >= A_WIN
    assert heads % hb == 0 and q_col % hb == 0 and k_col % hb == 0 and v_col % hb == 0
    nq = t // Q_BLOCK
    return pl.pallas_call(
        _attn_a_kernel,
        grid=(b, heads // hb, nq),
        in_specs=[pl.BlockSpec((1, Q_BLOCK, hb * HEAD_DIM),
                               lambda bi, h, i: (bi, i, q_col // hb + h)),
                  pl.BlockSpec((1, t, hb * HEAD_DIM), lambda bi, h, i: (bi, 0, k_col // hb + h)),
                  pl.BlockSpec((1, t, hb * HEAD_DIM), lambda bi, h, i: (bi, 0, v_col // hb + h)),
                  pl.BlockSpec((hb, 1, Q_BLOCK, A_WIN),
                               lambda bi, h, i: (h, jnp.minimum(i, A_VARIANTS - 1), 0, 0))],
        out_specs=pl.BlockSpec((1, Q_BLOCK, hb * HEAD_DIM), lambda bi, h, i: (bi, i, h)),
        out_shape=jax.ShapeDtypeStruct((b, t, heads * HEAD_DIM), BF16),
        compiler_params=_cparams("parallel", "parallel", "arbitrary"),
        name="band_attention",
    )(u, u, u, bias_tiles)


DSA_TK = 512


def _sortable_key(x):
    bits = lax.bitcast_convert_type(x, jnp.int32)
    return jnp.where(bits < 0, bits ^ jnp.int32(0x7FFFFFFF), bits)


DSA_GROUP = 4


def _dsa_kernel(qb_ref, qi_ref, wi_ref, k_ref, v_ref, ki_ref, o_ref,
                key_ref, qs_ref, m_ref, l_ref, acc_ref, *, heads, idx_heads, topk):
    i = pl.program_id(1)
    tk = DSA_TK
    n_kt = (i * Q_BLOCK + Q_BLOCK + tk - 1) // tk
    row = lax.broadcasted_iota(jnp.int32, (Q_BLOCK, 1), 0)
    limit = ((i * Q_BLOCK + row) // CHUNK + 1) * CHUNK

    def lane_fold(x):
        out = x[:, :LANES]
        for c in range(1, tk // LANES):
            out = out + x[:, c * LANES:(c + 1) * LANES]
        return out

    def score_tile(kt, carry):
        off = pl.multiple_of(kt * tk, tk)
        ki_t = ki_ref[0, pl.ds(off, tk), :]
        acc = jnp.zeros((Q_BLOCK, tk), F32)
        for h in range(idx_heads):
            logits = _dot_nt(qi_ref[0, :, h * IDX_DIM:(h + 1) * IDX_DIM], ki_t)
            acc = acc + wi_ref[0, :, h:h + 1] * jnp.maximum(logits, 0.0)
        kpos = off + lax.broadcasted_iota(jnp.int32, (Q_BLOCK, tk), 1)
        key_ref[:, pl.ds(off, tk)] = _sortable_key(jnp.where(kpos < limit, acc, NEG_INF))
        return carry

    lax.fori_loop(0, n_kt, score_tile, 0)

    def bit_step(bi, t):
        cand = t + lax.shift_left(jnp.int32(1), 31 - bi)

        def count_tile(kt, c):
            off = pl.multiple_of(kt * tk, tk)
            return c + lane_fold(jnp.where(key_ref[:, pl.ds(off, tk)] >= cand, 1.0, 0.0))

        part = lax.fori_loop(0, n_kt, count_tile, jnp.zeros((Q_BLOCK, LANES), F32))
        cnt = jnp.sum(part, axis=-1, keepdims=True)
        return jnp.where(cnt >= float(topk), cand, t)

    thr = lax.fori_loop(0, 32, bit_step, jnp.full((Q_BLOCK, 1), INT32_MIN, jnp.int32))

    def tie_tile(kt, carry):
        gt, eq = carry
        off = pl.multiple_of(kt * tk, tk)
        keys = key_ref[:, pl.ds(off, tk)]
        kpos = off + lax.broadcasted_iota(jnp.int32, (Q_BLOCK, tk), 1)
        gt = gt + lane_fold(jnp.where(keys > thr, 1.0, 0.0))
        eq = eq + lane_fold(jnp.where((keys == thr) & (kpos < limit), 1.0, 0.0))
        return gt, eq

    zeros = jnp.zeros((Q_BLOCK, LANES), F32)
    gt, eq = lax.fori_loop(0, n_kt, tie_tile, (zeros, zeros))
    need = float(topk) - jnp.sum(gt, axis=-1, keepdims=True)
    surplus = jnp.max(jnp.sum(eq, axis=-1, keepdims=True) - need)

    @pl.when(surplus > 0.0)
    def _():
        ki_ = lax.broadcasted_iota(jnp.int32, (tk, tk), 0)
        kj_ = lax.broadcasted_iota(jnp.int32, (tk, tk), 1)
        before = (ki_ < kj_).astype(BF16)

        def drop_tile(kt, seen):
            off = pl.multiple_of(kt * tk, tk)
            keys = key_ref[:, pl.ds(off, tk)]
            kpos = off + lax.broadcasted_iota(jnp.int32, (Q_BLOCK, tk), 1)
            is_eq = (keys == thr) & (kpos < limit)
            eq_f = jnp.where(is_eq, 1.0, 0.0)
            rank = seen + _dot(eq_f.astype(BF16), before)
            key_ref[:, pl.ds(off, tk)] = jnp.where(is_eq & (rank >= need), keys - 1, keys)
            return seen + jnp.sum(eq_f, axis=-1, keepdims=True)

        lax.fori_loop(0, n_kt, drop_tile, jnp.zeros((Q_BLOCK, 1), F32))

    for h in range(heads):
        qs_ref[h * Q_BLOCK:(h + 1) * Q_BLOCK, :] = qb_ref[0, :, h * HEAD_DIM:(h + 1) * HEAD_DIM]
    m_ref[...] = jnp.full(m_ref.shape, NEG_INF, F32)
    l_ref[...] = jnp.zeros(l_ref.shape, F32)
    acc_ref[...] = jnp.zeros(acc_ref.shape, F32)

    def attend_tile(kt, carry):
        off = pl.multiple_of(kt * tk, tk)
        k_t = k_ref[0, pl.ds(off, tk), :]
        v_t = v_ref[0, pl.ds(off, tk), :]
        kpos = off + lax.broadcasted_iota(jnp.int32, (Q_BLOCK, tk), 1)
        sel = (key_ref[:, pl.ds(off, tk)] >= thr) & (kpos < limit)
        for g in range(0, heads, DSA_GROUP):
            _flash_update(qs_ref, [h * Q_BLOCK for h in range(g, g + DSA_GROUP)], k_t, v_t, sel,
                          m_ref, l_ref, acc_ref)
        return carry

    lax.fori_loop(0, n_kt, attend_tile, 0)
    for h in range(heads):
        rows = slice(h * Q_BLOCK, (h + 1) * Q_BLOCK)
        o_ref[0, :, h * HEAD_DIM:(h + 1) * HEAD_DIM] = (
            acc_ref[rows, :] / l_ref[rows, :]).astype(o_ref.dtype)


def _dsa(u_q, wi, kb, vb, ki, heads, idx_heads):
    b, t, _ = u_q.shape
    assert heads * HEAD_DIM == idx_heads * IDX_DIM and heads % DSA_GROUP == 0
    qw = heads * HEAD_DIM
    topk = min(TOPK_MAX, t // 4)
    assert t % DSA_TK == 0 and topk <= DSA_TK
    nq = t // Q_BLOCK
    return pl.pallas_call(
        functools.partial(_dsa_kernel, heads=heads, idx_heads=idx_heads, topk=topk),
        grid=(b, nq),
        in_specs=[pl.BlockSpec((1, Q_BLOCK, qw), lambda bi, i: (bi, i, 0)),
                  pl.BlockSpec((1, Q_BLOCK, qw), lambda bi, i: (bi, i, 1)),
                  pl.BlockSpec((1, Q_BLOCK, idx_heads), lambda bi, i: (bi, i, 0)),
                  pl.BlockSpec((1, t, HEAD_DIM), lambda bi, i: (bi, 0, 0)),
                  pl.BlockSpec((1, t, HEAD_DIM), lambda bi, i: (bi, 0, 0)),
                  pl.BlockSpec((1, t, IDX_DIM), lambda bi, i: (bi, 0, 0))],
        out_specs=pl.BlockSpec((1, Q_BLOCK, qw), lambda bi, i: (bi, i, 0)),
        out_shape=jax.ShapeDtypeStruct((b, t, qw), BF16),
        scratch_shapes=[pltpu.VMEM((Q_BLOCK, t), jnp.int32),
                        pltpu.VMEM((heads * Q_BLOCK, HEAD_DIM), BF16),
                        pltpu.VMEM((heads * Q_BLOCK, LANES), F32),
                        pltpu.VMEM((heads * Q_BLOCK, LANES), F32),
                        pltpu.VMEM((heads * Q_BLOCK, HEAD_DIM), F32)],
        compiler_params=_cparams("parallel", "arbitrary"),
        name="dsa_attention",
    )(u_q, u_q, wi, kb, vb, ki)


def _shifted(x, prev_row, first):
    rolled = pltpu.roll(x, 1, axis=0)
    head = jnp.where(first, jnp.zeros_like(prev_row), prev_row)
    rid = lax.broadcasted_iota(jnp.int32, x.shape, 0)
    return jnp.where(rid == 0, head, rolled)


def _sigmoid(x):
    return 1.0 / (1.0 + jnp.exp(-x))


def _rwkv_prep_kernel(rkv_ref, rkv_prev_ref, lo_ref, lo_prev_ref, mu_rkv_ref, mu_lo_ref,
                      w0_ref, a0_ref, wup_ref, aup_ref, gup_ref,
                      r_ref, k_ref, v_ref, lw_ref, a_ref, g_ref, *, blocks_per_seq, width):
    first = (pl.program_id(0) % blocks_per_seq) == 0
    x = rkv_ref[...]
    prev = _shifted(x, rkv_prev_ref[7:8, :], first)
    x = x + mu_rkv_ref[...] * (prev - x)
    r_ref[...] = x[:, :width]
    k_ref[...] = x[:, width:2 * width]
    v_ref[...] = x[:, 2 * width:]

    lo = lo_ref[...]
    lo_prev = _shifted(lo, lo_prev_ref[7:8, :], first)
    lo = lo + mu_lo_ref[...] * (lo_prev - lo)
    xw = lo[:, :DECAY_LORA]
    xa = lo[:, DECAY_LORA:DECAY_LORA + ICLR_LORA]
    xg = lo[:, DECAY_LORA + ICLR_LORA:]
    z = -(w0_ref[...] + _dot(jnp.tanh(xw).astype(BF16), wup_ref[...]))
    softplus = jnp.maximum(z, 0.0) + jnp.log(1.0 + jnp.exp(-jnp.abs(z)))
    logw = -softplus - 0.5
    lw_ref[...] = -jnp.exp(logw)
    a_ref[...] = _sigmoid(a0_ref[...] + _dot(xa.astype(BF16), aup_ref[...]))
    g_ref[...] = _dot(_sigmoid(xg).astype(BF16), gup_ref[...])


def _rwkv_prep(u_rkv, u_lo, mu, w0, w_up, a0, a_up, g_up, rows_per_batch):
    m, w3 = u_rkv.shape
    width = w3 // 3
    lo_w = DECAY_LORA + ICLR_LORA + GATE_LORA
    tb = 128
    assert rows_per_batch % tb == 0
    blocks_per_seq = rows_per_batch // tb
    sub = tb // 8
    prev_map = lambda i: (jnp.maximum(i * sub - 1, 0), 0)
    row = lambda i: (i, 0)
    fixed = lambda i: (0, 0)
    out = jax.ShapeDtypeStruct((m, width), F32)
    return pl.pallas_call(
        functools.partial(_rwkv_prep_kernel, blocks_per_seq=blocks_per_seq, width=width),
        grid=(m // tb,),
        in_specs=[pl.BlockSpec((tb, w3), row),
                  pl.BlockSpec((8, w3), prev_map),
                  pl.BlockSpec((tb, lo_w), row),
                  pl.BlockSpec((8, lo_w), prev_map),
                  pl.BlockSpec((1, w3), fixed),
                  pl.BlockSpec((1, lo_w), fixed),
                  pl.BlockSpec((1, width), fixed),
                  pl.BlockSpec((1, width), fixed),
                  pl.BlockSpec((DECAY_LORA, width), fixed),
                  pl.BlockSpec((ICLR_LORA, width), fixed),
                  pl.BlockSpec((GATE_LORA, width), fixed)],
        out_specs=[pl.BlockSpec((tb, width), row)] * 6,
        out_shape=[out] * 6,
        compiler_params=_cparams("parallel"),
        name="rwkv_prep",
    )(u_rkv, u_rkv, u_lo, u_lo, mu[:w3].reshape(1, w3), mu[w3:].reshape(1, lo_w),
      w0.reshape(1, width), a0.reshape(1, width), w_up.astype(BF16), a_up.astype(BF16),
      g_up.astype(BF16))


WKV_L = 64
WKV_HEADS = 16


def _wkv_kernel(r_ref, k_ref, v_ref, lw_ref, a_ref, g_ref, kk_ref, ka_ref, rk_ref, lnw_ref,
                lnb_ref, o_ref, state_ref):
    L = WKV_L
    n = C_HEAD_DIM

    @pl.when(pl.program_id(2) == 0)
    def _():
        state_ref[...] = jnp.zeros(state_ref.shape, F32)

    ti = lax.broadcasted_iota(jnp.int32, (L, L), 0)
    si = lax.broadcasted_iota(jnp.int32, (L, L), 1)
    lower_strict = ti > si
    tri_ones = (ti >= si).astype(F32)
    eye = (ti == si).astype(F32)
    ti2 = lax.broadcasted_iota(jnp.int32, (L, 2 * L), 0)
    si2 = lax.broadcasted_iota(jnp.int32, (L, 2 * L), 1)
    lower_incl2 = ti2 >= jnp.where(si2 >= L, si2 - L, si2)

    heads = range(WKV_HEADS)
    sls = [slice(j * n, (j + 1) * n) for j in heads]

    def per_head(full, fn):
        return jnp.concatenate([jnp.broadcast_to(fn(full[:, sl]), (L, n)) for sl in sls], axis=1)

    r = r_ref[0]
    k = k_ref[0]
    v = v_ref[0]
    lw = lw_ref[0]
    a_lr = a_ref[0]
    kk = k * kk_ref[...]
    kk = kk / per_head(kk * kk, lambda z: jnp.maximum(
        jnp.sqrt(jnp.sum(z, axis=-1, keepdims=True)), 1e-12))
    k = k * (1.0 + (a_lr - 1.0) * ka_ref[...])
    b_s = kk * a_lr
    cum = _dot_f32(tri_ones, lw)
    c_end = cum[L - 1:L, :]
    c_inv = jnp.exp(-cum)
    c_out = jnp.exp(c_end - cum)
    a_t = (-kk * jnp.exp(cum - lw)).astype(BF16)
    r_t = (r * jnp.exp(cum)).astype(BF16)
    b_t = (b_s * c_inv).astype(BF16)
    k_t = (k * c_inv).astype(BF16)
    b_o = (b_s * c_out).astype(BF16)
    k_o = (k * c_out).astype(BF16)
    v_b = v.astype(BF16)
    s_decay = jnp.exp(c_end)

    ar = [jnp.concatenate([a_t[:, sl], r_t[:, sl]], axis=0) for sl in sls]
    bk = [jnp.concatenate([b_t[:, sl], k_t[:, sl]], axis=0) for sl in sls]
    bk_out = [jnp.concatenate([b_o[:, sl], k_o[:, sl]], axis=0) for sl in sls]
    vs = [v_b[:, sl] for sl in sls]
    prod = [_dot_nt(ar[j], bk[j]) for j in heads]
    n_ab = [jnp.where(lower_strict, p[:L, :L], 0.0) for p in prod]
    n_ak = [jnp.where(lower_strict, p[:L, L:], 0.0).astype(BF16) for p in prod]
    m_r = [jnp.where(lower_incl2, p[L:, :], 0.0).astype(BF16) for p in prod]

    inv = [eye + z for z in n_ab]
    pw = n_ab
    span = 1
    while 2 * span < L:
        pw_b = [z.astype(BF16) for z in pw]
        pw = [_dot(z, z) for z in pw_b]
        inv = [inv[j] + _dot(inv[j].astype(BF16), pw[j].astype(BF16)) for j in heads]
        span *= 2

    s0 = [state_ref[j] for j in heads]
    ah = [_dot_nt(ar[j], s0[j].astype(BF16)) for j in heads]
    rhs = [ah[j][:L] + _dot(n_ak[j], vs[j]) for j in heads]
    u = [_dot(inv[j].astype(BF16), rhs[j].astype(BF16)) for j in heads]
    uv = [jnp.concatenate([u[j].astype(BF16), vs[j]], axis=0) for j in heads]
    y = jnp.concatenate([ah[j][L:] + _dot(m_r[j], uv[j]) for j in heads], axis=1)
    for j in heads:
        s_inc = lax.dot_general(uv[j], bk_out[j], (((0,), (0,)), ((), ())),
                                preferred_element_type=F32)
        state_ref[j] = s0[j] * s_decay[:, sls[j]] + s_inc

    mean = per_head(y, lambda z: jnp.mean(z, axis=-1, keepdims=True))
    yc = y - mean
    var = per_head(yc * yc, lambda z: jnp.mean(z, axis=-1, keepdims=True))
    yn = yc * lax.rsqrt(var + C_GN_EPS) * lnw_ref[...] + lnb_ref[...]
    bonus = per_head(r * k * rk_ref[...], lambda z: jnp.sum(z, axis=-1, keepdims=True)) * v
    o_ref[0] = ((yn + bonus) * g_ref[0]).astype(o_ref.dtype)


def _wkv(r, k, v, lw, a, g, k_k, k_a, r_k, ln_w, ln_b, batch):
    m, width = r.shape
    t = m // batch
    hb = WKV_HEADS * C_HEAD_DIM
    assert t % WKV_L == 0 and width % hb == 0
    seq = lambda z: z.reshape(batch, t, width)
    par = lambda z: z.reshape(1, width).astype(F32)
    tok = pl.BlockSpec((1, WKV_L, hb), lambda bi, hg, c: (bi, c, hg))
    vec = pl.BlockSpec((1, hb), lambda bi, hg, c: (0, hg))
    out = pl.pallas_call(
        _wkv_kernel,
        grid=(batch, width // hb, t // WKV_L),
        in_specs=[tok] * 6 + [vec] * 5,
        out_specs=tok,
        out_shape=jax.ShapeDtypeStruct((batch, t, width), BF16),
        scratch_shapes=[pltpu.VMEM((WKV_HEADS, C_HEAD_DIM, C_HEAD_DIM), F32)],
        compiler_params=_cparams("parallel", "parallel", "arbitrary"),
        name="wkv7_chunked",
    )(seq(r), seq(k), seq(v), seq(lw), seq(a), seq(g), par(k_k), par(k_a), par(r_k),
      par(ln_w), par(ln_b))
    return out.reshape(m, width)


def _rope_tables(t):
    half = ROPE_DIM // 2
    freqs = ROPE_THETA ** (-jnp.arange(half, dtype=F32) / half)
    ang = jnp.arange(t, dtype=F32)[:, None] * freqs[None, :]
    cos, sin = jnp.cos(ang), jnp.sin(ang)
    zeros = jnp.zeros((t, LANES - ROPE_DIM), F32)
    return (jnp.concatenate([cos, cos, zeros], axis=-1),
            jnp.concatenate([-sin, sin, zeros], axis=-1))


def _rope128(x, cos, sin):
    half = ROPE_DIM // 2
    swapped = pltpu.roll(x, LANES - half, axis=1) + pltpu.roll(x, half, axis=1)
    return x * cos + swapped * sin


def _mla_prep_kernel(cq_ref, ckv_ref, kpe_ref, qn_ref, kvn_ref, cos_ref, sin_ref,
                     cq_o, ckv_o, kpe_o):
    cq = cq_ref[...]
    cq_o[...] = (cq * lax.rsqrt(jnp.mean(cq * cq, axis=-1, keepdims=True) + EPS)
                 * qn_ref[...]).astype(cq_o.dtype)
    ckv = ckv_ref[...]
    ckv_o[...] = (ckv * lax.rsqrt(jnp.mean(ckv * ckv, axis=-1, keepdims=True) + EPS)
                  * kvn_ref[...]).astype(ckv_o.dtype)
    lane = lax.broadcasted_iota(jnp.int32, kpe_ref.shape, 1)
    kpe = jnp.where(lane < ROPE_DIM, kpe_ref[...], 0.0)
    kpe_o[...] = _rope128(kpe, cos_ref[...], sin_ref[...]).astype(kpe_o.dtype)


def _mla_prep(u_lo, q_norm, kv_norm, cos, sin, q_lora, kv_lora, col0, rows_per_batch):
    m = u_lo.shape[0]
    tb = _pick(rows_per_batch, (256, 128, 64, 32, 16, 8))
    per = rows_per_batch // tb
    row = lambda i: (i, 0)
    fixed = lambda i: (0, 0)
    return pl.pallas_call(
        _mla_prep_kernel,
        grid=(m // tb,),
        in_specs=[pl.BlockSpec((tb, q_lora), row),
                  pl.BlockSpec((tb, kv_lora), row),
                  pl.BlockSpec((tb, LANES), row),
                  pl.BlockSpec((1, q_lora), fixed),
                  pl.BlockSpec((1, kv_lora), fixed),
                  pl.BlockSpec((tb, LANES), lambda i: (i % per, 0)),
                  pl.BlockSpec((tb, LANES), lambda i: (i % per, 0))],
        out_specs=[pl.BlockSpec((tb, q_lora), row),
                   pl.BlockSpec((tb, kv_lora), row),
                   pl.BlockSpec((tb, LANES), row)],
        out_shape=[jax.ShapeDtypeStruct((m, q_lora), BF16),
                   jax.ShapeDtypeStruct((m, kv_lora), BF16),
                   jax.ShapeDtypeStruct((m, LANES), BF16)],
        compiler_params=_cparams("parallel"),
        name="mla_prep",
    )(u_lo[:, col0:col0 + q_lora], u_lo[:, col0 + q_lora:col0 + q_lora + kv_lora],
      u_lo[:, col0 + q_lora + kv_lora:col0 + q_lora + kv_lora + LANES],
      q_norm.reshape(1, q_lora), kv_norm.reshape(1, kv_lora), cos, sin)


MLA_TQ = 512
MLA_TK = 512


def _mla_kernel(q_ref, kv_ref, kpe_ref, cos_ref, sin_ref, o_ref, qc_ref, m_ref, l_ref, acc_ref):
    i = pl.program_id(2)
    tq, tk = MLA_TQ, MLA_TK
    n_full = (i * tq) // tk
    qc_ref[:, :NOPE_DIM] = q_ref[0, :, :NOPE_DIM]
    qc_ref[:, NOPE_DIM:] = _rope128(q_ref[0, :, NOPE_DIM:].astype(F32), cos_ref[...],
                                    sin_ref[...]).astype(BF16)
    m_ref[...] = jnp.full(m_ref.shape, NEG_INF, F32)
    l_ref[...] = jnp.zeros(l_ref.shape, F32)
    acc_ref[...] = jnp.zeros(acc_ref.shape, F32)

    def tile(kt, masked):
        off = pl.multiple_of(kt * tk, tk)
        k_cat = jnp.concatenate([kv_ref[0, pl.ds(off, tk), :NOPE_DIM],
                                 kpe_ref[0, pl.ds(off, tk), :]], axis=1)
        v_t = kv_ref[0, pl.ds(off, tk), NOPE_DIM:]
        starts = list(range(0, tq, FLASH_ROWS))
        mask = None
        if masked:
            row = lax.broadcasted_iota(jnp.int32, (FLASH_ROWS, 1), 0)
            kpos = off + lax.broadcasted_iota(jnp.int32, (FLASH_ROWS, tk), 1)
            mask = [kpos < ((i * tq + r + row) // CHUNK + 1) * CHUNK for r in starts]
        _flash_update(qc_ref, starts, k_cat, v_t, mask, m_ref, l_ref, acc_ref)

    def full_tile(kt, carry):
        tile(kt, False)
        return carry

    lax.fori_loop(0, n_full, full_tile, 0)
    tile(n_full, True)
    o_ref[0] = (acc_ref[...] / l_ref[...]).astype(o_ref.dtype)


def _mla(q, kv, kpe, cos, sin, heads):
    b, t, _ = q.shape
    assert t % MLA_TK == 0 and MLA_TK % MLA_TQ == 0
    hw = NOPE_DIM + LANES
    return pl.pallas_call(
        _mla_kernel,
        grid=(b, heads, t // MLA_TQ),
        in_specs=[pl.BlockSpec((1, MLA_TQ, hw), lambda bi, h, i: (bi, i, h)),
                  pl.BlockSpec((1, t, NOPE_DIM + V_DIM), lambda bi, h, i: (bi, 0, h)),
                  pl.BlockSpec((1, t, LANES), lambda bi, h, i: (bi, 0, 0)),
                  pl.BlockSpec((MLA_TQ, LANES), lambda bi, h, i: (i, 0)),
                  pl.BlockSpec((MLA_TQ, LANES), lambda bi, h, i: (i, 0))],
        out_specs=pl.BlockSpec((1, MLA_TQ, V_DIM), lambda bi, h, i: (bi, i, h)),
        out_shape=jax.ShapeDtypeStruct((b, t, heads * V_DIM), BF16),
        scratch_shapes=[pltpu.VMEM((MLA_TQ, hw), BF16),
                        pltpu.VMEM((MLA_TQ, LANES), F32),
                        pltpu.VMEM((MLA_TQ, LANES), F32),
                        pltpu.VMEM((MLA_TQ, V_DIM), F32)],
        compiler_params=_cparams("parallel", "parallel", "arbitrary"),
        name="mla_attention",
    )(q, kv, kpe, cos, sin)


def _pad_cols(w, n):
    return jnp.pad(w, ((0, 0), (0, n - w.shape[1])))


def _even_mixer(h, x, gate, w_in, w_out, rel_bias, batch):
    m, d = h.shape
    t = m // batch
    heads = d // (2 * HEAD_DIM)
    idx_heads = d // 128
    hw = heads * HEAD_DIM
    o_kb = 4 * hw
    o_qi = o_kb + 2 * HEAD_DIM
    o_ki = o_qi + idx_heads * IDX_DIM
    q_factor = LOG2_E * HEAD_DIM ** -0.5
    w_a = jnp.concatenate([w_in[:, :hw] * q_factor, w_in[:, hw:3 * hw]], axis=1)
    u_a = _matmul(h, w_a.astype(BF16), out_dtype=BF16)
    w_qb = w_in[:, 3 * hw:4 * hw] * q_factor
    u_q = _matmul(h, jnp.concatenate([w_qb, w_in[:, o_qi:o_ki]], axis=1).astype(BF16),
                  out_dtype=BF16)
    small = jnp.concatenate([w_in[:, o_kb:o_qi], w_in[:, o_ki:]], axis=1)
    u_s = _matmul(h, _pad_cols(small, 3 * LANES).astype(BF16), out_dtype=F32, tn=3 * LANES)
    u_s = u_s.reshape(batch, t, 3 * LANES)
    kb = u_s[..., :HEAD_DIM].astype(BF16)
    vb = u_s[..., HEAD_DIM:2 * HEAD_DIM].astype(BF16)
    ki = u_s[..., 2 * HEAD_DIM:2 * HEAD_DIM + IDX_DIM].astype(BF16)
    wi = u_s[..., 2 * HEAD_DIM + IDX_DIM:2 * HEAD_DIM + IDX_DIM + idx_heads]
    ya = _attn_a(u_a.reshape(batch, t, 3 * hw), _band_bias_tiles(rel_bias), heads,
                 0, heads, 2 * heads)
    yb = _dsa(u_q.reshape(batch, t, 2 * hw), wi, kb, vb, ki, heads, idx_heads)
    y = jnp.concatenate([ya, yb], axis=-1).reshape(m, 2 * hw)
    return _matmul(y, w_out.astype(BF16), out_dtype=F32, epilogue="resid", x=x, gate=gate,
                   rows_per_batch=t)


def _odd_mixer(h, x, gate, w_in, w_out, c_mu, c_w0, c_w_up, c_a0, c_a_up, c_g_up, c_k_k, c_k_a,
               c_r_k, c_ln_w, c_ln_b, d_q_norm, d_kv_norm, d_w_uq, d_w_ukv, batch):
    m, d = h.shape
    t = m // batch
    width = c_w0.shape[0]
    heads = d // (2 * HEAD_DIM)
    q_lora = d_q_norm.shape[0]
    kv_lora = d_kv_norm.shape[0]
    lo_w = DECAY_LORA + ICLR_LORA + GATE_LORA
    w_in = w_in.astype(BF16)
    u_rkv = _matmul(h, w_in[:, :3 * width], out_dtype=F32)
    rest = w_in[:, 3 * width:]
    rest_cols = -(-(rest.shape[1] + LANES - ROPE_DIM) // 512) * 512
    u_lo = _matmul(h, _pad_cols(rest, rest_cols), out_dtype=F32)

    r, k, v, lw, a, g = _rwkv_prep(u_rkv, u_lo, c_mu, c_w0, c_w_up, c_a0, c_a_up, c_g_up, t)
    yc = _wkv(r, k, v, lw, a, g, c_k_k, c_k_a, c_r_k, c_ln_w, c_ln_b, batch)

    cos, sin = _rope_tables(t)
    cq, ckv, kpe = _mla_prep(u_lo, d_q_norm, d_kv_norm, cos, sin, q_lora, kv_lora, lo_w, t)
    w_uq = d_w_uq.reshape(q_lora, heads, NOPE_DIM + ROPE_DIM)
    w_uq = jnp.pad(w_uq, ((0, 0), (0, 0), (0, LANES - ROPE_DIM))).reshape(q_lora, -1)
    w_uq = w_uq * (LOG2_E * (NOPE_DIM + ROPE_DIM) ** -0.5)
    q = _matmul(cq, w_uq.astype(BF16), out_dtype=BF16)
    kv = _matmul(ckv, d_w_ukv.astype(BF16), out_dtype=BF16)
    yd = _mla(q.reshape(batch, t, -1), kv.reshape(batch, t, -1), kpe.reshape(batch, t, LANES),
              cos, sin, heads)
    y = jnp.concatenate([yc, yd.reshape(m, -1)], axis=-1)
    return _matmul(y, w_out.astype(BF16), out_dtype=F32, epilogue="resid", x=x, gate=gate,
                   rows_per_batch=t)


def kernel(x, c, ada_w, ada_table, norm_mix, norm_ffn, norm_final, ffn_w1, ffn_w2, ev_w_in,
           ev_w_out, a_rel_bias, od_w_in, od_w_out, c_mu, c_w0, c_w_up, c_a0, c_a_up, c_g_up,
           c_k_k, c_k_a, c_r_k, c_ln_w, c_ln_b, d_q_norm, d_kv_norm, d_w_uq, d_w_ukv):
    batch, t, d = x.shape
    depth = ada_table.shape[0]
    m = batch * t
    silu_c = (c * _sigmoid(c)).astype(BF16)
    silu_c = jnp.pad(silu_c, ((0, 8 - batch % 8 if batch % 8 else 0), (0, 0)))
    ada = _matmul(silu_c, ada_w.astype(BF16), out_dtype=F32, tn=2048)[:batch].reshape(batch, 6, d)
    x = x.reshape(m, d)
    for layer in range(depth):
        mod = ada + ada_table[layer]
        shift_m, scale_m, gate_m, shift_f, scale_f, gate_f = (mod[:, i, :] for i in range(6))
        h = _norm_mod(x, norm_mix[layer], scale_m, shift_m, t)
        if layer % 2 == 0:
            e = layer // 2
            x = _even_mixer(h, x, gate_m, ev_w_in[e], ev_w_out[e], a_rel_bias[e], batch)
        else:
            o = layer // 2
            x = _odd_mixer(h, x, gate_m, od_w_in[o], od_w_out[o], c_mu[o], c_w0[o], c_w_up[o],
                           c_a0[o], c_a_up[o], c_g_up[o], c_k_k[o], c_k_a[o], c_r_k[o],
                           c_ln_w[o], c_ln_b[o], d_q_norm[o], d_kv_norm[o], d_w_uq[o],
                           d_w_ukv[o], batch)
        h = _norm_mod(x, norm_ffn[layer], scale_f, shift_f, t)
        h1 = _matmul(h, ffn_w1[layer].astype(BF16), out_dtype=BF16, epilogue="relu2")
        x = _matmul(h1, ffn_w2[layer].astype(BF16), out_dtype=F32, epilogue="resid", x=x,
                    gate=gate_f, rows_per_batch=t)
    return _norm_plain(x, norm_final, F32).reshape(batch, t, d)
```

```python
import functools

import numpy as np
import jax
import jax.numpy as jnp
from jax import lax
from jax.experimental import pallas as pl
from jax.experimental.pallas import tpu as pltpu

F32 = jnp.float32
BF16 = jnp.bfloat16

CHUNK = 64
Q_BLOCK = 128
HEAD_DIM = 128
EPS = 1e-6
NEG_INF = -1e30
A_LEFT_CHUNKS = 8
A_MAX_REL = 256
IDX_DIM = 64
TOPK_MAX = 256
C_HEAD_DIM = 64
DECAY_LORA = 128
ICLR_LORA = 128
GATE_LORA = 256
C_GN_EPS = 64e-5
NOPE_DIM = 128
ROPE_DIM = 64
V_DIM = 128
ROPE_THETA = 10000.0

V7X_VMEM_LIMIT_BYTES = 56 * 1024 * 1024
LANES = 128
INT32_MIN = -(2 ** 31)
LOG2_E = 1.4426950408889634


def _cparams(*sem):
    return pltpu.CompilerParams(dimension_semantics=sem, vmem_limit_bytes=V7X_VMEM_LIMIT_BYTES)


def _dot_nt(a, b):
    return lax.dot_general(a, b, (((1,), (1,)), ((), ())), preferred_element_type=F32)


def _dot(a, b):
    return jnp.dot(a, b, preferred_element_type=F32)


def _dot_f32(a, b):
    return jnp.dot(a, b, preferred_element_type=F32, precision=lax.Precision.HIGHEST)


FLASH_ROWS = 128


def _flash_update(q_ref, row_starts, k_t, v_t, mask, m_ref, l_ref, acc_ref):
    tk = k_t.shape[0]
    reps = tk // LANES
    rows = [slice(r, r + FLASH_ROWS) for r in row_starts]
    s = [_dot_nt(q_ref[r, :], k_t) for r in rows]
    if isinstance(mask, list):
        s = [jnp.where(mk, x, NEG_INF) for mk, x in zip(mask, s)]
    elif mask is not None:
        s = [x + mask for x in s]
    m_old = [m_ref[r, :] for r in rows]
    m_new = [jnp.maximum(mo, jnp.max(x, axis=-1, keepdims=True)) for mo, x in zip(m_old, s)]
    p = [jnp.exp2(x - jnp.concatenate([mn] * reps, axis=1)) for x, mn in zip(s, m_new)]
    alpha = [jnp.exp2(mo - mn) for mo, mn in zip(m_old, m_new)]
    pv = [_dot(x.astype(v_t.dtype), v_t) for x in p]
    for r, mn, al, x, y in zip(rows, m_new, alpha, p, pv):
        m_ref[r, :] = mn
        l_ref[r, :] = al * l_ref[r, :] + jnp.sum(x, axis=-1, keepdims=True)
        acc_ref[r, :] = al * acc_ref[r, :] + y


def _pick(n, prefs):
    for p in prefs:
        if n % p == 0:
            return p
    return n


MM_ACC_CHUNK = 256

def _mm_kernel(*refs, nk, n_lhs, epilogue):
    a_refs = refs[:n_lhs]
    w_ref = refs[n_lhs]
    if epilogue == "resid":
        x_ref, g_ref, o_ref = refs[n_lhs + 1:n_lhs + 4]
        scratch = refs[n_lhs + 4:]
    else:
        o_ref = refs[n_lhs + 1]
        scratch = refs[n_lhs + 2:]

    tn = o_ref.shape[1]

    def finish(acc, cols):
        if epilogue == "relu2":
            r = jnp.maximum(acc, 0.0)
            acc = r * r
        elif epilogue == "resid":
            acc = x_ref[:, cols] + g_ref[0, :, cols] * acc
        o_ref[:, cols] = acc.astype(o_ref.dtype)

    def product(cols):
        part, row = None, 0
        for a_ref in a_refs:
            kw = a_ref.shape[1]
            term = _dot(a_ref[...], w_ref[row:row + kw, cols].astype(BF16))
            part = term if part is None else part + term
            row += kw
        return part

    if nk == 1:
        finish(product(slice(0, tn)), slice(0, tn))
        return

    acc_ref = scratch[0]
    k = pl.program_id(2)
    chunks = [slice(c, c + MM_ACC_CHUNK) for c in range(0, tn, MM_ACC_CHUNK)]

    @pl.when(k == 0)
    def _():
        for cols in chunks:
            acc_ref[:, cols] = product(cols)

    @pl.when((k > 0) & (k < nk - 1))
    def _():
        for cols in chunks:
            acc_ref[:, cols] += product(cols)

    @pl.when(k == nk - 1)
    def _():
        for cols in chunks:
            finish(acc_ref[:, cols] + product(cols), cols)


def _matmul(a, w, *, out_dtype, epilogue="none", x=None, gate=None, rows_per_batch=None,
            tm=None, tn=None, tk=None):
    lhs = a if isinstance(a, tuple) else (a,)
    m = lhs[0].shape[0]
    kd, n = w.shape
    assert sum(z.shape[1] for z in lhs) == kd
    tm = tm or _pick(m, (1024, 512, 256, 128, 64, 32, 16, 8))
    tk = tk or (kd if kd <= 4096 else _pick(kd, (2048, 1024, 512)))
    tn = tn or _pick(n, (1024, 512, 384, 256, 128) if tk < kd else (512, 384, 256, 128))
    nk = kd // tk
    assert m % tm == 0 and n % tn == 0 and kd % tk == 0 and (nk == 1 or len(lhs) == 1)
    in_specs = [pl.BlockSpec((tm, z.shape[1] if nk == 1 else tk), lambda i, j, k: (i, k))
                for z in lhs]
    in_specs.append(pl.BlockSpec((tk, tn), lambda i, j, k: (k, j)))
    args = [*lhs, w]
    if epilogue == "resid":
        assert rows_per_batch % tm == 0
        per = rows_per_batch // tm
        in_specs += [pl.BlockSpec((tm, tn), lambda i, j, k: (i, j)),
                     pl.BlockSpec((1, 1, tn), lambda i, j, k: (i // per, 0, j))]
        args += [x, gate.reshape(gate.shape[0], 1, n)]
    return pl.pallas_call(
        functools.partial(_mm_kernel, nk=nk, n_lhs=len(lhs), epilogue=epilogue),
        grid=(m // tm, n // tn, nk),
        in_specs=in_specs,
        out_specs=pl.BlockSpec((tm, tn), lambda i, j, k: (i, j)),
        out_shape=jax.ShapeDtypeStruct((m, n), out_dtype),
        scratch_shapes=[pltpu.VMEM((tm, tn), F32)] if nk > 1 else [],
        compiler_params=_cparams("parallel", "parallel", "arbitrary"),
        name="matmul_" + epilogue,
    )(*args)


def _norm_mod_kernel(x_ref, g_ref, sc_ref, sh_ref, o_ref):
    x = x_ref[...]
    ms = jnp.mean(x * x, axis=-1, keepdims=True)
    y = x * lax.rsqrt(ms + EPS) * g_ref[...]
    o_ref[...] = (y * (1.0 + sc_ref[0]) + sh_ref[0]).astype(o_ref.dtype)


def _norm_plain_kernel(x_ref, g_ref, o_ref):
    x = x_ref[...]
    ms = jnp.mean(x * x, axis=-1, keepdims=True)
    o_ref[...] = (x * lax.rsqrt(ms + EPS) * g_ref[...]).astype(o_ref.dtype)


def _norm_mod(x, g, scale, shift, rows_per_batch, out_dtype=BF16):
    m, d = x.shape
    tr = _pick(rows_per_batch, (256, 128, 64, 32, 16, 8))
    per = rows_per_batch // tr
    b = scale.shape[0]
    return pl.pallas_call(
        _norm_mod_kernel,
        grid=(m // tr,),
        in_specs=[pl.BlockSpec((tr, d), lambda i: (i, 0)),
                  pl.BlockSpec((1, d), lambda i: (0, 0)),
                  pl.BlockSpec((1, 1, d), lambda i: (i // per, 0, 0)),
                  pl.BlockSpec((1, 1, d), lambda i: (i // per, 0, 0))],
        out_specs=pl.BlockSpec((tr, d), lambda i: (i, 0)),
        out_shape=jax.ShapeDtypeStruct((m, d), out_dtype),
        compiler_params=_cparams("parallel"),
        name="norm_mod",
    )(x, g.reshape(1, d), scale.reshape(b, 1, d), shift.reshape(b, 1, d))


def _norm_plain(x, g, out_dtype):
    m, d = x.shape
    tr = _pick(m, (256, 128, 64, 32, 16, 8))
    return pl.pallas_call(
        _norm_plain_kernel,
        grid=(m // tr,),
        in_specs=[pl.BlockSpec((tr, d), lambda i: (i, 0)),
                  pl.BlockSpec((1, d), lambda i: (0, 0))],
        out_specs=pl.BlockSpec((tr, d), lambda i: (i, 0)),
        out_shape=jax.ShapeDtypeStruct((m, d), out_dtype),
        compiler_params=_cparams("parallel"),
        name="norm_plain",
    )(x, g.reshape(1, d))


A_PAD = A_LEFT_CHUNKS * CHUNK
A_WIN = A_PAD + Q_BLOCK
A_VARIANTS = A_PAD // Q_BLOCK + 1


def _band_bias_tiles(rel_bias):
    width = A_WIN + Q_BLOCK
    tiles = []
    for v in range(A_VARIANTS):
        rel = np.clip(v * Q_BLOCK + Q_BLOCK - 1 - np.arange(width), -A_MAX_REL, A_MAX_REL)
        diag = jnp.pad(rel_bias.astype(F32)[:, rel + A_MAX_REL] * LOG2_E, ((0, 0), (0, 1)))
        skew = jnp.tile(diag, (1, Q_BLOCK))[:, :Q_BLOCK * width].reshape(-1, Q_BLOCK, width)
        tiles.append(skew[:, :, Q_BLOCK - 1:Q_BLOCK - 1 + A_WIN])
    return jnp.stack(tiles, axis=1)


A_HEADS_PER_STEP = 4


def _attn_a_kernel(q_ref, k_ref, v_ref, b_ref, o_ref):
    i = pl.program_id(2)
    ks = pl.multiple_of(jnp.maximum(i * Q_BLOCK - A_PAD, 0), Q_BLOCK)
    qpos = i * Q_BLOCK + lax.broadcasted_iota(jnp.int32, (Q_BLOCK, A_WIN), 0)
    kpos = ks + lax.broadcasted_iota(jnp.int32, (Q_BLOCK, A_WIN), 1)
    chunk_start = (qpos // CHUNK) * CHUNK
    valid = (kpos >= chunk_start - A_PAD) & (kpos < chunk_start + CHUNK)
    cols = [slice(h * HEAD_DIM, (h + 1) * HEAD_DIM) for h in range(A_HEADS_PER_STEP)]
    vs = [v_ref[0, pl.ds(ks, A_WIN), c] for c in cols]
    s = [_dot_nt(q_ref[0, :, c], k_ref[0, pl.ds(ks, A_WIN), c]) + b_ref[h, 0]
         for h, c in enumerate(cols)]
    s = [jnp.where(valid, x, NEG_INF) for x in s]
    p = [jnp.exp2(x - jnp.max(x, axis=-1, keepdims=True)) for x in s]
    l = [jnp.sum(x, axis=-1, keepdims=True) for x in p]
    o = [_dot(x.astype(v.dtype), v) for x, v in zip(p, vs)]
    for c, x, y in zip(cols, o, l):
        o_ref[0, :, c] = (x / y).astype(o_ref.dtype)


def _attn_a(u, bias_tiles, heads, q_col, k_col, v_col):
    b, t, _ = u.shape
    hb = A_HEADS_PER_STEP
    assert t % Q_BLOCK == 0 and t >= A_WIN
    assert heads % hb == 0 and q_col % hb == 0 and k_col % hb == 0 and v_col % hb == 0
    nq = t // Q_BLOCK
    return pl.pallas_call(
        _attn_a_kernel,
        grid=(b, heads // hb, nq),
        in_specs=[pl.BlockSpec((1, Q_BLOCK, hb * HEAD_DIM),
                               lambda bi, h, i: (bi, i, q_col // hb + h)),
                  pl.BlockSpec((1, t, hb * HEAD_DIM), lambda bi, h, i: (bi, 0, k_col // hb + h)),
                  pl.BlockSpec((1, t, hb * HEAD_DIM), lambda bi, h, i: (bi, 0, v_col // hb + h)),
                  pl.BlockSpec((hb, 1, Q_BLOCK, A_WIN),
                               lambda bi, h, i: (h, jnp.minimum(i, A_VARIANTS - 1), 0, 0))],
        out_specs=pl.BlockSpec((1, Q_BLOCK, hb * HEAD_DIM), lambda bi, h, i: (bi, i, h)),
        out_shape=jax.ShapeDtypeStruct((b, t, heads * HEAD_DIM), BF16),
        compiler_params=_cparams("parallel", "parallel", "arbitrary"),
        name="band_attention",
    )(u, u, u, bias_tiles)


DSA_TK = 512


def _sortable_key(x):
    bits = lax.bitcast_convert_type(x, jnp.int32)
    return jnp.where(bits < 0, bits ^ jnp.int32(0x7FFFFFFF), bits)


DSA_GROUP = 4


def _dsa_kernel(qb_ref, qi_ref, wi_ref, k_ref, v_ref, ki_ref, o_ref,
                key_ref, qs_ref, m_ref, l_ref, acc_ref, *, heads, idx_heads, topk):
    i = pl.program_id(1)
    tk = DSA_TK
    n_kt = (i * Q_BLOCK + Q_BLOCK + tk - 1) // tk
    row = lax.broadcasted_iota(jnp.int32, (Q_BLOCK, 1), 0)
    limit = ((i * Q_BLOCK + row) // CHUNK + 1) * CHUNK

    def lane_fold(x):
        out = x[:, :LANES]
        for c in range(1, tk // LANES):
            out = out + x[:, c * LANES:(c + 1) * LANES]
        return out

    def score_tile(kt, carry):
        off = pl.multiple_of(kt * tk, tk)
        ki_t = ki_ref[0, pl.ds(off, tk), :]
        acc = jnp.zeros((Q_BLOCK, tk), F32)
        for h in range(idx_heads):
            logits = _dot_nt(qi_ref[0, :, h * IDX_DIM:(h + 1) * IDX_DIM], ki_t)
            acc = acc + wi_ref[0, :, h:h + 1] * jnp.maximum(logits, 0.0)
        kpos = off + lax.broadcasted_iota(jnp.int32, (Q_BLOCK, tk), 1)
        key_ref[:, pl.ds(off, tk)] = _sortable_key(jnp.where(kpos < limit, acc, NEG_INF))
        return carry

    lax.fori_loop(0, n_kt, score_tile, 0)

    def bit_step(bi, t):
        cand = t + lax.shift_left(jnp.int32(1), 31 - bi)

        def count_tile(kt, c):
            off = pl.multiple_of(kt * tk, tk)
            return c + lane_fold(jnp.where(key_ref[:, pl.ds(off, tk)] >= cand, 1.0, 0.0))

        part = lax.fori_loop(0, n_kt, count_tile, jnp.zeros((Q_BLOCK, LANES), F32))
        cnt = jnp.sum(part, axis=-1, keepdims=True)
        return jnp.where(cnt >= float(topk), cand, t)

    thr = lax.fori_loop(0, 32, bit_step, jnp.full((Q_BLOCK, 1), INT32_MIN, jnp.int32))

    def tie_tile(kt, carry):
        gt, eq = carry
        off = pl.multiple_of(kt * tk, tk)
        keys = key_ref[:, pl.ds(off, tk)]
        kpos = off + lax.broadcasted_iota(jnp.int32, (Q_BLOCK, tk), 1)
        gt = gt + lane_fold(jnp.where(keys > thr, 1.0, 0.0))
        eq = eq + lane_fold(jnp.where((keys == thr) & (kpos < limit), 1.0, 0.0))
        return gt, eq

    zeros = jnp.zeros((Q_BLOCK, LANES), F32)
    gt, eq = lax.fori_loop(0, n_kt, tie_tile, (zeros, zeros))
    need = float(topk) - jnp.sum(gt, axis=-1, keepdims=True)
    surplus = jnp.max(jnp.sum(eq, axis=-1, keepdims=True) - need)

    @pl.when(surplus > 0.0)
    def _():
        ki_ = lax.broadcasted_iota(jnp.int32, (tk, tk), 0)
        kj_ = lax.broadcasted_iota(jnp.int32, (tk, tk), 1)
        before = (ki_ < kj_).astype(BF16)

        def drop_tile(kt, seen):
            off = pl.multiple_of(kt * tk, tk)
            keys = key_ref[:, pl.ds(off, tk)]
            kpos = off + lax.broadcasted_iota(jnp.int32, (Q_BLOCK, tk), 1)
            is_eq = (keys == thr) & (kpos < limit)
            eq_f = jnp.where(is_eq, 1.0, 0.0)
            rank = seen + _dot(eq_f.astype(BF16), before)
            key_ref[:, pl.ds(off, tk)] = jnp.where(is_eq & (rank >= need), keys - 1, keys)
            return seen + jnp.sum(eq_f, axis=-1, keepdims=True)

        lax.fori_loop(0, n_kt, drop_tile, jnp.zeros((Q_BLOCK, 1), F32))

    for h in range(heads):
        qs_ref[h * Q_BLOCK:(h + 1) * Q_BLOCK, :] = qb_ref[0, :, h * HEAD_DIM:(h + 1) * HEAD_DIM]
    m_ref[...] = jnp.full(m_ref.shape, NEG_INF, F32)
    l_ref[...] = jnp.zeros(l_ref.shape, F32)
    acc_ref[...] = jnp.zeros(acc_ref.shape, F32)

    def attend_tile(kt, carry):
        off = pl.multiple_of(kt * tk, tk)
        k_t = k_ref[0, pl.ds(off, tk), :]
        v_t = v_ref[0, pl.ds(off, tk), :]
        kpos = off + lax.broadcasted_iota(jnp.int32, (Q_BLOCK, tk), 1)
        sel = (key_ref[:, pl.ds(off, tk)] >= thr) & (kpos < limit)
        sel = jnp.where(sel, 0.0, NEG_INF)
        for g in range(0, heads, DSA_GROUP):
            _flash_update(qs_ref, [h * Q_BLOCK for h in range(g, g + DSA_GROUP)], k_t, v_t, sel,
                          m_ref, l_ref, acc_ref)
        return carry

    lax.fori_loop(0, n_kt, attend_tile, 0)
    for h in range(heads):
        rows = slice(h * Q_BLOCK, (h + 1) * Q_BLOCK)
        o_ref[0, :, h * HEAD_DIM:(h + 1) * HEAD_DIM] = (
            acc_ref[rows, :] / l_ref[rows, :]).astype(o_ref.dtype)


def _dsa(u_q, wi, kb, vb, ki, heads, idx_heads):
    b, t, _ = u_q.shape
    assert heads * HEAD_DIM == idx_heads * IDX_DIM and heads % DSA_GROUP == 0
    qw = heads * HEAD_DIM
    topk = min(TOPK_MAX, t // 4)
    assert t % DSA_TK == 0 and topk <= DSA_TK
    nq = t // Q_BLOCK
    return pl.pallas_call(
        functools.partial(_dsa_kernel, heads=heads, idx_heads=idx_heads, topk=topk),
        grid=(b, nq),
        in_specs=[pl.BlockSpec((1, Q_BLOCK, qw), lambda bi, i: (bi, i, 0)),
                  pl.BlockSpec((1, Q_BLOCK, qw), lambda bi, i: (bi, i, 1)),
                  pl.BlockSpec((1, Q_BLOCK, idx_heads), lambda bi, i: (bi, i, 0)),
                  pl.BlockSpec((1, t, HEAD_DIM), lambda bi, i: (bi, 0, 0)),
                  pl.BlockSpec((1, t, HEAD_DIM), lambda bi, i: (bi, 0, 0)),
                  pl.BlockSpec((1, t, IDX_DIM), lambda bi, i: (bi, 0, 0))],
        out_specs=pl.BlockSpec((1, Q_BLOCK, qw), lambda bi, i: (bi, i, 0)),
        out_shape=jax.ShapeDtypeStruct((b, t, qw), BF16),
        scratch_shapes=[pltpu.VMEM((Q_BLOCK, t), jnp.int32),
                        pltpu.VMEM((heads * Q_BLOCK, HEAD_DIM), BF16),
                        pltpu.VMEM((heads * Q_BLOCK, LANES), F32),
                        pltpu.VMEM((heads * Q_BLOCK, LANES), F32),
                        pltpu.VMEM((heads * Q_BLOCK, HEAD_DIM), F32)],
        compiler_params=_cparams("parallel", "arbitrary"),
        name="dsa_attention",
    )(u_q, u_q, wi, kb, vb, ki)


def _shifted(x, prev_row, first):
    rolled = pltpu.roll(x, 1, axis=0)
    head = jnp.where(first, jnp.zeros_like(prev_row), prev_row)
    rid = lax.broadcasted_iota(jnp.int32, x.shape, 0)
    return jnp.where(rid == 0, head, rolled)


def _sigmoid(x):
    return 1.0 / (1.0 + jnp.exp(-x))


def _rwkv_prep_kernel(rkv_ref, rkv_prev_ref, lo_ref, lo_prev_ref, mu_rkv_ref, mu_lo_ref,
                      w0_ref, a0_ref, wup_ref, aup_ref, gup_ref,
                      r_ref, k_ref, v_ref, lw_ref, a_ref, g_ref, *, blocks_per_seq, width):
    first = (pl.program_id(0) % blocks_per_seq) == 0
    x = rkv_ref[...]
    prev = _shifted(x, rkv_prev_ref[7:8, :], first)
    x = x + mu_rkv_ref[...] * (prev - x)
    r_ref[...] = x[:, :width]
    k_ref[...] = x[:, width:2 * width]
    v_ref[...] = x[:, 2 * width:]

    lo = lo_ref[...]
    lo_prev = _shifted(lo, lo_prev_ref[7:8, :], first)
    lo = lo + mu_lo_ref[...] * (lo_prev - lo)
    xw = lo[:, :DECAY_LORA]
    xa = lo[:, DECAY_LORA:DECAY_LORA + ICLR_LORA]
    xg = lo[:, DECAY_LORA + ICLR_LORA:]
    z = -(w0_ref[...] + _dot(jnp.tanh(xw).astype(BF16), wup_ref[...]))
    softplus = jnp.maximum(z, 0.0) + jnp.log(1.0 + jnp.exp(-jnp.abs(z)))
    logw = -softplus - 0.5
    lw_ref[...] = -jnp.exp(logw)
    a_ref[...] = _sigmoid(a0_ref[...] + _dot(xa.astype(BF16), aup_ref[...]))
    g_ref[...] = _dot(_sigmoid(xg).astype(BF16), gup_ref[...])


def _rwkv_prep(u_rkv, u_lo, mu, w0, w_up, a0, a_up, g_up, rows_per_batch):
    m, w3 = u_rkv.shape
    width = w3 // 3
    lo_w = DECAY_LORA + ICLR_LORA + GATE_LORA
    tb = 128
    assert rows_per_batch % tb == 0
    blocks_per_seq = rows_per_batch // tb
    sub = tb // 8
    prev_map = lambda i: (jnp.maximum(i * sub - 1, 0), 0)
    row = lambda i: (i, 0)
    fixed = lambda i: (0, 0)
    out = jax.ShapeDtypeStruct((m, width), F32)
    return pl.pallas_call(
        functools.partial(_rwkv_prep_kernel, blocks_per_seq=blocks_per_seq, width=width),
        grid=(m // tb,),
        in_specs=[pl.BlockSpec((tb, w3), row),
                  pl.BlockSpec((8, w3), prev_map),
                  pl.BlockSpec((tb, lo_w), row),
                  pl.BlockSpec((8, lo_w), prev_map),
                  pl.BlockSpec((1, w3), fixed),
                  pl.BlockSpec((1, lo_w), fixed),
                  pl.BlockSpec((1, width), fixed),
                  pl.BlockSpec((1, width), fixed),
                  pl.BlockSpec((DECAY_LORA, width), fixed),
                  pl.BlockSpec((ICLR_LORA, width), fixed),
                  pl.BlockSpec((GATE_LORA, width), fixed)],
        out_specs=[pl.BlockSpec((tb, width), row)] * 6,
        out_shape=[out] * 6,
        compiler_params=_cparams("parallel"),
        name="rwkv_prep",
    )(u_rkv, u_rkv, u_lo, u_lo, mu[:w3].reshape(1, w3), mu[w3:].reshape(1, lo_w),
      w0.reshape(1, width), a0.reshape(1, width), w_up.astype(BF16), a_up.astype(BF16),
      g_up.astype(BF16))


WKV_L = 64
WKV_HEADS = 16


def _wkv_kernel(r_ref, k_ref, v_ref, lw_ref, a_ref, g_ref, kk_ref, ka_ref, rk_ref, lnw_ref,
                lnb_ref, o_ref, state_ref):
    L = WKV_L
    n = C_HEAD_DIM

    @pl.when(pl.program_id(2) == 0)
    def _():
        state_ref[...] = jnp.zeros(state_ref.shape, F32)

    ti = lax.broadcasted_iota(jnp.int32, (L, L), 0)
    si = lax.broadcasted_iota(jnp.int32, (L, L), 1)
    lower_strict = ti > si
    tri_ones = (ti >= si).astype(F32)
    eye = (ti == si).astype(F32)
    ti2 = lax.broadcasted_iota(jnp.int32, (L, 2 * L), 0)
    si2 = lax.broadcasted_iota(jnp.int32, (L, 2 * L), 1)
    lower_incl2 = ti2 >= jnp.where(si2 >= L, si2 - L, si2)

    heads = range(WKV_HEADS)
    sls = [slice(j * n, (j + 1) * n) for j in heads]

    def per_head(full, fn):
        return jnp.concatenate([jnp.broadcast_to(fn(full[:, sl]), (L, n)) for sl in sls], axis=1)

    r = r_ref[0]
    k = k_ref[0]
    v = v_ref[0]
    lw = lw_ref[0]
    a_lr = a_ref[0]
    kk = k * kk_ref[...]
    kk = kk / per_head(kk * kk, lambda z: jnp.maximum(
        jnp.sqrt(jnp.sum(z, axis=-1, keepdims=True)), 1e-12))
    k = k * (1.0 + (a_lr - 1.0) * ka_ref[...])
    b_s = kk * a_lr
    cum = _dot_f32(tri_ones, lw)
    c_end = cum[L - 1:L, :]
    c_inv = jnp.exp(-cum)
    c_out = jnp.exp(c_end - cum)
    a_t = (-kk * jnp.exp(cum - lw)).astype(BF16)
    r_t = (r * jnp.exp(cum)).astype(BF16)
    b_t = (b_s * c_inv).astype(BF16)
    k_t = (k * c_inv).astype(BF16)
    b_o = (b_s * c_out).astype(BF16)
    k_o = (k * c_out).astype(BF16)
    v_b = v.astype(BF16)
    s_decay = jnp.exp(c_end)

    ar = [jnp.concatenate([a_t[:, sl], r_t[:, sl]], axis=0) for sl in sls]
    bk = [jnp.concatenate([b_t[:, sl], k_t[:, sl]], axis=0) for sl in sls]
    bk_out = [jnp.concatenate([b_o[:, sl], k_o[:, sl]], axis=0) for sl in sls]
    vs = [v_b[:, sl] for sl in sls]
    prod = [_dot_nt(ar[j], bk[j]) for j in heads]
    n_ab = [jnp.where(lower_strict, p[:L, :L], 0.0) for p in prod]
    n_ak = [jnp.where(lower_strict, p[:L, L:], 0.0).astype(BF16) for p in prod]
    m_r = [jnp.where(lower_incl2, p[L:, :], 0.0).astype(BF16) for p in prod]

    inv = [eye + z for z in n_ab]
    pw = n_ab
    span = 1
    while 2 * span < L:
        pw_b = [z.astype(BF16) for z in pw]
        pw = [_dot(z, z) for z in pw_b]
        inv = [inv[j] + _dot(inv[j].astype(BF16), pw[j].astype(BF16)) for j in heads]
        span *= 2

    s0 = [state_ref[j] for j in heads]
    ah = [_dot_nt(ar[j], s0[j].astype(BF16)) for j in heads]
    rhs = [ah[j][:L] + _dot(n_ak[j], vs[j]) for j in heads]
    u = [_dot(inv[j].astype(BF16), rhs[j].astype(BF16)) for j in heads]
    uv = [jnp.concatenate([u[j].astype(BF16), vs[j]], axis=0) for j in heads]
    y = jnp.concatenate([ah[j][L:] + _dot(m_r[j], uv[j]) for j in heads], axis=1)
    for j in heads:
        s_inc = lax.dot_general(uv[j], bk_out[j], (((0,), (0,)), ((), ())),
                                preferred_element_type=F32)
        state_ref[j] = s0[j] * s_decay[:, sls[j]] + s_inc

    mean = per_head(y, lambda z: jnp.mean(z, axis=-1, keepdims=True))
    yc = y - mean
    var = per_head(yc * yc, lambda z: jnp.mean(z, axis=-1, keepdims=True))
    yn = yc * lax.rsqrt(var + C_GN_EPS) * lnw_ref[...] + lnb_ref[...]
    bonus = per_head(r * k * rk_ref[...], lambda z: jnp.sum(z, axis=-1, keepdims=True)) * v
    o_ref[0] = ((yn + bonus) * g_ref[0]).astype(o_ref.dtype)


def _wkv(r, k, v, lw, a, g, k_k, k_a, r_k, ln_w, ln_b, batch):
    m, width = r.shape
    t = m // batch
    hb = WKV_HEADS * C_HEAD_DIM
    assert t % WKV_L == 0 and width % hb == 0
    seq = lambda z: z.reshape(batch, t, width)
    par = lambda z: z.reshape(1, width).astype(F32)
    tok = pl.BlockSpec((1, WKV_L, hb), lambda bi, hg, c: (bi, c, hg))
    vec = pl.BlockSpec((1, hb), lambda bi, hg, c: (0, hg))
    out = pl.pallas_call(
        _wkv_kernel,
        grid=(batch, width // hb, t // WKV_L),
        in_specs=[tok] * 6 + [vec] * 5,
        out_specs=tok,
        out_shape=jax.ShapeDtypeStruct((batch, t, width), BF16),
        scratch_shapes=[pltpu.VMEM((WKV_HEADS, C_HEAD_DIM, C_HEAD_DIM), F32)],
        compiler_params=_cparams("parallel", "parallel", "arbitrary"),
        name="wkv7_chunked",
    )(seq(r), seq(k), seq(v), seq(lw), seq(a), seq(g), par(k_k), par(k_a), par(r_k),
      par(ln_w), par(ln_b))
    return out.reshape(m, width)


def _rope_tables(t):
    half = ROPE_DIM // 2
    freqs = ROPE_THETA ** (-jnp.arange(half, dtype=F32) / half)
    ang = jnp.arange(t, dtype=F32)[:, None] * freqs[None, :]
    cos, sin = jnp.cos(ang), jnp.sin(ang)
    zeros = jnp.zeros((t, LANES - ROPE_DIM), F32)
    return (jnp.concatenate([cos, cos, zeros], axis=-1),
            jnp.concatenate([-sin, sin, zeros], axis=-1))


def _rope128(x, cos, sin):
    half = ROPE_DIM // 2
    swapped = pltpu.roll(x, LANES - half, axis=1) + pltpu.roll(x, half, axis=1)
    return x * cos + swapped * sin


def _mla_prep_kernel(cq_ref, ckv_ref, kpe_ref, qn_ref, kvn_ref, cos_ref, sin_ref,
                     cq_o, ckv_o, kpe_o):
    cq = cq_ref[...]
    cq_o[...] = (cq * lax.rsqrt(jnp.mean(cq * cq, axis=-1, keepdims=True) + EPS)
                 * qn_ref[...]).astype(cq_o.dtype)
    ckv = ckv_ref[...]
    ckv_o[...] = (ckv * lax.rsqrt(jnp.mean(ckv * ckv, axis=-1, keepdims=True) + EPS)
                  * kvn_ref[...]).astype(ckv_o.dtype)
    lane = lax.broadcasted_iota(jnp.int32, kpe_ref.shape, 1)
    kpe = jnp.where(lane < ROPE_DIM, kpe_ref[...], 0.0)
    kpe_o[...] = _rope128(kpe, cos_ref[...], sin_ref[...]).astype(kpe_o.dtype)


def _mla_prep(u_lo, q_norm, kv_norm, cos, sin, q_lora, kv_lora, col0, rows_per_batch):
    m = u_lo.shape[0]
    tb = _pick(rows_per_batch, (256, 128, 64, 32, 16, 8))
    per = rows_per_batch // tb
    row = lambda i: (i, 0)
    fixed = lambda i: (0, 0)
    return pl.pallas_call(
        _mla_prep_kernel,
        grid=(m // tb,),
        in_specs=[pl.BlockSpec((tb, q_lora), row),
                  pl.BlockSpec((tb, kv_lora), row),
                  pl.BlockSpec((tb, LANES), row),
                  pl.BlockSpec((1, q_lora), fixed),
                  pl.BlockSpec((1, kv_lora), fixed),
                  pl.BlockSpec((tb, LANES), lambda i: (i % per, 0)),
                  pl.BlockSpec((tb, LANES), lambda i: (i % per, 0))],
        out_specs=[pl.BlockSpec((tb, q_lora), row),
                   pl.BlockSpec((tb, kv_lora), row),
                   pl.BlockSpec((tb, LANES), row)],
        out_shape=[jax.ShapeDtypeStruct((m, q_lora), BF16),
                   jax.ShapeDtypeStruct((m, kv_lora), BF16),
                   jax.ShapeDtypeStruct((m, LANES), BF16)],
        compiler_params=_cparams("parallel"),
        name="mla_prep",
    )(u_lo[:, col0:col0 + q_lora], u_lo[:, col0 + q_lora:col0 + q_lora + kv_lora],
      u_lo[:, col0 + q_lora + kv_lora:col0 + q_lora + kv_lora + LANES],
      q_norm.reshape(1, q_lora), kv_norm.reshape(1, kv_lora), cos, sin)


MLA_TQ = 1024
MLA_TK = 512


def _mla_kernel(q_ref, kv_ref, kpe_ref, cos_ref, sin_ref, o_ref, qc_ref, m_ref, l_ref, acc_ref):
    i = pl.program_id(2)
    tq, tk = MLA_TQ, MLA_TK
    n_full = (i * tq) // tk
    qc_ref[:, :NOPE_DIM] = q_ref[0, :, :NOPE_DIM]
    qc_ref[:, NOPE_DIM:] = _rope128(q_ref[0, :, NOPE_DIM:].astype(F32), cos_ref[...],
                                    sin_ref[...]).astype(BF16)
    m_ref[...] = jnp.full(m_ref.shape, NEG_INF, F32)
    l_ref[...] = jnp.zeros(l_ref.shape, F32)
    acc_ref[...] = jnp.zeros(acc_ref.shape, F32)

    def tile(kt, first_row):
        off = pl.multiple_of(kt * tk, tk)
        k_cat = jnp.concatenate([kv_ref[0, pl.ds(off, tk), :NOPE_DIM],
                                 kpe_ref[0, pl.ds(off, tk), :]], axis=1)
        v_t = kv_ref[0, pl.ds(off, tk), NOPE_DIM:]
        starts = list(range(first_row or 0, tq, FLASH_ROWS))
        mask = None
        if first_row is not None:
            row = lax.broadcasted_iota(jnp.int32, (FLASH_ROWS, 1), 0)
            kpos = off + lax.broadcasted_iota(jnp.int32, (FLASH_ROWS, tk), 1)
            mask = [kpos < ((i * tq + r + row) // CHUNK + 1) * CHUNK for r in starts]
        _flash_update(qc_ref, starts, k_cat, v_t, mask, m_ref, l_ref, acc_ref)

    def full_tile(kt, carry):
        tile(kt, None)
        return carry

    lax.fori_loop(0, n_full, full_tile, 0)
    for d in range(max(tq // tk, 1)):
        tile(n_full + d, d * tk)
    o_ref[0] = (acc_ref[...] / l_ref[...]).astype(o_ref.dtype)


def _mla(q, kv, kpe, cos, sin, heads):
    b, t, _ = q.shape
    assert t % MLA_TQ == 0 and t % MLA_TK == 0
    assert MLA_TK % MLA_TQ == 0 or MLA_TQ % MLA_TK == 0
    hw = NOPE_DIM + LANES
    return pl.pallas_call(
        _mla_kernel,
        grid=(b, heads, t // MLA_TQ),
        in_specs=[pl.BlockSpec((1, MLA_TQ, hw), lambda bi, h, i: (bi, i, h)),
                  pl.BlockSpec((1, t, NOPE_DIM + V_DIM), lambda bi, h, i: (bi, 0, h)),
                  pl.BlockSpec((1, t, LANES), lambda bi, h, i: (bi, 0, 0)),
                  pl.BlockSpec((MLA_TQ, LANES), lambda bi, h, i: (i, 0)),
                  pl.BlockSpec((MLA_TQ, LANES), lambda bi, h, i: (i, 0))],
        out_specs=pl.BlockSpec((1, MLA_TQ, V_DIM), lambda bi, h, i: (bi, i, h)),
        out_shape=jax.ShapeDtypeStruct((b, t, heads * V_DIM), BF16),
        scratch_shapes=[pltpu.VMEM((MLA_TQ, hw), BF16),
                        pltpu.VMEM((MLA_TQ, LANES), F32),
                        pltpu.VMEM((MLA_TQ, LANES), F32),
                        pltpu.VMEM((MLA_TQ, V_DIM), F32)],
        compiler_params=_cparams("parallel", "parallel", "arbitrary"),
        name="mla_attention",
    )(q, kv, kpe, cos, sin)


def _pad_cols(w, n):
    return jnp.pad(w, ((0, 0), (0, n - w.shape[1])))


def _even_mixer(h, x, gate, w_in, w_out, rel_bias, batch):
    m, d = h.shape
    t = m // batch
    heads = d // (2 * HEAD_DIM)
    idx_heads = d // 128
    hw = heads * HEAD_DIM
    o_kb = 4 * hw
    o_qi = o_kb + 2 * HEAD_DIM
    o_ki = o_qi + idx_heads * IDX_DIM
    q_factor = LOG2_E * HEAD_DIM ** -0.5
    w_a = jnp.concatenate([w_in[:, :hw] * q_factor, w_in[:, hw:3 * hw]], axis=1)
    u_a = _matmul(h, w_a.astype(BF16), out_dtype=BF16)
    w_qb = w_in[:, 3 * hw:4 * hw] * q_factor
    u_q = _matmul(h, jnp.concatenate([w_qb, w_in[:, o_qi:o_ki]], axis=1).astype(BF16),
                  out_dtype=BF16)
    small = jnp.concatenate([w_in[:, o_kb:o_qi], w_in[:, o_ki:]], axis=1)
    u_s = _matmul(h, _pad_cols(small, 3 * LANES).astype(BF16), out_dtype=F32, tn=3 * LANES)
    u_s = u_s.reshape(batch, t, 3 * LANES)
    kb = u_s[..., :HEAD_DIM].astype(BF16)
    vb = u_s[..., HEAD_DIM:2 * HEAD_DIM].astype(BF16)
    ki = u_s[..., 2 * HEAD_DIM:2 * HEAD_DIM + IDX_DIM].astype(BF16)
    wi = u_s[..., 2 * HEAD_DIM + IDX_DIM:2 * HEAD_DIM + IDX_DIM + idx_heads]
    ya = _attn_a(u_a.reshape(batch, t, 3 * hw), _band_bias_tiles(rel_bias), heads,
                 0, heads, 2 * heads)
    yb = _dsa(u_q.reshape(batch, t, 2 * hw), wi, kb, vb, ki, heads, idx_heads)
    return _matmul((ya.reshape(m, hw), yb.reshape(m, hw)), w_out.astype(BF16), out_dtype=F32,
                   epilogue="resid", x=x, gate=gate, rows_per_batch=t)


def _odd_mixer(h, x, gate, w_in, w_out, c_mu, c_w0, c_w_up, c_a0, c_a_up, c_g_up, c_k_k, c_k_a,
               c_r_k, c_ln_w, c_ln_b, d_q_norm, d_kv_norm, d_w_uq, d_w_ukv, batch):
    m, d = h.shape
    t = m // batch
    width = c_w0.shape[0]
    heads = d // (2 * HEAD_DIM)
    q_lora = d_q_norm.shape[0]
    kv_lora = d_kv_norm.shape[0]
    lo_w = DECAY_LORA + ICLR_LORA + GATE_LORA
    w_in = w_in.astype(BF16)
    u_rkv = _matmul(h, w_in[:, :3 * width], out_dtype=F32)
    rest = w_in[:, 3 * width:]
    rest_cols = -(-(rest.shape[1] + LANES - ROPE_DIM) // 512) * 512
    u_lo = _matmul(h, _pad_cols(rest, rest_cols), out_dtype=F32)

    r, k, v, lw, a, g = _rwkv_prep(u_rkv, u_lo, c_mu, c_w0, c_w_up, c_a0, c_a_up, c_g_up, t)
    yc = _wkv(r, k, v, lw, a, g, c_k_k, c_k_a, c_r_k, c_ln_w, c_ln_b, batch)

    cos, sin = _rope_tables(t)
    cq, ckv, kpe = _mla_prep(u_lo, d_q_norm, d_kv_norm, cos, sin, q_lora, kv_lora, lo_w, t)
    w_uq = d_w_uq.reshape(q_lora, heads, NOPE_DIM + ROPE_DIM)
    w_uq = jnp.pad(w_uq, ((0, 0), (0, 0), (0, LANES - ROPE_DIM))).reshape(q_lora, -1)
    w_uq = w_uq * (LOG2_E * (NOPE_DIM + ROPE_DIM) ** -0.5)
    q = _matmul(cq, w_uq.astype(BF16), out_dtype=BF16)
    kv = _matmul(ckv, d_w_ukv.astype(BF16), out_dtype=BF16)
    yd = _mla(q.reshape(batch, t, -1), kv.reshape(batch, t, -1), kpe.reshape(batch, t, LANES),
              cos, sin, heads)
    return _matmul((yc, yd.reshape(m, -1)), w_out.astype(BF16), out_dtype=F32,
                   epilogue="resid", x=x, gate=gate, rows_per_batch=t)


def kernel(x, c, ada_w, ada_table, norm_mix, norm_ffn, norm_final, ffn_w1, ffn_w2, ev_w_in,
           ev_w_out, a_rel_bias, od_w_in, od_w_out, c_mu, c_w0, c_w_up, c_a0, c_a_up, c_g_up,
           c_k_k, c_k_a, c_r_k, c_ln_w, c_ln_b, d_q_norm, d_kv_norm, d_w_uq, d_w_ukv):
    batch, t, d = x.shape
    depth = ada_table.shape[0]
    m = batch * t
    silu_c = (c * _sigmoid(c)).astype(BF16)
    silu_c = jnp.pad(silu_c, ((0, 8 - batch % 8 if batch % 8 else 0), (0, 0)))
    ada = _matmul(silu_c, ada_w.astype(BF16), out_dtype=F32, tn=2048)[:batch].reshape(batch, 6, d)
    x = x.reshape(m, d)
    for layer in range(depth):
        mod = ada + ada_table[layer]
        shift_m, scale_m, gate_m, shift_f, scale_f, gate_f = (mod[:, i, :] for i in range(6))
        h = _norm_mod(x, norm_mix[layer], scale_m, shift_m, t)
        if layer % 2 == 0:
            e = layer // 2
            x = _even_mixer(h, x, gate_m, ev_w_in[e], ev_w_out[e], a_rel_bias[e], batch)
        else:
            o = layer // 2
            x = _odd_mixer(h, x, gate_m, od_w_in[o], od_w_out[o], c_mu[o], c_w0[o], c_w_up[o],
                           c_a0[o], c_a_up[o], c_g_up[o], c_k_k[o], c_k_a[o], c_r_k[o],
                           c_ln_w[o], c_ln_b[o], d_q_norm[o], d_kv_norm[o], d_w_uq[o],
                           d_w_ukv[o], batch)
        h = _norm_mod(x, norm_ffn[layer], scale_f, shift_f, t)
        h1 = _matmul(h, ffn_w1[layer], out_dtype=BF16, epilogue="relu2")
        x = _matmul(h1, ffn_w2[layer].astype(BF16), out_dtype=F32, epilogue="resid", x=x,
                    gate=gate_f, rows_per_batch=t)
    return _norm_plain(x, norm_final, F32).reshape(batch, t, d)
```

```python
import functools

import numpy as np
import jax
import jax.numpy as jnp
from jax import lax
from jax.experimental import pallas as pl
from jax.experimental.pallas import tpu as pltpu

F32 = jnp.float32
BF16 = jnp.bfloat16

CHUNK = 64
Q_BLOCK = 128
HEAD_DIM = 128
EPS = 1e-6
NEG_INF = -1e30
A_LEFT_CHUNKS = 8
A_MAX_REL = 256
IDX_DIM = 64
TOPK_MAX = 256
C_HEAD_DIM = 64
DECAY_LORA = 128
ICLR_LORA = 128
GATE_LORA = 256
C_GN_EPS = 64e-5
NOPE_DIM = 128
ROPE_DIM = 64
V_DIM = 128
ROPE_THETA = 10000.0

V7X_VMEM_LIMIT_BYTES = 56 * 1024 * 1024
LANES = 128
INT32_MIN = -(2 ** 31)
LOG2_E = 1.4426950408889634


def _cparams(*sem):
    return pltpu.CompilerParams(dimension_semantics=sem, vmem_limit_bytes=V7X_VMEM_LIMIT_BYTES)


def _dot_nt(a, b):
    return lax.dot_general(a, b, (((1,), (1,)), ((), ())), preferred_element_type=F32)


def _dot(a, b):
    return jnp.dot(a, b, preferred_element_type=F32)


def _dot_f32(a, b):
    return jnp.dot(a, b, preferred_element_type=F32, precision=lax.Precision.HIGHEST)


FLASH_ROWS = 128


def _flash_update(q_ref, row_starts, k_t, v_t, mask, m_ref, l_ref, acc_ref):
    tk = k_t.shape[0]
    reps = tk // LANES
    rows = [slice(r, r + FLASH_ROWS) for r in row_starts]
    s = [_dot_nt(q_ref[r, :], k_t) for r in rows]
    if isinstance(mask, list):
        s = [jnp.where(mk, x, NEG_INF) for mk, x in zip(mask, s)]
    elif mask is not None:
        s = [x + mask for x in s]
    m_old = [m_ref[r, :] for r in rows]
    m_new = [jnp.maximum(mo, jnp.max(x, axis=-1, keepdims=True)) for mo, x in zip(m_old, s)]
    p = [jnp.exp2(x - jnp.concatenate([mn] * reps, axis=1)) for x, mn in zip(s, m_new)]
    alpha = [jnp.exp2(mo - mn) for mo, mn in zip(m_old, m_new)]
    pv = [_dot(x.astype(v_t.dtype), v_t) for x in p]
    for r, mn, al, x, y in zip(rows, m_new, alpha, p, pv):
        m_ref[r, :] = mn
        l_ref[r, :] = al * l_ref[r, :] + jnp.sum(x, axis=-1, keepdims=True)
        acc_ref[r, :] = al * acc_ref[r, :] + y


def _pick(n, prefs):
    for p in prefs:
        if n % p == 0:
            return p
    return n


MM_ACC_CHUNK = 256

def _mm_kernel(*refs, nk, n_lhs, epilogue):
    a_refs = refs[:n_lhs]
    w_ref = refs[n_lhs]
    if epilogue == "resid":
        x_ref, g_ref, o_ref = refs[n_lhs + 1:n_lhs + 4]
        scratch = refs[n_lhs + 4:]
    else:
        o_ref = refs[n_lhs + 1]
        scratch = refs[n_lhs + 2:]

    tn = o_ref.shape[1]

    def finish(acc, cols):
        if epilogue == "relu2":
            r = jnp.maximum(acc, 0.0)
            acc = r * r
        elif epilogue == "resid":
            acc = x_ref[:, cols] + g_ref[0, :, cols] * acc
        o_ref[:, cols] = acc.astype(o_ref.dtype)

    def product(cols):
        part, row = None, 0
        for a_ref in a_refs:
            kw = a_ref.shape[1]
            term = _dot(a_ref[...], w_ref[row:row + kw, cols].astype(BF16))
            part = term if part is None else part + term
            row += kw
        return part

    if nk == 1:
        finish(product(slice(0, tn)), slice(0, tn))
        return

    acc_ref = scratch[0]
    k = pl.program_id(2)
    chunks = [slice(c, c + MM_ACC_CHUNK) for c in range(0, tn, MM_ACC_CHUNK)]

    @pl.when(k == 0)
    def _():
        for cols in chunks:
            acc_ref[:, cols] = product(cols)

    @pl.when((k > 0) & (k < nk - 1))
    def _():
        for cols in chunks:
            acc_ref[:, cols] += product(cols)

    @pl.when(k == nk - 1)
    def _():
        for cols in chunks:
            finish(acc_ref[:, cols] + product(cols), cols)


def _matmul(a, w, *, out_dtype, w_layer=None, epilogue="none", x=None, gate=None,
            rows_per_batch=None, tm=None, tn=None, tk=None):
    lhs = a if isinstance(a, tuple) else (a,)
    m = lhs[0].shape[0]
    assert (w.ndim == 3) == (w_layer is not None)
    kd, n = w.shape[-2:]
    assert sum(z.shape[1] for z in lhs) == kd
    tm = tm or _pick(m, (1024, 512, 256, 128, 64, 32, 16, 8))
    tk = tk or (kd if kd <= 4096 else _pick(kd, (2048, 1024, 512)))
    tn = tn or _pick(n, (1024, 512, 384, 256, 128) if tk < kd else (512, 384, 256, 128))
    nk = kd // tk
    assert m % tm == 0 and n % tn == 0 and kd % tk == 0 and (nk == 1 or len(lhs) == 1)
    in_specs = [pl.BlockSpec((tm, z.shape[1] if nk == 1 else tk), lambda i, j, k: (i, k))
                for z in lhs]
    if w_layer is None:
        in_specs.append(pl.BlockSpec((tk, tn), lambda i, j, k: (k, j)))
    else:
        in_specs.append(pl.BlockSpec((None, tk, tn), lambda i, j, k: (w_layer, k, j)))
    args = [*lhs, w]
    if epilogue == "resid":
        assert rows_per_batch % tm == 0
        per = rows_per_batch // tm
        in_specs += [pl.BlockSpec((tm, tn), lambda i, j, k: (i, j)),
                     pl.BlockSpec((1, 1, tn), lambda i, j, k: (i // per, 0, j))]
        args += [x, gate.reshape(gate.shape[0], 1, n)]
    return pl.pallas_call(
        functools.partial(_mm_kernel, nk=nk, n_lhs=len(lhs), epilogue=epilogue),
        grid=(m // tm, n // tn, nk),
        in_specs=in_specs,
        out_specs=pl.BlockSpec((tm, tn), lambda i, j, k: (i, j)),
        out_shape=jax.ShapeDtypeStruct((m, n), out_dtype),
        scratch_shapes=[pltpu.VMEM((tm, tn), F32)] if nk > 1 else [],
        compiler_params=_cparams("parallel", "parallel", "arbitrary"),
        name="matmul_" + epilogue,
    )(*args)


def _norm_mod_kernel(x_ref, g_ref, sc_ref, sh_ref, o_ref):
    x = x_ref[...]
    ms = jnp.mean(x * x, axis=-1, keepdims=True)
    y = x * lax.rsqrt(ms + EPS) * g_ref[...]
    o_ref[...] = (y * (1.0 + sc_ref[0]) + sh_ref[0]).astype(o_ref.dtype)


def _norm_plain_kernel(x_ref, g_ref, o_ref):
    x = x_ref[...]
    ms = jnp.mean(x * x, axis=-1, keepdims=True)
    o_ref[...] = (x * lax.rsqrt(ms + EPS) * g_ref[...]).astype(o_ref.dtype)


def _norm_mod(x, g, scale, shift, rows_per_batch, out_dtype=BF16):
    m, d = x.shape
    tr = _pick(rows_per_batch, (256, 128, 64, 32, 16, 8))
    per = rows_per_batch // tr
    b = scale.shape[0]
    return pl.pallas_call(
        _norm_mod_kernel,
        grid=(m // tr,),
        in_specs=[pl.BlockSpec((tr, d), lambda i: (i, 0)),
                  pl.BlockSpec((1, d), lambda i: (0, 0)),
                  pl.BlockSpec((1, 1, d), lambda i: (i // per, 0, 0)),
                  pl.BlockSpec((1, 1, d), lambda i: (i // per, 0, 0))],
        out_specs=pl.BlockSpec((tr, d), lambda i: (i, 0)),
        out_shape=jax.ShapeDtypeStruct((m, d), out_dtype),
        compiler_params=_cparams("parallel"),
        name="norm_mod",
    )(x, g.reshape(1, d), scale.reshape(b, 1, d), shift.reshape(b, 1, d))


def _norm_plain(x, g, out_dtype):
    m, d = x.shape
    tr = _pick(m, (256, 128, 64, 32, 16, 8))
    return pl.pallas_call(
        _norm_plain_kernel,
        grid=(m // tr,),
        in_specs=[pl.BlockSpec((tr, d), lambda i: (i, 0)),
                  pl.BlockSpec((1, d), lambda i: (0, 0))],
        out_specs=pl.BlockSpec((tr, d), lambda i: (i, 0)),
        out_shape=jax.ShapeDtypeStruct((m, d), out_dtype),
        compiler_params=_cparams("parallel"),
        name="norm_plain",
    )(x, g.reshape(1, d))


A_PAD = A_LEFT_CHUNKS * CHUNK
A_WIN = A_PAD + Q_BLOCK
A_VARIANTS = A_PAD // Q_BLOCK + 1


def _band_bias_tiles(rel_bias):
    width = A_WIN + Q_BLOCK
    tiles = []
    for v in range(A_VARIANTS):
        rel = np.clip(v * Q_BLOCK + Q_BLOCK - 1 - np.arange(width), -A_MAX_REL, A_MAX_REL)
        diag = jnp.pad(rel_bias.astype(F32)[:, rel + A_MAX_REL] * LOG2_E, ((0, 0), (0, 1)))
        skew = jnp.tile(diag, (1, Q_BLOCK))[:, :Q_BLOCK * width].reshape(-1, Q_BLOCK, width)
        tiles.append(skew[:, :, Q_BLOCK - 1:Q_BLOCK - 1 + A_WIN])
    return jnp.stack(tiles, axis=1)


A_HEADS_PER_STEP = 4


def _attn_a_kernel(q_ref, k_ref, v_ref, b_ref, o_ref):
    i = pl.program_id(2)
    ks = pl.multiple_of(jnp.maximum(i * Q_BLOCK - A_PAD, 0), Q_BLOCK)
    qpos = i * Q_BLOCK + lax.broadcasted_iota(jnp.int32, (Q_BLOCK, A_WIN), 0)
    kpos = ks + lax.broadcasted_iota(jnp.int32, (Q_BLOCK, A_WIN), 1)
    chunk_start = (qpos // CHUNK) * CHUNK
    valid = (kpos >= chunk_start - A_PAD) & (kpos < chunk_start + CHUNK)
    cols = [slice(h * HEAD_DIM, (h + 1) * HEAD_DIM) for h in range(A_HEADS_PER_STEP)]
    vs = [v_ref[0, pl.ds(ks, A_WIN), c] for c in cols]
    s = [_dot_nt(q_ref[0, :, c], k_ref[0, pl.ds(ks, A_WIN), c]) + b_ref[h, 0]
         for h, c in enumerate(cols)]
    s = [jnp.where(valid, x, NEG_INF) for x in s]
    p = [jnp.exp2(x - jnp.max(x, axis=-1, keepdims=True)) for x in s]
    l = [jnp.sum(x, axis=-1, keepdims=True) for x in p]
    o = [_dot(x.astype(v.dtype), v) for x, v in zip(p, vs)]
    for c, x, y in zip(cols, o, l):
        o_ref[0, :, c] = (x / y).astype(o_ref.dtype)


def _attn_a(u, bias_tiles, heads, q_col, k_col, v_col):
    b, t, _ = u.shape
    hb = A_HEADS_PER_STEP
    assert t % Q_BLOCK == 0 and t >= A_WIN
    assert heads % hb == 0 and q_col % hb == 0 and k_col % hb == 0 and v_col % hb == 0
    nq = t // Q_BLOCK
    return pl.pallas_call(
        _attn_a_kernel,
        grid=(b, heads // hb, nq),
        in_specs=[pl.BlockSpec((1, Q_BLOCK, hb * HEAD_DIM),
                               lambda bi, h, i: (bi, i, q_col // hb + h)),
                  pl.BlockSpec((1, t, hb * HEAD_DIM), lambda bi, h, i: (bi, 0, k_col // hb + h)),
                  pl.BlockSpec((1, t, hb * HEAD_DIM), lambda bi, h, i: (bi, 0, v_col // hb + h)),
                  pl.BlockSpec((hb, 1, Q_BLOCK, A_WIN),
                               lambda bi, h, i: (h, jnp.minimum(i, A_VARIANTS - 1), 0, 0))],
        out_specs=pl.BlockSpec((1, Q_BLOCK, hb * HEAD_DIM), lambda bi, h, i: (bi, i, h)),
        out_shape=jax.ShapeDtypeStruct((b, t, heads * HEAD_DIM), BF16),
        compiler_params=_cparams("parallel", "parallel", "arbitrary"),
        name="band_attention",
    )(u, u, u, bias_tiles)


DSA_TK = 512


def _sortable_key(x):
    bits = lax.bitcast_convert_type(x, jnp.int32)
    return jnp.where(bits < 0, bits ^ jnp.int32(0x7FFFFFFF), bits)


DSA_GROUP = 8


def _dsa_kernel(qb_ref, qi_ref, wi_ref, k_ref, v_ref, ki_ref, o_ref,
                key_ref, qs_ref, m_ref, l_ref, acc_ref, *, heads, idx_heads, topk):
    i = pl.program_id(1)
    tk = DSA_TK
    n_kt = (i * Q_BLOCK + Q_BLOCK + tk - 1) // tk
    row = lax.broadcasted_iota(jnp.int32, (Q_BLOCK, 1), 0)
    limit = ((i * Q_BLOCK + row) // CHUNK + 1) * CHUNK

    def lane_fold(x):
        out = x[:, :LANES]
        for c in range(1, tk // LANES):
            out = out + x[:, c * LANES:(c + 1) * LANES]
        return out

    def score_tile(kt, carry):
        off = pl.multiple_of(kt * tk, tk)
        ki_t = ki_ref[0, pl.ds(off, tk), :]
        acc = jnp.zeros((Q_BLOCK, tk), F32)
        for h in range(idx_heads):
            logits = _dot_nt(qi_ref[0, :, h * IDX_DIM:(h + 1) * IDX_DIM], ki_t)
            acc = acc + wi_ref[0, :, h:h + 1] * jnp.maximum(logits, 0.0)
        kpos = off + lax.broadcasted_iota(jnp.int32, (Q_BLOCK, tk), 1)
        key_ref[:, pl.ds(off, tk)] = _sortable_key(jnp.where(kpos < limit, acc, NEG_INF))
        return carry

    lax.fori_loop(0, n_kt, score_tile, 0)

    def bit_step(bi, t):
        cand = t + lax.shift_left(jnp.int32(1), 31 - bi)

        def count_tile(kt, c):
            off = pl.multiple_of(kt * tk, tk)
            return c + lane_fold(jnp.where(key_ref[:, pl.ds(off, tk)] >= cand, 1.0, 0.0))

        part = lax.fori_loop(0, n_kt, count_tile, jnp.zeros((Q_BLOCK, LANES), F32))
        cnt = jnp.sum(part, axis=-1, keepdims=True)
        return jnp.where(cnt >= float(topk), cand, t)

    thr = lax.fori_loop(0, 32, bit_step, jnp.full((Q_BLOCK, 1), INT32_MIN, jnp.int32))

    def tie_tile(kt, carry):
        gt, eq = carry
        off = pl.multiple_of(kt * tk, tk)
        keys = key_ref[:, pl.ds(off, tk)]
        kpos = off + lax.broadcasted_iota(jnp.int32, (Q_BLOCK, tk), 1)
        gt = gt + lane_fold(jnp.where(keys > thr, 1.0, 0.0))
        eq = eq + lane_fold(jnp.where((keys == thr) & (kpos < limit), 1.0, 0.0))
        return gt, eq

    zeros = jnp.zeros((Q_BLOCK, LANES), F32)
    gt, eq = lax.fori_loop(0, n_kt, tie_tile, (zeros, zeros))
    need = float(topk) - jnp.sum(gt, axis=-1, keepdims=True)
    surplus = jnp.max(jnp.sum(eq, axis=-1, keepdims=True) - need)

    @pl.when(surplus > 0.0)
    def _():
        ki_ = lax.broadcasted_iota(jnp.int32, (tk, tk), 0)
        kj_ = lax.broadcasted_iota(jnp.int32, (tk, tk), 1)
        before = (ki_ < kj_).astype(BF16)

        def drop_tile(kt, seen):
            off = pl.multiple_of(kt * tk, tk)
            keys = key_ref[:, pl.ds(off, tk)]
            kpos = off + lax.broadcasted_iota(jnp.int32, (Q_BLOCK, tk), 1)
            is_eq = (keys == thr) & (kpos < limit)
            eq_f = jnp.where(is_eq, 1.0, 0.0)
            rank = seen + _dot(eq_f.astype(BF16), before)
            key_ref[:, pl.ds(off, tk)] = jnp.where(is_eq & (rank >= need), keys - 1, keys)
            return seen + jnp.sum(eq_f, axis=-1, keepdims=True)

        lax.fori_loop(0, n_kt, drop_tile, jnp.zeros((Q_BLOCK, 1), F32))

    for h in range(heads):
        qs_ref[h * Q_BLOCK:(h + 1) * Q_BLOCK, :] = qb_ref[0, :, h * HEAD_DIM:(h + 1) * HEAD_DIM]
    m_ref[...] = jnp.full(m_ref.shape, NEG_INF, F32)
    l_ref[...] = jnp.zeros(l_ref.shape, F32)
    acc_ref[...] = jnp.zeros(acc_ref.shape, F32)

    def attend_tile(kt, carry):
        off = pl.multiple_of(kt * tk, tk)
        k_t = k_ref[0, pl.ds(off, tk), :]
        v_t = v_ref[0, pl.ds(off, tk), :]
        kpos = off + lax.broadcasted_iota(jnp.int32, (Q_BLOCK, tk), 1)
        sel = (key_ref[:, pl.ds(off, tk)] >= thr) & (kpos < limit)
        sel = jnp.where(sel, 0.0, NEG_INF)
        for g in range(0, heads, DSA_GROUP):
            _flash_update(qs_ref, [h * Q_BLOCK for h in range(g, g + DSA_GROUP)], k_t, v_t, sel,
                          m_ref, l_ref, acc_ref)
        return carry

    lax.fori_loop(0, n_kt, attend_tile, 0)
    for h in range(heads):
        rows = slice(h * Q_BLOCK, (h + 1) * Q_BLOCK)
        o_ref[0, :, h * HEAD_DIM:(h + 1) * HEAD_DIM] = (
            acc_ref[rows, :] / l_ref[rows, :]).astype(o_ref.dtype)


def _dsa(u_q, wi, kb, vb, ki, heads, idx_heads):
    b, t, _ = u_q.shape
    assert heads * HEAD_DIM == idx_heads * IDX_DIM and heads % DSA_GROUP == 0
    qw = heads * HEAD_DIM
    topk = min(TOPK_MAX, t // 4)
    assert t % DSA_TK == 0 and topk <= DSA_TK
    nq = t // Q_BLOCK
    return pl.pallas_call(
        functools.partial(_dsa_kernel, heads=heads, idx_heads=idx_heads, topk=topk),
        grid=(b, nq),
        in_specs=[pl.BlockSpec((1, Q_BLOCK, qw), lambda bi, i: (bi, i, 0)),
                  pl.BlockSpec((1, Q_BLOCK, qw), lambda bi, i: (bi, i, 1)),
                  pl.BlockSpec((1, Q_BLOCK, idx_heads), lambda bi, i: (bi, i, 0)),
                  pl.BlockSpec((1, t, HEAD_DIM), lambda bi, i: (bi, 0, 0)),
                  pl.BlockSpec((1, t, HEAD_DIM), lambda bi, i: (bi, 0, 0)),
                  pl.BlockSpec((1, t, IDX_DIM), lambda bi, i: (bi, 0, 0))],
        out_specs=pl.BlockSpec((1, Q_BLOCK, qw), lambda bi, i: (bi, i, 0)),
        out_shape=jax.ShapeDtypeStruct((b, t, qw), BF16),
        scratch_shapes=[pltpu.VMEM((Q_BLOCK, t), jnp.int32),
                        pltpu.VMEM((heads * Q_BLOCK, HEAD_DIM), BF16),
                        pltpu.VMEM((heads * Q_BLOCK, LANES), F32),
                        pltpu.VMEM((heads * Q_BLOCK, LANES), F32),
                        pltpu.VMEM((heads * Q_BLOCK, HEAD_DIM), F32)],
        compiler_params=_cparams("parallel", "arbitrary"),
        name="dsa_attention",
    )(u_q, u_q, wi, kb, vb, ki)


def _shifted(x, prev_row, first):
    rolled = pltpu.roll(x, 1, axis=0)
    head = jnp.where(first, jnp.zeros_like(prev_row), prev_row)
    rid = lax.broadcasted_iota(jnp.int32, x.shape, 0)
    return jnp.where(rid == 0, head, rolled)


def _sigmoid(x):
    return 1.0 / (1.0 + jnp.exp(-x))


PREV_ROWS = 16


def _rwkv_prep_kernel(rkv_ref, rkv_prev_ref, lo_ref, lo_prev_ref, mu_rkv_ref, mu_lo_ref,
                      w0_ref, a0_ref, wup_ref, aup_ref, gup_ref,
                      r_ref, k_ref, v_ref, lw_ref, a_ref, g_ref, *, blocks_per_seq, width):
    first = (pl.program_id(0) % blocks_per_seq) == 0
    x = rkv_ref[...].astype(F32)
    prev = _shifted(x, rkv_prev_ref[PREV_ROWS - 1:PREV_ROWS, :].astype(F32), first)
    x = x + mu_rkv_ref[...] * (prev - x)
    r_ref[...] = x[:, :width].astype(r_ref.dtype)
    k_ref[...] = x[:, width:2 * width].astype(k_ref.dtype)
    v_ref[...] = x[:, 2 * width:].astype(v_ref.dtype)

    lo = lo_ref[...]
    lo_prev = _shifted(lo, lo_prev_ref[PREV_ROWS - 1:PREV_ROWS, :], first)
    lo = lo + mu_lo_ref[...] * (lo_prev - lo)
    xw = lo[:, :DECAY_LORA]
    xa = lo[:, DECAY_LORA:DECAY_LORA + ICLR_LORA]
    xg = lo[:, DECAY_LORA + ICLR_LORA:]
    z = -(w0_ref[...] + _dot(jnp.tanh(xw).astype(BF16), wup_ref[...]))
    softplus = jnp.maximum(z, 0.0) + jnp.log(1.0 + jnp.exp(-jnp.abs(z)))
    logw = -softplus - 0.5
    lw_ref[...] = -jnp.exp(logw)
    a_ref[...] = _sigmoid(a0_ref[...] + _dot(xa.astype(BF16), aup_ref[...]))
    g_ref[...] = _dot(_sigmoid(xg).astype(BF16), gup_ref[...]).astype(g_ref.dtype)


def _rwkv_prep(u_rkv, u_lo, mu, w0, w_up, a0, a_up, g_up, rows_per_batch):
    m, w3 = u_rkv.shape
    width = w3 // 3
    lo_w = DECAY_LORA + ICLR_LORA + GATE_LORA
    tb = 128
    assert rows_per_batch % tb == 0
    blocks_per_seq = rows_per_batch // tb
    sub = tb // PREV_ROWS
    prev_map = lambda i: (jnp.maximum(i * sub - 1, 0), 0)
    row = lambda i: (i, 0)
    fixed = lambda i: (0, 0)
    wide = jax.ShapeDtypeStruct((m, width), F32)
    narrow = jax.ShapeDtypeStruct((m, width), BF16)
    return pl.pallas_call(
        functools.partial(_rwkv_prep_kernel, blocks_per_seq=blocks_per_seq, width=width),
        grid=(m // tb,),
        in_specs=[pl.BlockSpec((tb, w3), row),
                  pl.BlockSpec((PREV_ROWS, w3), prev_map),
                  pl.BlockSpec((tb, lo_w), row),
                  pl.BlockSpec((PREV_ROWS, lo_w), prev_map),
                  pl.BlockSpec((1, w3), fixed),
                  pl.BlockSpec((1, lo_w), fixed),
                  pl.BlockSpec((1, width), fixed),
                  pl.BlockSpec((1, width), fixed),
                  pl.BlockSpec((DECAY_LORA, width), fixed),
                  pl.BlockSpec((ICLR_LORA, width), fixed),
                  pl.BlockSpec((GATE_LORA, width), fixed)],
        out_specs=[pl.BlockSpec((tb, width), row)] * 6,
        out_shape=[narrow, narrow, narrow, wide, wide, narrow],
        compiler_params=_cparams("parallel"),
        name="rwkv_prep",
    )(u_rkv, u_rkv, u_lo, u_lo, mu[:w3].reshape(1, w3), mu[w3:].reshape(1, lo_w),
      w0.reshape(1, width), a0.reshape(1, width), w_up.astype(BF16), a_up.astype(BF16),
      g_up.astype(BF16))


WKV_L = 64
WKV_HEADS = 16


def _wkv_kernel(r_ref, k_ref, v_ref, lw_ref, a_ref, g_ref, kk_ref, ka_ref, rk_ref, lnw_ref,
                lnb_ref, o_ref, state_ref):
    L = WKV_L
    n = C_HEAD_DIM

    @pl.when(pl.program_id(2) == 0)
    def _():
        state_ref[...] = jnp.zeros(state_ref.shape, F32)

    ti = lax.broadcasted_iota(jnp.int32, (L, L), 0)
    si = lax.broadcasted_iota(jnp.int32, (L, L), 1)
    tri_ones = (ti >= si).astype(F32)
    P = 2 * L
    def block_masks(width):
        tp = lax.broadcasted_iota(jnp.int32, (P, width), 0)
        sp = lax.broadcasted_iota(jnp.int32, (P, width), 1)
        sp = jnp.where(sp >= P, sp - P, sp)
        same_head = ((tp >= L) & (sp >= L)) | ((tp < L) & (sp < L))
        return same_head & (tp > sp), same_head & (tp >= sp), tp == sp

    bd_strict, _, diag = block_masks(P)
    _, bd_incl2, _ = block_masks(2 * P)
    eye = diag.astype(F32)
    head0 = lax.broadcasted_iota(jnp.int32, (L, P), 1) < n

    heads = range(WKV_HEADS)
    sls = [slice(j * n, (j + 1) * n) for j in heads]

    def per_head(full, fn):
        return jnp.concatenate([jnp.broadcast_to(fn(full[:, sl]), (L, n)) for sl in sls], axis=1)

    r = r_ref[0].astype(F32)
    k = k_ref[0].astype(F32)
    v = v_ref[0].astype(F32)
    lw = lw_ref[0]
    a_lr = a_ref[0]
    kk = k * kk_ref[...]
    kk = kk / per_head(kk * kk, lambda z: jnp.maximum(
        jnp.sqrt(jnp.sum(z, axis=-1, keepdims=True)), 1e-12))
    k = k * (1.0 + (a_lr - 1.0) * ka_ref[...])
    b_s = kk * a_lr
    cum = _dot_f32(tri_ones, lw)
    c_end = cum[L - 1:L, :]
    c_inv = jnp.exp(-cum)
    c_out = jnp.exp(c_end - cum)
    a_t = -kk * jnp.exp(cum - lw)
    r_t = r * jnp.exp(cum)
    b_t = (b_s * c_inv).astype(BF16)
    k_t = (k * c_inv).astype(BF16)
    b_o = b_s * c_out
    k_o = k * c_out
    s_decay = jnp.exp(c_end)

    pairs = range(WKV_HEADS // 2)
    cols = [slice(p * P, (p + 1) * P) for p in pairs]

    def split(x):
        return [jnp.where(head0, x, 0.0).astype(BF16), jnp.where(head0, 0.0, x).astype(BF16)]

    ar4 = [jnp.concatenate(split(a_t[:, c]) + split(r_t[:, c]), axis=0) for c in cols]
    bk4 = [jnp.concatenate([b_t[:, c]] * 2 + [k_t[:, c]] * 2, axis=0) for c in cols]
    bko4 = [jnp.concatenate(split(b_o[:, c]) + split(k_o[:, c]), axis=0) for c in cols]
    vbd = [jnp.concatenate(split(v[:, c]), axis=0) for c in cols]
    prod = [_dot_nt(ar4[p], bk4[p]) for p in pairs]
    n_ab = [jnp.where(bd_strict, z[:P, :P], 0.0) for z in prod]
    n_ak = [jnp.where(bd_strict, z[:P, P:], 0.0).astype(BF16) for z in prod]
    m_r = [jnp.where(bd_incl2, z[P:, :], 0.0).astype(BF16) for z in prod]

    inv = [eye + z for z in n_ab]
    pw = n_ab
    span = 1
    while 2 * span < L:
        pw_b = [z.astype(BF16) for z in pw]
        pw = [_dot(z, z) for z in pw_b]
        inv = [inv[p] + _dot(inv[p].astype(BF16), pw[p].astype(BF16)) for p in pairs]
        span *= 2

    s0 = [state_ref[p] for p in pairs]
    ah = [_dot_nt(ar4[p], s0[p].astype(BF16)) for p in pairs]
    rhs = [ah[p][:P] + _dot(n_ak[p], vbd[p]) for p in pairs]
    u = [_dot(inv[p].astype(BF16), rhs[p].astype(BF16)) for p in pairs]
    uv = [jnp.concatenate([u[p].astype(BF16), vbd[p]], axis=0) for p in pairs]
    y_bd = [ah[p][P:] + _dot(m_r[p], uv[p]) for p in pairs]
    y = jnp.concatenate([z[:L] + z[L:] for z in y_bd], axis=1)
    for p in pairs:
        s_inc = lax.dot_general(uv[p], bko4[p], (((0,), (0,)), ((), ())),
                                preferred_element_type=F32)
        state_ref[p] = s0[p] * s_decay[:, cols[p]] + s_inc

    mean = per_head(y, lambda z: jnp.mean(z, axis=-1, keepdims=True))
    yc = y - mean
    var = per_head(yc * yc, lambda z: jnp.mean(z, axis=-1, keepdims=True))
    yn = yc * lax.rsqrt(var + C_GN_EPS) * lnw_ref[...] + lnb_ref[...]
    bonus = per_head(r * k * rk_ref[...], lambda z: jnp.sum(z, axis=-1, keepdims=True)) * v
    o_ref[0] = ((yn + bonus) * g_ref[0]).astype(o_ref.dtype)


def _wkv(r, k, v, lw, a, g, k_k, k_a, r_k, ln_w, ln_b, batch):
    m, width = r.shape
    t = m // batch
    hb = WKV_HEADS * C_HEAD_DIM
    assert t % WKV_L == 0 and width % hb == 0
    seq = lambda z: z.reshape(batch, t, width)
    par = lambda z: z.reshape(1, width).astype(F32)
    tok = pl.BlockSpec((1, WKV_L, hb), lambda bi, hg, c: (bi, c, hg))
    vec = pl.BlockSpec((1, hb), lambda bi, hg, c: (0, hg))
    out = pl.pallas_call(
        _wkv_kernel,
        grid=(batch, width // hb, t // WKV_L),
        in_specs=[tok] * 6 + [vec] * 5,
        out_specs=tok,
        out_shape=jax.ShapeDtypeStruct((batch, t, width), BF16),
        scratch_shapes=[pltpu.VMEM((WKV_HEADS // 2, 2 * C_HEAD_DIM, 2 * C_HEAD_DIM), F32)],
        compiler_params=_cparams("parallel", "parallel", "arbitrary"),
        name="wkv7_chunked",
    )(seq(r), seq(k), seq(v), seq(lw), seq(a), seq(g), par(k_k), par(k_a), par(r_k),
      par(ln_w), par(ln_b))
    return out.reshape(m, width)


def _rope_tables(t):
    half = ROPE_DIM // 2
    freqs = ROPE_THETA ** (-jnp.arange(half, dtype=F32) / half)
    ang = jnp.arange(t, dtype=F32)[:, None] * freqs[None, :]
    cos, sin = jnp.cos(ang), jnp.sin(ang)
    zeros = jnp.zeros((t, LANES - ROPE_DIM), F32)
    return (jnp.concatenate([cos, cos, zeros], axis=-1),
            jnp.concatenate([-sin, sin, zeros], axis=-1))


def _rope128(x, cos, sin):
    half = ROPE_DIM // 2
    swapped = pltpu.roll(x, LANES - half, axis=1) + pltpu.roll(x, half, axis=1)
    return x * cos + swapped * sin


def _mla_prep_kernel(cq_ref, ckv_ref, kpe_ref, qn_ref, kvn_ref, cos_ref, sin_ref,
                     cq_o, ckv_o, kpe_o):
    cq = cq_ref[...]
    cq_o[...] = (cq * lax.rsqrt(jnp.mean(cq * cq, axis=-1, keepdims=True) + EPS)
                 * qn_ref[...]).astype(cq_o.dtype)
    ckv = ckv_ref[...]
    ckv_o[...] = (ckv * lax.rsqrt(jnp.mean(ckv * ckv, axis=-1, keepdims=True) + EPS)
                  * kvn_ref[...]).astype(ckv_o.dtype)
    lane = lax.broadcasted_iota(jnp.int32, kpe_ref.shape, 1)
    kpe = jnp.where(lane < ROPE_DIM, kpe_ref[...], 0.0)
    kpe_o[...] = _rope128(kpe, cos_ref[...], sin_ref[...]).astype(kpe_o.dtype)


def _mla_prep(u_lo, q_norm, kv_norm, cos, sin, q_lora, kv_lora, col0, rows_per_batch):
    m = u_lo.shape[0]
    tb = _pick(rows_per_batch, (256, 128, 64, 32, 16, 8))
    per = rows_per_batch // tb
    row = lambda i: (i, 0)
    fixed = lambda i: (0, 0)
    return pl.pallas_call(
        _mla_prep_kernel,
        grid=(m // tb,),
        in_specs=[pl.BlockSpec((tb, q_lora), row),
                  pl.BlockSpec((tb, kv_lora), row),
                  pl.BlockSpec((tb, LANES), row),
                  pl.BlockSpec((1, q_lora), fixed),
                  pl.BlockSpec((1, kv_lora), fixed),
                  pl.BlockSpec((tb, LANES), lambda i: (i % per, 0)),
                  pl.BlockSpec((tb, LANES), lambda i: (i % per, 0))],
        out_specs=[pl.BlockSpec((tb, q_lora), row),
                   pl.BlockSpec((tb, kv_lora), row),
                   pl.BlockSpec((tb, LANES), row)],
        out_shape=[jax.ShapeDtypeStruct((m, q_lora), BF16),
                   jax.ShapeDtypeStruct((m, kv_lora), BF16),
                   jax.ShapeDtypeStruct((m, LANES), BF16)],
        compiler_params=_cparams("parallel"),
        name="mla_prep",
    )(u_lo[:, col0:col0 + q_lora], u_lo[:, col0 + q_lora:col0 + q_lora + kv_lora],
      u_lo[:, col0 + q_lora + kv_lora:col0 + q_lora + kv_lora + LANES],
      q_norm.reshape(1, q_lora), kv_norm.reshape(1, kv_lora), cos, sin)


MLA_TQ = 1024
MLA_TK = 512


def _mla_kernel(q_ref, kv_ref, kpe_ref, cos_ref, sin_ref, o_ref, qc_ref, m_ref, l_ref, acc_ref):
    i = pl.program_id(2)
    tq, tk = MLA_TQ, MLA_TK
    n_full = (i * tq) // tk
    qc_ref[:, :NOPE_DIM] = q_ref[0, :, :NOPE_DIM]
    qc_ref[:, NOPE_DIM:] = _rope128(q_ref[0, :, NOPE_DIM:].astype(F32), cos_ref[...],
                                    sin_ref[...]).astype(BF16)
    m_ref[...] = jnp.full(m_ref.shape, NEG_INF, F32)
    l_ref[...] = jnp.zeros(l_ref.shape, F32)
    acc_ref[...] = jnp.zeros(acc_ref.shape, F32)

    def tile(kt, first_row):
        off = pl.multiple_of(kt * tk, tk)
        k_cat = jnp.concatenate([kv_ref[0, pl.ds(off, tk), :NOPE_DIM],
                                 kpe_ref[0, pl.ds(off, tk), :]], axis=1)
        v_t = kv_ref[0, pl.ds(off, tk), NOPE_DIM:]
        starts = list(range(first_row or 0, tq, FLASH_ROWS))
        mask = None
        if first_row is not None:
            row = lax.broadcasted_iota(jnp.int32, (FLASH_ROWS, 1), 0)
            kpos = off + lax.broadcasted_iota(jnp.int32, (FLASH_ROWS, tk), 1)
            mask = [kpos < ((i * tq + r + row) // CHUNK + 1) * CHUNK for r in starts]
        _flash_update(qc_ref, starts, k_cat, v_t, mask, m_ref, l_ref, acc_ref)

    def full_tile(kt, carry):
        tile(kt, None)
        return carry

    lax.fori_loop(0, n_full, full_tile, 0)
    for d in range(max(tq // tk, 1)):
        tile(n_full + d, d * tk)
    o_ref[0] = (acc_ref[...] / l_ref[...]).astype(o_ref.dtype)


def _mla(q, kv, kpe, cos, sin, heads):
    b, t, _ = q.shape
    assert t % MLA_TQ == 0 and t % MLA_TK == 0
    assert MLA_TK % MLA_TQ == 0 or MLA_TQ % MLA_TK == 0
    hw = NOPE_DIM + LANES
    return pl.pallas_call(
        _mla_kernel,
        grid=(b, heads, t // MLA_TQ),
        in_specs=[pl.BlockSpec((1, MLA_TQ, hw), lambda bi, h, i: (bi, i, h)),
                  pl.BlockSpec((1, t, NOPE_DIM + V_DIM), lambda bi, h, i: (bi, 0, h)),
                  pl.BlockSpec((1, t, LANES), lambda bi, h, i: (bi, 0, 0)),
                  pl.BlockSpec((MLA_TQ, LANES), lambda bi, h, i: (i, 0)),
                  pl.BlockSpec((MLA_TQ, LANES), lambda bi, h, i: (i, 0))],
        out_specs=pl.BlockSpec((1, MLA_TQ, V_DIM), lambda bi, h, i: (bi, i, h)),
        out_shape=jax.ShapeDtypeStruct((b, t, heads * V_DIM), BF16),
        scratch_shapes=[pltpu.VMEM((MLA_TQ, hw), BF16),
                        pltpu.VMEM((MLA_TQ, LANES), F32),
                        pltpu.VMEM((MLA_TQ, LANES), F32),
                        pltpu.VMEM((MLA_TQ, V_DIM), F32)],
        compiler_params=_cparams("parallel", "parallel", "arbitrary"),
        name="mla_attention",
    )(q, kv, kpe, cos, sin)


def _pad_cols(w, n):
    return jnp.pad(w, ((0, 0), (0, n - w.shape[1])))


def _even_mixer(h, x, gate, w_in, w_out, e, rel_bias, batch):
    m, d = h.shape
    t = m // batch
    heads = d // (2 * HEAD_DIM)
    idx_heads = d // 128
    hw = heads * HEAD_DIM
    o_kb = 4 * hw
    o_qi = o_kb + 2 * HEAD_DIM
    o_ki = o_qi + idx_heads * IDX_DIM
    q_factor = LOG2_E * HEAD_DIM ** -0.5
    w_a = jnp.concatenate([w_in[:, :hw] * q_factor, w_in[:, hw:3 * hw]], axis=1)
    u_a = _matmul(h, w_a.astype(BF16), out_dtype=BF16)
    w_qb = w_in[:, 3 * hw:4 * hw] * q_factor
    u_q = _matmul(h, jnp.concatenate([w_qb, w_in[:, o_qi:o_ki]], axis=1).astype(BF16),
                  out_dtype=BF16)
    small = jnp.concatenate([w_in[:, o_kb:o_qi], w_in[:, o_ki:]], axis=1)
    u_s = _matmul(h, _pad_cols(small, 3 * LANES).astype(BF16), out_dtype=F32, tn=3 * LANES)
    u_s = u_s.reshape(batch, t, 3 * LANES)
    kb = u_s[..., :HEAD_DIM].astype(BF16)
    vb = u_s[..., HEAD_DIM:2 * HEAD_DIM].astype(BF16)
    ki = u_s[..., 2 * HEAD_DIM:2 * HEAD_DIM + IDX_DIM].astype(BF16)
    wi = u_s[..., 2 * HEAD_DIM + IDX_DIM:2 * HEAD_DIM + IDX_DIM + idx_heads]
    ya = _attn_a(u_a.reshape(batch, t, 3 * hw), _band_bias_tiles(rel_bias), heads,
                 0, heads, 2 * heads)
    yb = _dsa(u_q.reshape(batch, t, 2 * hw), wi, kb, vb, ki, heads, idx_heads)
    return _matmul((ya.reshape(m, hw), yb.reshape(m, hw)), w_out, w_layer=e, out_dtype=F32,
                   epilogue="resid", x=x, gate=gate, rows_per_batch=t)


def _odd_mixer(h, x, gate, w_in, w_out, o, c_mu, c_w0, c_w_up, c_a0, c_a_up, c_g_up, c_k_k, c_k_a,
               c_r_k, c_ln_w, c_ln_b, d_q_norm, d_kv_norm, d_w_uq, d_w_ukv, batch):
    m, d = h.shape
    t = m // batch
    width = c_w0.shape[0]
    heads = d // (2 * HEAD_DIM)
    q_lora = d_q_norm.shape[0]
    kv_lora = d_kv_norm.shape[0]
    lo_w = DECAY_LORA + ICLR_LORA + GATE_LORA
    w_in = w_in.astype(BF16)
    u_rkv = _matmul(h, w_in[:, :3 * width], out_dtype=BF16)
    rest = w_in[:, 3 * width:]
    rest_cols = -(-(rest.shape[1] + LANES - ROPE_DIM) // 512) * 512
    u_lo = _matmul(h, _pad_cols(rest, rest_cols), out_dtype=F32)

    r, k, v, lw, a, g = _rwkv_prep(u_rkv, u_lo, c_mu, c_w0, c_w_up, c_a0, c_a_up, c_g_up, t)
    yc = _wkv(r, k, v, lw, a, g, c_k_k, c_k_a, c_r_k, c_ln_w, c_ln_b, batch)

    cos, sin = _rope_tables(t)
    cq, ckv, kpe = _mla_prep(u_lo, d_q_norm, d_kv_norm, cos, sin, q_lora, kv_lora, lo_w, t)
    w_uq = d_w_uq.reshape(q_lora, heads, NOPE_DIM + ROPE_DIM)
    w_uq = jnp.pad(w_uq, ((0, 0), (0, 0), (0, LANES - ROPE_DIM))).reshape(q_lora, -1)
    w_uq = w_uq * (LOG2_E * (NOPE_DIM + ROPE_DIM) ** -0.5)
    q = _matmul(cq, w_uq.astype(BF16), out_dtype=BF16)
    kv = _matmul(ckv, d_w_ukv, w_layer=o, out_dtype=BF16)
    yd = _mla(q.reshape(batch, t, -1), kv.reshape(batch, t, -1), kpe.reshape(batch, t, LANES),
              cos, sin, heads)
    return _matmul((yc, yd.reshape(m, -1)), w_out, w_layer=o, out_dtype=F32,
                   epilogue="resid", x=x, gate=gate, rows_per_batch=t)


def kernel(x, c, ada_w, ada_table, norm_mix, norm_ffn, norm_final, ffn_w1, ffn_w2, ev_w_in,
           ev_w_out, a_rel_bias, od_w_in, od_w_out, c_mu, c_w0, c_w_up, c_a0, c_a_up, c_g_up,
           c_k_k, c_k_a, c_r_k, c_ln_w, c_ln_b, d_q_norm, d_kv_norm, d_w_uq, d_w_ukv):
    batch, t, d = x.shape
    depth = ada_table.shape[0]
    m = batch * t
    silu_c = (c * _sigmoid(c)).astype(BF16)
    silu_c = jnp.pad(silu_c, ((0, 8 - batch % 8 if batch % 8 else 0), (0, 0)))
    ada = _matmul(silu_c, ada_w, out_dtype=F32)[:batch].reshape(batch, 6, d)
    x = x.reshape(m, d)
    ffn_w2 = ffn_w2.astype(BF16)
    ev_w_out = ev_w_out.astype(BF16)
    od_w_out = od_w_out.astype(BF16)
    d_w_ukv = d_w_ukv.astype(BF16)
    for layer in range(depth):
        mod = ada + ada_table[layer]
        shift_m, scale_m, gate_m, shift_f, scale_f, gate_f = (mod[:, i, :] for i in range(6))
        h = _norm_mod(x, norm_mix[layer], scale_m, shift_m, t)
        if layer % 2 == 0:
            e = layer // 2
            x = _even_mixer(h, x, gate_m, ev_w_in[e], ev_w_out, e, a_rel_bias[e], batch)
        else:
            o = layer // 2
            x = _odd_mixer(h, x, gate_m, od_w_in[o], od_w_out, o, c_mu[o], c_w0[o], c_w_up[o],
                           c_a0[o], c_a_up[o], c_g_up[o], c_k_k[o], c_k_a[o], c_r_k[o],
                           c_ln_w[o], c_ln_b[o], d_q_norm[o], d_kv_norm[o], d_w_uq[o],
                           d_w_ukv, batch)
        h = _norm_mod(x, norm_ffn[layer], scale_f, shift_f, t)
        h1 = _matmul(h, ffn_w1, w_layer=layer, out_dtype=BF16, epilogue="relu2")
        x = _matmul(h1, ffn_w2, w_layer=layer, out_dtype=F32, epilogue="resid", x=x,
                    gate=gate_f, rows_per_batch=t)
    return _norm_plain(x, norm_final, F32).reshape(batch, t, d)
```

```python
import functools

import numpy as np
import jax
import jax.numpy as jnp
from jax import lax
from jax.experimental import pallas as pl
from jax.experimental.pallas import tpu as pltpu

F32 = jnp.float32
BF16 = jnp.bfloat16

CHUNK = 64
Q_BLOCK = 128
HEAD_DIM = 128
EPS = 1e-6
NEG_INF = -1e30
A_LEFT_CHUNKS = 8
A_MAX_REL = 256
IDX_DIM = 64
TOPK_MAX = 256
C_HEAD_DIM = 64
DECAY_LORA = 128
ICLR_LORA = 128
GATE_LORA = 256
C_GN_EPS = 64e-5
NOPE_DIM = 128
ROPE_DIM = 64
V_DIM = 128
ROPE_THETA = 10000.0

V7X_VMEM_LIMIT_BYTES = 56 * 1024 * 1024
LANES = 128
INT32_MIN = -(2 ** 31)
LOG2_E = 1.4426950408889634


def _cparams(*sem):
    return pltpu.CompilerParams(dimension_semantics=sem, vmem_limit_bytes=V7X_VMEM_LIMIT_BYTES)


def _dot_nt(a, b):
    return lax.dot_general(a, b, (((1,), (1,)), ((), ())), preferred_element_type=F32)


def _dot(a, b):
    return jnp.dot(a, b, preferred_element_type=F32)


def _dot_f32(a, b):
    return jnp.dot(a, b, preferred_element_type=F32, precision=lax.Precision.HIGHEST)


FLASH_ROWS = 128


def _flash_update(q_ref, row_starts, k_t, v_t, mask, m_ref, l_ref, acc_ref):
    tk = k_t.shape[0]
    reps = tk // LANES
    rows = [slice(r, r + FLASH_ROWS) for r in row_starts]
    s = [_dot_nt(q_ref[r, :], k_t) for r in rows]
    if isinstance(mask, list):
        s = [jnp.where(mk, x, NEG_INF) for mk, x in zip(mask, s)]
    elif mask is not None:
        s = [x + mask for x in s]
    m_old = [m_ref[r, :] for r in rows]
    m_new = [jnp.maximum(mo, jnp.max(x, axis=-1, keepdims=True)) for mo, x in zip(m_old, s)]
    p = [jnp.exp2(x - jnp.concatenate([mn] * reps, axis=1)) for x, mn in zip(s, m_new)]
    alpha = [jnp.exp2(mo - mn) for mo, mn in zip(m_old, m_new)]
    pv = [_dot(x.astype(v_t.dtype), v_t) for x in p]
    for r, mn, al, x, y in zip(rows, m_new, alpha, p, pv):
        m_ref[r, :] = mn
        l_ref[r, :] = al * l_ref[r, :] + jnp.sum(x, axis=-1, keepdims=True)
        acc_ref[r, :] = al * acc_ref[r, :] + y


def _loop_by_two(n, body):
    def pair(j, carry):
        body(2 * j)
        body(2 * j + 1)
        return carry

    lax.fori_loop(0, n // 2, pair, 0)

    @pl.when(n % 2 == 1)
    def _():
        body(n - 1)


def _pick(n, prefs):
    for p in prefs:
        if n % p == 0:
            return p
    return n


MM_ACC_CHUNK = 256

def _mm_kernel(*refs, nk, n_lhs, epilogue):
    a_refs = refs[:n_lhs]
    w_ref = refs[n_lhs]
    if epilogue == "resid":
        x_ref, g_ref, o_ref = refs[n_lhs + 1:n_lhs + 4]
        scratch = refs[n_lhs + 4:]
    else:
        o_ref = refs[n_lhs + 1]
        scratch = refs[n_lhs + 2:]

    tn = o_ref.shape[1]

    def finish(acc, cols):
        if epilogue == "relu2":
            r = jnp.maximum(acc, 0.0)
            acc = r * r
        elif epilogue == "resid":
            acc = x_ref[:, cols] + g_ref[0, :, cols] * acc
        o_ref[:, cols] = acc.astype(o_ref.dtype)

    def product(cols):
        part, row = None, 0
        for a_ref in a_refs:
            kw = a_ref.shape[1]
            term = _dot(a_ref[...], w_ref[row:row + kw, cols].astype(BF16))
            part = term if part is None else part + term
            row += kw
        return part

    if nk == 1:
        finish(product(slice(0, tn)), slice(0, tn))
        return

    acc_ref = scratch[0]
    k = pl.program_id(2)
    chunks = [slice(c, c + MM_ACC_CHUNK) for c in range(0, tn, MM_ACC_CHUNK)]

    @pl.when(k == 0)
    def _():
        for cols in chunks:
            acc_ref[:, cols] = product(cols)

    @pl.when((k > 0) & (k < nk - 1))
    def _():
        for cols in chunks:
            acc_ref[:, cols] += product(cols)

    @pl.when(k == nk - 1)
    def _():
        for cols in chunks:
            finish(acc_ref[:, cols] + product(cols), cols)


def _matmul(a, w, *, out_dtype, w_layer=None, epilogue="none", x=None, gate=None,
            rows_per_batch=None, tm=None, tn=None, tk=None):
    lhs = a if isinstance(a, tuple) else (a,)
    m = lhs[0].shape[0]
    assert (w.ndim == 3) == (w_layer is not None)
    kd, n = w.shape[-2:]
    assert sum(z.shape[1] for z in lhs) == kd
    tm = tm or _pick(m, (1024, 512, 256, 128, 64, 32, 16, 8))
    tk = tk or (kd if kd <= 4096 else _pick(kd, (2048, 1024, 512)))
    tn = tn or _pick(n, (1024, 512, 384, 256, 128) if tk < kd else (512, 384, 256, 128))
    nk = kd // tk
    assert m % tm == 0 and n % tn == 0 and kd % tk == 0 and (nk == 1 or len(lhs) == 1)
    in_specs = [pl.BlockSpec((tm, z.shape[1] if nk == 1 else tk), lambda i, j, k: (i, k))
                for z in lhs]
    if w_layer is None:
        in_specs.append(pl.BlockSpec((tk, tn), lambda i, j, k: (k, j)))
    else:
        in_specs.append(pl.BlockSpec((None, tk, tn), lambda i, j, k: (w_layer, k, j)))
    args = [*lhs, w]
    if epilogue == "resid":
        assert rows_per_batch % tm == 0
        per = rows_per_batch // tm
        in_specs += [pl.BlockSpec((tm, tn), lambda i, j, k: (i, j)),
                     pl.BlockSpec((1, 1, tn), lambda i, j, k: (i // per, 0, j))]
        args += [x, gate.reshape(gate.shape[0], 1, n)]
    return pl.pallas_call(
        functools.partial(_mm_kernel, nk=nk, n_lhs=len(lhs), epilogue=epilogue),
        grid=(m // tm, n // tn, nk),
        in_specs=in_specs,
        out_specs=pl.BlockSpec((tm, tn), lambda i, j, k: (i, j)),
        out_shape=jax.ShapeDtypeStruct((m, n), out_dtype),
        scratch_shapes=[pltpu.VMEM((tm, tn), F32)] if nk > 1 else [],
        compiler_params=_cparams("parallel", "parallel", "arbitrary"),
        name="matmul_" + epilogue,
    )(*args)


def _norm_mod_kernel(x_ref, g_ref, sc_ref, sh_ref, o_ref):
    x = x_ref[...]
    ms = jnp.mean(x * x, axis=-1, keepdims=True)
    y = x * lax.rsqrt(ms + EPS) * g_ref[...]
    o_ref[...] = (y * (1.0 + sc_ref[0]) + sh_ref[0]).astype(o_ref.dtype)


def _norm_plain_kernel(x_ref, g_ref, o_ref):
    x = x_ref[...]
    ms = jnp.mean(x * x, axis=-1, keepdims=True)
    o_ref[...] = (x * lax.rsqrt(ms + EPS) * g_ref[...]).astype(o_ref.dtype)


def _norm_mod(x, g, scale, shift, rows_per_batch, out_dtype=BF16):
    m, d = x.shape
    tr = _pick(rows_per_batch, (256, 128, 64, 32, 16, 8))
    per = rows_per_batch // tr
    b = scale.shape[0]
    return pl.pallas_call(
        _norm_mod_kernel,
        grid=(m // tr,),
        in_specs=[pl.BlockSpec((tr, d), lambda i: (i, 0)),
                  pl.BlockSpec((1, d), lambda i: (0, 0)),
                  pl.BlockSpec((1, 1, d), lambda i: (i // per, 0, 0)),
                  pl.BlockSpec((1, 1, d), lambda i: (i // per, 0, 0))],
        out_specs=pl.BlockSpec((tr, d), lambda i: (i, 0)),
        out_shape=jax.ShapeDtypeStruct((m, d), out_dtype),
        compiler_params=_cparams("parallel"),
        name="norm_mod",
    )(x, g.reshape(1, d), scale.reshape(b, 1, d), shift.reshape(b, 1, d))


def _norm_plain(x, g, out_dtype):
    m, d = x.shape
    tr = _pick(m, (256, 128, 64, 32, 16, 8))
    return pl.pallas_call(
        _norm_plain_kernel,
        grid=(m // tr,),
        in_specs=[pl.BlockSpec((tr, d), lambda i: (i, 0)),
                  pl.BlockSpec((1, d), lambda i: (0, 0))],
        out_specs=pl.BlockSpec((tr, d), lambda i: (i, 0)),
        out_shape=jax.ShapeDtypeStruct((m, d), out_dtype),
        compiler_params=_cparams("parallel"),
        name="norm_plain",
    )(x, g.reshape(1, d))


A_PAD = A_LEFT_CHUNKS * CHUNK
A_WIN = A_PAD + Q_BLOCK
A_VARIANTS = A_PAD // Q_BLOCK + 1


def _band_bias_tiles(rel_bias):
    width = A_WIN + Q_BLOCK
    tiles = []
    for v in range(A_VARIANTS):
        rel = np.clip(v * Q_BLOCK + Q_BLOCK - 1 - np.arange(width), -A_MAX_REL, A_MAX_REL)
        diag = jnp.pad(rel_bias.astype(F32)[:, rel + A_MAX_REL] * LOG2_E, ((0, 0), (0, 1)))
        skew = jnp.tile(diag, (1, Q_BLOCK))[:, :Q_BLOCK * width].reshape(-1, Q_BLOCK, width)
        tiles.append(skew[:, :, Q_BLOCK - 1:Q_BLOCK - 1 + A_WIN])
    return jnp.stack(tiles, axis=1)


A_HEADS_PER_STEP = 4


def _attn_a_kernel(q_ref, k_ref, v_ref, b_ref, o_ref):
    i = pl.program_id(2)
    ks = pl.multiple_of(jnp.maximum(i * Q_BLOCK - A_PAD, 0), Q_BLOCK)
    qpos = i * Q_BLOCK + lax.broadcasted_iota(jnp.int32, (Q_BLOCK, A_WIN), 0)
    kpos = ks + lax.broadcasted_iota(jnp.int32, (Q_BLOCK, A_WIN), 1)
    chunk_start = (qpos // CHUNK) * CHUNK
    valid = (kpos >= chunk_start - A_PAD) & (kpos < chunk_start + CHUNK)
    cols = [slice(h * HEAD_DIM, (h + 1) * HEAD_DIM) for h in range(A_HEADS_PER_STEP)]
    vs = [v_ref[0, pl.ds(ks, A_WIN), c] for c in cols]
    s = [_dot_nt(q_ref[0, :, c], k_ref[0, pl.ds(ks, A_WIN), c]) + b_ref[h, 0]
         for h, c in enumerate(cols)]
    s = [jnp.where(valid, x, NEG_INF) for x in s]
    p = [jnp.exp2(x - jnp.max(x, axis=-1, keepdims=True)) for x in s]
    l = [jnp.sum(x, axis=-1, keepdims=True) for x in p]
    o = [_dot(x.astype(v.dtype), v) for x, v in zip(p, vs)]
    for c, x, y in zip(cols, o, l):
        o_ref[0, :, c] = (x / y).astype(o_ref.dtype)


def _attn_a(u, bias_tiles, heads, q_col, k_col, v_col):
    b, t, _ = u.shape
    hb = A_HEADS_PER_STEP
    assert t % Q_BLOCK == 0 and t >= A_WIN
    assert heads % hb == 0 and q_col % hb == 0 and k_col % hb == 0 and v_col % hb == 0
    nq = t // Q_BLOCK
    return pl.pallas_call(
        _attn_a_kernel,
        grid=(b, heads // hb, nq),
        in_specs=[pl.BlockSpec((1, Q_BLOCK, hb * HEAD_DIM),
                               lambda bi, h, i: (bi, i, q_col // hb + h)),
                  pl.BlockSpec((1, t, hb * HEAD_DIM), lambda bi, h, i: (bi, 0, k_col // hb + h)),
                  pl.BlockSpec((1, t, hb * HEAD_DIM), lambda bi, h, i: (bi, 0, v_col // hb + h)),
                  pl.BlockSpec((hb, 1, Q_BLOCK, A_WIN),
                               lambda bi, h, i: (h, jnp.minimum(i, A_VARIANTS - 1), 0, 0))],
        out_specs=pl.BlockSpec((1, Q_BLOCK, hb * HEAD_DIM), lambda bi, h, i: (bi, i, h)),
        out_shape=jax.ShapeDtypeStruct((b, t, heads * HEAD_DIM), BF16),
        compiler_params=_cparams("parallel", "parallel", "arbitrary"),
        name="band_attention",
    )(u, u, u, bias_tiles)


DSA_TK = 512


def _sortable_key(x):
    bits = lax.bitcast_convert_type(x, jnp.int32)
    return jnp.where(bits < 0, bits ^ jnp.int32(0x7FFFFFFF), bits)


DSA_GROUP = 8


def _dsa_kernel(qb_ref, qi_ref, wi_ref, k_ref, v_ref, ki_ref, o_ref,
                key_ref, qs_ref, m_ref, l_ref, acc_ref, *, heads, idx_heads, topk):
    i = pl.program_id(1)
    tk = DSA_TK
    n_kt = (i * Q_BLOCK + Q_BLOCK + tk - 1) // tk
    row = lax.broadcasted_iota(jnp.int32, (Q_BLOCK, 1), 0)
    limit = ((i * Q_BLOCK + row) // CHUNK + 1) * CHUNK

    def lane_fold(x):
        out = x[:, :LANES]
        for c in range(1, tk // LANES):
            out = out + x[:, c * LANES:(c + 1) * LANES]
        return out

    def score_tile(kt, carry):
        off = pl.multiple_of(kt * tk, tk)
        ki_t = ki_ref[0, pl.ds(off, tk), :]
        acc = jnp.zeros((Q_BLOCK, tk), F32)
        for h in range(idx_heads):
            logits = _dot_nt(qi_ref[0, :, h * IDX_DIM:(h + 1) * IDX_DIM], ki_t)
            acc = acc + wi_ref[0, :, h:h + 1] * jnp.maximum(logits, 0.0)
        kpos = off + lax.broadcasted_iota(jnp.int32, (Q_BLOCK, tk), 1)
        key_ref[:, pl.ds(off, tk)] = _sortable_key(jnp.where(kpos < limit, acc, NEG_INF))
        return carry

    lax.fori_loop(0, n_kt, score_tile, 0)

    def bit_step(bi, t):
        cand = t + lax.shift_left(jnp.int32(1), 31 - bi)

        def count_tile(kt, c):
            off = pl.multiple_of(kt * tk, tk)
            return c + lane_fold(jnp.where(key_ref[:, pl.ds(off, tk)] >= cand, 1.0, 0.0))

        part = lax.fori_loop(0, n_kt, count_tile, jnp.zeros((Q_BLOCK, LANES), F32))
        cnt = jnp.sum(part, axis=-1, keepdims=True)
        return jnp.where(cnt >= float(topk), cand, t)

    thr = lax.fori_loop(0, 32, bit_step, jnp.full((Q_BLOCK, 1), INT32_MIN, jnp.int32))

    def tie_tile(kt, carry):
        gt, eq = carry
        off = pl.multiple_of(kt * tk, tk)
        keys = key_ref[:, pl.ds(off, tk)]
        kpos = off + lax.broadcasted_iota(jnp.int32, (Q_BLOCK, tk), 1)
        gt = gt + lane_fold(jnp.where(keys > thr, 1.0, 0.0))
        eq = eq + lane_fold(jnp.where((keys == thr) & (kpos < limit), 1.0, 0.0))
        return gt, eq

    zeros = jnp.zeros((Q_BLOCK, LANES), F32)
    gt, eq = lax.fori_loop(0, n_kt, tie_tile, (zeros, zeros))
    need = float(topk) - jnp.sum(gt, axis=-1, keepdims=True)
    surplus = jnp.max(jnp.sum(eq, axis=-1, keepdims=True) - need)

    @pl.when(surplus > 0.0)
    def _():
        ki_ = lax.broadcasted_iota(jnp.int32, (tk, tk), 0)
        kj_ = lax.broadcasted_iota(jnp.int32, (tk, tk), 1)
        before = (ki_ < kj_).astype(BF16)

        def drop_tile(kt, seen):
            off = pl.multiple_of(kt * tk, tk)
            keys = key_ref[:, pl.ds(off, tk)]
            kpos = off + lax.broadcasted_iota(jnp.int32, (Q_BLOCK, tk), 1)
            is_eq = (keys == thr) & (kpos < limit)
            eq_f = jnp.where(is_eq, 1.0, 0.0)
            rank = seen + _dot(eq_f.astype(BF16), before)
            key_ref[:, pl.ds(off, tk)] = jnp.where(is_eq & (rank >= need), keys - 1, keys)
            return seen + jnp.sum(eq_f, axis=-1, keepdims=True)

        lax.fori_loop(0, n_kt, drop_tile, jnp.zeros((Q_BLOCK, 1), F32))

    for h in range(heads):
        qs_ref[h * Q_BLOCK:(h + 1) * Q_BLOCK, :] = qb_ref[0, :, h * HEAD_DIM:(h + 1) * HEAD_DIM]
    m_ref[...] = jnp.full(m_ref.shape, NEG_INF, F32)
    l_ref[...] = jnp.zeros(l_ref.shape, F32)
    acc_ref[...] = jnp.zeros(acc_ref.shape, F32)

    def attend_tile(kt, carry):
        off = pl.multiple_of(kt * tk, tk)
        k_t = k_ref[0, pl.ds(off, tk), :]
        v_t = v_ref[0, pl.ds(off, tk), :]
        kpos = off + lax.broadcasted_iota(jnp.int32, (Q_BLOCK, tk), 1)
        sel = (key_ref[:, pl.ds(off, tk)] >= thr) & (kpos < limit)
        sel = jnp.where(sel, 0.0, NEG_INF)
        for g in range(0, heads, DSA_GROUP):
            _flash_update(qs_ref, [h * Q_BLOCK for h in range(g, g + DSA_GROUP)], k_t, v_t, sel,
                          m_ref, l_ref, acc_ref)
        return carry

    _loop_by_two(n_kt, lambda kt: attend_tile(kt, 0))
    for h in range(heads):
        rows = slice(h * Q_BLOCK, (h + 1) * Q_BLOCK)
        o_ref[0, :, h * HEAD_DIM:(h + 1) * HEAD_DIM] = (
            acc_ref[rows, :] / l_ref[rows, :]).astype(o_ref.dtype)


def _dsa(u_q, wi, kb, vb, ki, heads, idx_heads):
    b, t, _ = u_q.shape
    assert heads * HEAD_DIM == idx_heads * IDX_DIM and heads % DSA_GROUP == 0
    qw = heads * HEAD_DIM
    topk = min(TOPK_MAX, t // 4)
    assert t % DSA_TK == 0 and topk <= DSA_TK
    nq = t // Q_BLOCK
    return pl.pallas_call(
        functools.partial(_dsa_kernel, heads=heads, idx_heads=idx_heads, topk=topk),
        grid=(b, nq),
        in_specs=[pl.BlockSpec((1, Q_BLOCK, qw), lambda bi, i: (bi, i, 0)),
                  pl.BlockSpec((1, Q_BLOCK, qw), lambda bi, i: (bi, i, 1)),
                  pl.BlockSpec((1, Q_BLOCK, idx_heads), lambda bi, i: (bi, i, 0)),
                  pl.BlockSpec((1, t, HEAD_DIM), lambda bi, i: (bi, 0, 0)),
                  pl.BlockSpec((1, t, HEAD_DIM), lambda bi, i: (bi, 0, 0)),
                  pl.BlockSpec((1, t, IDX_DIM), lambda bi, i: (bi, 0, 0))],
        out_specs=pl.BlockSpec((1, Q_BLOCK, qw), lambda bi, i: (bi, i, 0)),
        out_shape=jax.ShapeDtypeStruct((b, t, qw), BF16),
        scratch_shapes=[pltpu.VMEM((Q_BLOCK, t), jnp.int32),
                        pltpu.VMEM((heads * Q_BLOCK, HEAD_DIM), BF16),
                        pltpu.VMEM((heads * Q_BLOCK, LANES), F32),
                        pltpu.VMEM((heads * Q_BLOCK, LANES), F32),
                        pltpu.VMEM((heads * Q_BLOCK, HEAD_DIM), F32)],
        compiler_params=_cparams("parallel", "arbitrary"),
        name="dsa_attention",
    )(u_q, u_q, wi, kb, vb, ki)


def _shifted(x, prev_row, first):
    rolled = pltpu.roll(x, 1, axis=0)
    head = jnp.where(first, jnp.zeros_like(prev_row), prev_row)
    rid = lax.broadcasted_iota(jnp.int32, x.shape, 0)
    return jnp.where(rid == 0, head, rolled)


def _sigmoid(x):
    return 1.0 / (1.0 + jnp.exp(-x))


PREV_ROWS = 16


def _rwkv_prep_kernel(rkv_ref, rkv_prev_ref, lo_ref, lo_prev_ref, mu_rkv_ref, mu_lo_ref,
                      w0_ref, a0_ref, wup_ref, aup_ref, gup_ref,
                      r_ref, k_ref, v_ref, lw_ref, a_ref, g_ref, *, blocks_per_seq, width):
    first = (pl.program_id(0) % blocks_per_seq) == 0
    x = rkv_ref[...].astype(F32)
    prev = _shifted(x, rkv_prev_ref[PREV_ROWS - 1:PREV_ROWS, :].astype(F32), first)
    x = x + mu_rkv_ref[...] * (prev - x)
    r_ref[...] = x[:, :width].astype(r_ref.dtype)
    k_ref[...] = x[:, width:2 * width].astype(k_ref.dtype)
    v_ref[...] = x[:, 2 * width:].astype(v_ref.dtype)

    lo = lo_ref[...]
    lo_prev = _shifted(lo, lo_prev_ref[PREV_ROWS - 1:PREV_ROWS, :], first)
    lo = lo + mu_lo_ref[...] * (lo_prev - lo)
    xw = lo[:, :DECAY_LORA]
    xa = lo[:, DECAY_LORA:DECAY_LORA + ICLR_LORA]
    xg = lo[:, DECAY_LORA + ICLR_LORA:]
    z = -(w0_ref[...] + _dot(jnp.tanh(xw).astype(BF16), wup_ref[...]))
    softplus = jnp.maximum(z, 0.0) + jnp.log(1.0 + jnp.exp(-jnp.abs(z)))
    logw = -softplus - 0.5
    lw_ref[...] = -jnp.exp(logw)
    a_ref[...] = _sigmoid(a0_ref[...] + _dot(xa.astype(BF16), aup_ref[...]))
    g_ref[...] = _dot(_sigmoid(xg).astype(BF16), gup_ref[...]).astype(g_ref.dtype)


def _rwkv_prep(u_rkv, u_lo, mu, w0, w_up, a0, a_up, g_up, rows_per_batch):
    m, w3 = u_rkv.shape
    width = w3 // 3
    lo_w = DECAY_LORA + ICLR_LORA + GATE_LORA
    tb = 128
    assert rows_per_batch % tb == 0
    blocks_per_seq = rows_per_batch // tb
    sub = tb // PREV_ROWS
    prev_map = lambda i: (jnp.maximum(i * sub - 1, 0), 0)
    row = lambda i: (i, 0)
    fixed = lambda i: (0, 0)
    wide = jax.ShapeDtypeStruct((m, width), F32)
    narrow = jax.ShapeDtypeStruct((m, width), BF16)
    return pl.pallas_call(
        functools.partial(_rwkv_prep_kernel, blocks_per_seq=blocks_per_seq, width=width),
        grid=(m // tb,),
        in_specs=[pl.BlockSpec((tb, w3), row),
                  pl.BlockSpec((PREV_ROWS, w3), prev_map),
                  pl.BlockSpec((tb, lo_w), row),
                  pl.BlockSpec((PREV_ROWS, lo_w), prev_map),
                  pl.BlockSpec((1, w3), fixed),
                  pl.BlockSpec((1, lo_w), fixed),
                  pl.BlockSpec((1, width), fixed),
                  pl.BlockSpec((1, width), fixed),
                  pl.BlockSpec((DECAY_LORA, width), fixed),
                  pl.BlockSpec((ICLR_LORA, width), fixed),
                  pl.BlockSpec((GATE_LORA, width), fixed)],
        out_specs=[pl.BlockSpec((tb, width), row)] * 6,
        out_shape=[narrow, narrow, narrow, wide, wide, narrow],
        compiler_params=_cparams("parallel"),
        name="rwkv_prep",
    )(u_rkv, u_rkv, u_lo, u_lo, mu[:w3].reshape(1, w3), mu[w3:].reshape(1, lo_w),
      w0.reshape(1, width), a0.reshape(1, width), w_up.astype(BF16), a_up.astype(BF16),
      g_up.astype(BF16))


WKV_L = 64
WKV_HEADS = 16


def _wkv_kernel(r_ref, k_ref, v_ref, lw_ref, a_ref, g_ref, kk_ref, ka_ref, rk_ref, lnw_ref,
                lnb_ref, o_ref, state_ref):
    L = WKV_L
    n = C_HEAD_DIM

    @pl.when(pl.program_id(2) == 0)
    def _():
        state_ref[...] = jnp.zeros(state_ref.shape, F32)

    ti = lax.broadcasted_iota(jnp.int32, (L, L), 0)
    si = lax.broadcasted_iota(jnp.int32, (L, L), 1)
    tri_ones = (ti >= si).astype(F32)
    P = 2 * L
    def block_masks(width):
        tp = lax.broadcasted_iota(jnp.int32, (P, width), 0)
        sp = lax.broadcasted_iota(jnp.int32, (P, width), 1)
        sp = jnp.where(sp >= P, sp - P, sp)
        same_head = ((tp >= L) & (sp >= L)) | ((tp < L) & (sp < L))
        return same_head & (tp > sp), same_head & (tp >= sp), tp == sp

    bd_strict, _, diag = block_masks(P)
    _, bd_incl2, _ = block_masks(2 * P)
    eye = diag.astype(F32)
    head0 = lax.broadcasted_iota(jnp.int32, (L, P), 1) < n

    pairs = range(WKV_HEADS // 2)
    cols = [slice(p * P, (p + 1) * P) for p in pairs]

    def per_head(full, fn):
        out = []
        for c in cols:
            x = full[:, c]
            s0 = jnp.sum(jnp.where(head0, x, 0.0), axis=-1, keepdims=True)
            s1 = jnp.sum(jnp.where(head0, 0.0, x), axis=-1, keepdims=True)
            out.append(jnp.where(head0, fn(s0), fn(s1)))
        return jnp.concatenate(out, axis=1)

    r = r_ref[0].astype(F32)
    k = k_ref[0].astype(F32)
    v = v_ref[0].astype(F32)
    lw = lw_ref[0]
    a_lr = a_ref[0]
    kk = k * kk_ref[...]
    kk = kk * per_head(kk * kk, lambda s: lax.rsqrt(jnp.maximum(s, 1e-24)))
    k = k * (1.0 + (a_lr - 1.0) * ka_ref[...])
    b_s = kk * a_lr
    cum = _dot_f32(tri_ones, lw)
    c_end = cum[L - 1:L, :]
    c_inv = jnp.exp(-cum)
    c_out = jnp.exp(c_end - cum)
    a_t = -kk * jnp.exp(cum - lw)
    r_t = r * jnp.exp(cum)
    b_t = (b_s * c_inv).astype(BF16)
    k_t = (k * c_inv).astype(BF16)
    b_o = b_s * c_out
    k_o = k * c_out
    s_decay = jnp.exp(c_end)

    def split(x):
        return [jnp.where(head0, x, 0.0).astype(BF16), jnp.where(head0, 0.0, x).astype(BF16)]

    ar4 = [jnp.concatenate(split(a_t[:, c]) + split(r_t[:, c]), axis=0) for c in cols]
    bk4 = [jnp.concatenate([b_t[:, c]] * 2 + [k_t[:, c]] * 2, axis=0) for c in cols]
    bko4 = [jnp.concatenate(split(b_o[:, c]) + split(k_o[:, c]), axis=0) for c in cols]
    vbd = [jnp.concatenate(split(v[:, c]), axis=0) for c in cols]
    prod = [_dot_nt(ar4[p], bk4[p]) for p in pairs]
    n_ab = [jnp.where(bd_strict, z[:P, :P], 0.0) for z in prod]
    n_ak = [jnp.where(bd_strict, z[:P, P:], 0.0).astype(BF16) for z in prod]
    m_r = [jnp.where(bd_incl2, z[P:, :], 0.0).astype(BF16) for z in prod]

    inv = [eye + z for z in n_ab]
    pw = n_ab
    span = 1
    while 2 * span < L:
        pw_b = [z.astype(BF16) for z in pw]
        pw = [_dot(z, z) for z in pw_b]
        inv = [inv[p] + _dot(inv[p].astype(BF16), pw[p].astype(BF16)) for p in pairs]
        span *= 2

    s0 = [state_ref[p] for p in pairs]
    ah = [_dot_nt(ar4[p], s0[p].astype(BF16)) for p in pairs]
    rhs = [ah[p][:P] + _dot(n_ak[p], vbd[p]) for p in pairs]
    u = [_dot(inv[p].astype(BF16), rhs[p].astype(BF16)) for p in pairs]
    uv = [jnp.concatenate([u[p].astype(BF16), vbd[p]], axis=0) for p in pairs]
    y_bd = [ah[p][P:] + _dot(m_r[p], uv[p]) for p in pairs]
    y = jnp.concatenate([z[:L] + z[L:] for z in y_bd], axis=1)
    for p in pairs:
        s_inc = lax.dot_general(uv[p], bko4[p], (((0,), (0,)), ((), ())),
                                preferred_element_type=F32)
        state_ref[p] = s0[p] * s_decay[:, cols[p]] + s_inc

    mean = per_head(y, lambda s: s * (1.0 / n))
    yc = y - mean
    var = per_head(yc * yc, lambda s: s * (1.0 / n))
    yn = yc * lax.rsqrt(var + C_GN_EPS) * lnw_ref[...] + lnb_ref[...]
    bonus = per_head(r * k * rk_ref[...], lambda s: s) * v
    o_ref[0] = ((yn + bonus) * g_ref[0]).astype(o_ref.dtype)


def _wkv(r, k, v, lw, a, g, k_k, k_a, r_k, ln_w, ln_b, batch):
    m, width = r.shape
    t = m // batch
    hb = WKV_HEADS * C_HEAD_DIM
    assert t % WKV_L == 0 and width % hb == 0
    seq = lambda z: z.reshape(batch, t, width)
    par = lambda z: z.reshape(1, width).astype(F32)
    tok = pl.BlockSpec((1, WKV_L, hb), lambda bi, hg, c: (bi, c, hg))
    vec = pl.BlockSpec((1, hb), lambda bi, hg, c: (0, hg))
    out = pl.pallas_call(
        _wkv_kernel,
        grid=(batch, width // hb, t // WKV_L),
        in_specs=[tok] * 6 + [vec] * 5,
        out_specs=tok,
        out_shape=jax.ShapeDtypeStruct((batch, t, width), BF16),
        scratch_shapes=[pltpu.VMEM((WKV_HEADS // 2, 2 * C_HEAD_DIM, 2 * C_HEAD_DIM), F32)],
        compiler_params=_cparams("parallel", "parallel", "arbitrary"),
        name="wkv7_chunked",
    )(seq(r), seq(k), seq(v), seq(lw), seq(a), seq(g), par(k_k), par(k_a), par(r_k),
      par(ln_w), par(ln_b))
    return out.reshape(m, width)


def _rope_tables(t):
    half = ROPE_DIM // 2
    freqs = ROPE_THETA ** (-jnp.arange(half, dtype=F32) / half)
    ang = jnp.arange(t, dtype=F32)[:, None] * freqs[None, :]
    cos, sin = jnp.cos(ang), jnp.sin(ang)
    zeros = jnp.zeros((t, LANES - ROPE_DIM), F32)
    return (jnp.concatenate([cos, cos, zeros], axis=-1),
            jnp.concatenate([-sin, sin, zeros], axis=-1))


def _rope128(x, cos, sin):
    half = ROPE_DIM // 2
    swapped = pltpu.roll(x, LANES - half, axis=1) + pltpu.roll(x, half, axis=1)
    return x * cos + swapped * sin


def _mla_prep_kernel(cq_ref, ckv_ref, kpe_ref, qn_ref, kvn_ref, cos_ref, sin_ref,
                     cq_o, ckv_o, kpe_o):
    cq = cq_ref[...]
    cq_o[...] = (cq * lax.rsqrt(jnp.mean(cq * cq, axis=-1, keepdims=True) + EPS)
                 * qn_ref[...]).astype(cq_o.dtype)
    ckv = ckv_ref[...]
    ckv_o[...] = (ckv * lax.rsqrt(jnp.mean(ckv * ckv, axis=-1, keepdims=True) + EPS)
                  * kvn_ref[...]).astype(ckv_o.dtype)
    lane = lax.broadcasted_iota(jnp.int32, kpe_ref.shape, 1)
    kpe = jnp.where(lane < ROPE_DIM, kpe_ref[...], 0.0)
    kpe_o[...] = _rope128(kpe, cos_ref[...], sin_ref[...]).astype(kpe_o.dtype)


def _mla_prep(u_lo, q_norm, kv_norm, cos, sin, q_lora, kv_lora, col0, rows_per_batch):
    m = u_lo.shape[0]
    tb = _pick(rows_per_batch, (256, 128, 64, 32, 16, 8))
    per = rows_per_batch // tb
    row = lambda i: (i, 0)
    fixed = lambda i: (0, 0)
    return pl.pallas_call(
        _mla_prep_kernel,
        grid=(m // tb,),
        in_specs=[pl.BlockSpec((tb, q_lora), row),
                  pl.BlockSpec((tb, kv_lora), row),
                  pl.BlockSpec((tb, LANES), row),
                  pl.BlockSpec((1, q_lora), fixed),
                  pl.BlockSpec((1, kv_lora), fixed),
                  pl.BlockSpec((tb, LANES), lambda i: (i % per, 0)),
                  pl.BlockSpec((tb, LANES), lambda i: (i % per, 0))],
        out_specs=[pl.BlockSpec((tb, q_lora), row),
                   pl.BlockSpec((tb, kv_lora), row),
                   pl.BlockSpec((tb, LANES), row)],
        out_shape=[jax.ShapeDtypeStruct((m, q_lora), BF16),
                   jax.ShapeDtypeStruct((m, kv_lora), BF16),
                   jax.ShapeDtypeStruct((m, LANES), BF16)],
        compiler_params=_cparams("parallel"),
        name="mla_prep",
    )(u_lo[:, col0:col0 + q_lora], u_lo[:, col0 + q_lora:col0 + q_lora + kv_lora],
      u_lo[:, col0 + q_lora + kv_lora:col0 + q_lora + kv_lora + LANES],
      q_norm.reshape(1, q_lora), kv_norm.reshape(1, kv_lora), cos, sin)


MLA_TQ = 1024
MLA_TK = 512


def _mla_kernel(q_ref, kv_ref, kpe_ref, cos_ref, sin_ref, o_ref, qc_ref, m_ref, l_ref, acc_ref):
    i = pl.program_id(2)
    tq, tk = MLA_TQ, MLA_TK
    n_full = (i * tq) // tk
    qc_ref[:, :NOPE_DIM] = q_ref[0, :, :NOPE_DIM]
    qc_ref[:, NOPE_DIM:] = _rope128(q_ref[0, :, NOPE_DIM:].astype(F32), cos_ref[...],
                                    sin_ref[...]).astype(BF16)
    m_ref[...] = jnp.full(m_ref.shape, NEG_INF, F32)
    l_ref[...] = jnp.zeros(l_ref.shape, F32)
    acc_ref[...] = jnp.zeros(acc_ref.shape, F32)

    def tile(kt, first_row):
        off = pl.multiple_of(kt * tk, tk)
        k_cat = jnp.concatenate([kv_ref[0, pl.ds(off, tk), :NOPE_DIM],
                                 kpe_ref[0, pl.ds(off, tk), :]], axis=1)
        v_t = kv_ref[0, pl.ds(off, tk), NOPE_DIM:]
        starts = list(range(first_row or 0, tq, FLASH_ROWS))
        mask = None
        if first_row is not None:
            row = lax.broadcasted_iota(jnp.int32, (FLASH_ROWS, 1), 0)
            kpos = off + lax.broadcasted_iota(jnp.int32, (FLASH_ROWS, tk), 1)
            mask = [kpos < ((i * tq + r + row) // CHUNK + 1) * CHUNK for r in starts]
        _flash_update(qc_ref, starts, k_cat, v_t, mask, m_ref, l_ref, acc_ref)

    def full_tile(kt, carry):
        tile(kt, None)
        return carry

    _loop_by_two(n_full, lambda kt: full_tile(kt, 0))
    for d in range(max(tq // tk, 1)):
        tile(n_full + d, d * tk)
    o_ref[0] = (acc_ref[...] / l_ref[...]).astype(o_ref.dtype)


def _mla(q, kv, kpe, cos, sin, heads):
    b, t, _ = q.shape
    assert t % MLA_TQ == 0 and t % MLA_TK == 0
    assert MLA_TK % MLA_TQ == 0 or MLA_TQ % MLA_TK == 0
    hw = NOPE_DIM + LANES
    return pl.pallas_call(
        _mla_kernel,
        grid=(b, heads, t // MLA_TQ),
        in_specs=[pl.BlockSpec((1, MLA_TQ, hw), lambda bi, h, i: (bi, i, h)),
                  pl.BlockSpec((1, t, NOPE_DIM + V_DIM), lambda bi, h, i: (bi, 0, h)),
                  pl.BlockSpec((1, t, LANES), lambda bi, h, i: (bi, 0, 0)),
                  pl.BlockSpec((MLA_TQ, LANES), lambda bi, h, i: (i, 0)),
                  pl.BlockSpec((MLA_TQ, LANES), lambda bi, h, i: (i, 0))],
        out_specs=pl.BlockSpec((1, MLA_TQ, V_DIM), lambda bi, h, i: (bi, i, h)),
        out_shape=jax.ShapeDtypeStruct((b, t, heads * V_DIM), BF16),
        scratch_shapes=[pltpu.VMEM((MLA_TQ, hw), BF16),
                        pltpu.VMEM((MLA_TQ, LANES), F32),
                        pltpu.VMEM((MLA_TQ, LANES), F32),
                        pltpu.VMEM((MLA_TQ, V_DIM), F32)],
        compiler_params=_cparams("parallel", "parallel", "arbitrary"),
        name="mla_attention",
    )(q, kv, kpe, cos, sin)


def _pad_cols(w, n):
    return jnp.pad(w, ((0, 0), (0, n - w.shape[1])))


def _even_mixer(h, x, gate, w_in, w_out, e, rel_bias, batch):
    m, d = h.shape
    t = m // batch
    heads = d // (2 * HEAD_DIM)
    idx_heads = d // 128
    hw = heads * HEAD_DIM
    o_kb = 4 * hw
    o_qi = o_kb + 2 * HEAD_DIM
    o_ki = o_qi + idx_heads * IDX_DIM
    q_factor = LOG2_E * HEAD_DIM ** -0.5
    w_a = jnp.concatenate([w_in[:, :hw] * q_factor, w_in[:, hw:3 * hw]], axis=1)
    u_a = _matmul(h, w_a.astype(BF16), out_dtype=BF16)
    w_qb = w_in[:, 3 * hw:4 * hw] * q_factor
    u_q = _matmul(h, jnp.concatenate([w_qb, w_in[:, o_qi:o_ki]], axis=1).astype(BF16),
                  out_dtype=BF16)
    small = jnp.concatenate([w_in[:, o_kb:o_qi], w_in[:, o_ki:]], axis=1)
    u_s = _matmul(h, _pad_cols(small, 3 * LANES).astype(BF16), out_dtype=F32, tn=3 * LANES)
    u_s = u_s.reshape(batch, t, 3 * LANES)
    kb = u_s[..., :HEAD_DIM].astype(BF16)
    vb = u_s[..., HEAD_DIM:2 * HEAD_DIM].astype(BF16)
    ki = u_s[..., 2 * HEAD_DIM:2 * HEAD_DIM + IDX_DIM].astype(BF16)
    wi = u_s[..., 2 * HEAD_DIM + IDX_DIM:2 * HEAD_DIM + IDX_DIM + idx_heads]
    ya = _attn_a(u_a.reshape(batch, t, 3 * hw), _band_bias_tiles(rel_bias), heads,
                 0, heads, 2 * heads)
    yb = _dsa(u_q.reshape(batch, t, 2 * hw), wi, kb, vb, ki, heads, idx_heads)
    return _matmul((ya.reshape(m, hw), yb.reshape(m, hw)), w_out, w_layer=e, out_dtype=F32,
                   epilogue="resid", x=x, gate=gate, rows_per_batch=t)


def _odd_mixer(h, x, gate, w_in, w_out, o, c_mu, c_w0, c_w_up, c_a0, c_a_up, c_g_up, c_k_k, c_k_a,
               c_r_k, c_ln_w, c_ln_b, d_q_norm, d_kv_norm, d_w_uq, d_w_ukv, batch):
    m, d = h.shape
    t = m // batch
    width = c_w0.shape[0]
    heads = d // (2 * HEAD_DIM)
    q_lora = d_q_norm.shape[0]
    kv_lora = d_kv_norm.shape[0]
    lo_w = DECAY_LORA + ICLR_LORA + GATE_LORA
    w_in = w_in.astype(BF16)
    u_rkv = _matmul(h, w_in[:, :3 * width], out_dtype=BF16)
    rest = w_in[:, 3 * width:]
    rest_cols = -(-(rest.shape[1] + LANES - ROPE_DIM) // 512) * 512
    u_lo = _matmul(h, _pad_cols(rest, rest_cols), out_dtype=F32)

    r, k, v, lw, a, g = _rwkv_prep(u_rkv, u_lo, c_mu, c_w0, c_w_up, c_a0, c_a_up, c_g_up, t)
    yc = _wkv(r, k, v, lw, a, g, c_k_k, c_k_a, c_r_k, c_ln_w, c_ln_b, batch)

    cos, sin = _rope_tables(t)
    cq, ckv, kpe = _mla_prep(u_lo, d_q_norm, d_kv_norm, cos, sin, q_lora, kv_lora, lo_w, t)
    w_uq = d_w_uq.reshape(q_lora, heads, NOPE_DIM + ROPE_DIM)
    w_uq = jnp.pad(w_uq, ((0, 0), (0, 0), (0, LANES - ROPE_DIM))).reshape(q_lora, -1)
    w_uq = w_uq * (LOG2_E * (NOPE_DIM + ROPE_DIM) ** -0.5)
    q = _matmul(cq, w_uq.astype(BF16), out_dtype=BF16)
    kv = _matmul(ckv, d_w_ukv, w_layer=o, out_dtype=BF16)
    yd = _mla(q.reshape(batch, t, -1), kv.reshape(batch, t, -1), kpe.reshape(batch, t, LANES),
              cos, sin, heads)
    return _matmul((yc, yd.reshape(m, -1)), w_out, w_layer=o, out_dtype=F32,
                   epilogue="resid", x=x, gate=gate, rows_per_batch=t)


def kernel(x, c, ada_w, ada_table, norm_mix, norm_ffn, norm_final, ffn_w1, ffn_w2, ev_w_in,
           ev_w_out, a_rel_bias, od_w_in, od_w_out, c_mu, c_w0, c_w_up, c_a0, c_a_up, c_g_up,
           c_k_k, c_k_a, c_r_k, c_ln_w, c_ln_b, d_q_norm, d_kv_norm, d_w_uq, d_w_ukv):
    batch, t, d = x.shape
    depth = ada_table.shape[0]
    m = batch * t
    silu_c = (c * _sigmoid(c)).astype(BF16)
    silu_c = jnp.pad(silu_c, ((0, 8 - batch % 8 if batch % 8 else 0), (0, 0)))
    ada = _matmul(silu_c, ada_w, out_dtype=F32)[:batch].reshape(batch, 6, d)
    x = x.reshape(m, d)
    ffn_w2 = ffn_w2.astype(BF16)
    ev_w_out = ev_w_out.astype(BF16)
    od_w_out = od_w_out.astype(BF16)
    d_w_ukv = d_w_ukv.astype(BF16)
    for layer in range(depth):
        mod = ada + ada_table[layer]
        shift_m, scale_m, gate_m, shift_f, scale_f, gate_f = (mod[:, i, :] for i in range(6))
        h = _norm_mod(x, norm_mix[layer], scale_m, shift_m, t)
        if layer % 2 == 0:
            e = layer // 2
            x = _even_mixer(h, x, gate_m, ev_w_in[e], ev_w_out, e, a_rel_bias[e], batch)
        else:
            o = layer // 2
            x = _odd_mixer(h, x, gate_m, od_w_in[o], od_w_out, o, c_mu[o], c_w0[o], c_w_up[o],
                           c_a0[o], c_a_up[o], c_g_up[o], c_k_k[o], c_k_a[o], c_r_k[o],
                           c_ln_w[o], c_ln_b[o], d_q_norm[o], d_kv_norm[o], d_w_uq[o],
                           d_w_ukv, batch)
        h = _norm_mod(x, norm_ffn[layer], scale_f, shift_f, t)
        h1 = _matmul(h, ffn_w1, w_layer=layer, out_dtype=BF16, epilogue="relu2")
        x = _matmul(h1, ffn_w2, w_layer=layer, out_dtype=F32, epilogue="resid", x=x,
                    gate=gate_f, rows_per_batch=t)
    return _norm_plain(x, norm_final, F32).reshape(batch, t, d)
```

```python
import functools

import numpy as np
import jax
import jax.numpy as jnp
from jax import lax
from jax.experimental import pallas as pl
from jax.experimental.pallas import tpu as pltpu

F32 = jnp.float32
BF16 = jnp.bfloat16

CHUNK = 64
Q_BLOCK = 128
HEAD_DIM = 128
EPS = 1e-6
NEG_INF = -1e30
A_LEFT_CHUNKS = 8
A_MAX_REL = 256
IDX_DIM = 64
TOPK_MAX = 256
C_HEAD_DIM = 64
DECAY_LORA = 128
ICLR_LORA = 128
GATE_LORA = 256
C_GN_EPS = 64e-5
NOPE_DIM = 128
ROPE_DIM = 64
V_DIM = 128
ROPE_THETA = 10000.0

V7X_VMEM_LIMIT_BYTES = 56 * 1024 * 1024
LANES = 128
INT32_MIN = -(2 ** 31)
LOG2_E = 1.4426950408889634


def _cparams(*sem):
    return pltpu.CompilerParams(dimension_semantics=sem, vmem_limit_bytes=V7X_VMEM_LIMIT_BYTES)


def _dot_nt(a, b):
    return lax.dot_general(a, b, (((1,), (1,)), ((), ())), preferred_element_type=F32)


def _dot(a, b):
    return jnp.dot(a, b, preferred_element_type=F32)


def _dot_f32(a, b):
    return jnp.dot(a, b, preferred_element_type=F32, precision=lax.Precision.HIGHEST)


FLASH_ROWS = 128


def _flash_update(q_ref, row_starts, k_t, v_t, mask, m_ref, l_ref, acc_ref, sum_on_mxu=False):
    tk = k_t.shape[0]
    reps = tk // LANES
    rows = [slice(r, r + FLASH_ROWS) for r in row_starts]
    s = [_dot_nt(q_ref[r, :], k_t) for r in rows]
    if mask is not None:
        masks = mask if isinstance(mask, list) else [mask] * len(rows)
        s = [x + mk for x, mk in zip(s, masks)]
    m_old = [m_ref[r, :] for r in rows]
    m_new = [jnp.maximum(mo, jnp.max(x, axis=-1, keepdims=True)) for mo, x in zip(m_old, s)]
    p = [jnp.exp2(x - jnp.concatenate([mn] * reps, axis=1)) for x, mn in zip(s, m_new)]
    alpha = [jnp.exp2(mo - mn) for mo, mn in zip(m_old, m_new)]
    if sum_on_mxu:
        v_t = jnp.concatenate([v_t, jnp.ones((tk, LANES), v_t.dtype)], axis=1)
    pv = [_dot(x.astype(v_t.dtype), v_t) for x in p]
    dv = acc_ref.shape[1]
    for r, mn, al, x, y in zip(rows, m_new, alpha, p, pv):
        m_ref[r, :] = mn
        row_sum = y[:, dv:] if sum_on_mxu else jnp.sum(x, axis=-1, keepdims=True)
        l_ref[r, :] = al * l_ref[r, :] + row_sum
        acc_ref[r, :] = al * acc_ref[r, :] + y[:, :dv]


def _loop_by_two(n, body):
    def pair(j, carry):
        body(2 * j)
        body(2 * j + 1)
        return carry

    lax.fori_loop(0, n // 2, pair, 0)

    @pl.when(n % 2 == 1)
    def _():
        body(n - 1)


def _pick(n, prefs):
    for p in prefs:
        if n % p == 0:
            return p
    return n


MM_ACC_CHUNK = 256

def _mm_kernel(*refs, nk, n_lhs, epilogue):
    a_refs = refs[:n_lhs]
    w_ref = refs[n_lhs]
    if epilogue == "resid":
        x_ref, g_ref, o_ref = refs[n_lhs + 1:n_lhs + 4]
        scratch = refs[n_lhs + 4:]
    else:
        o_ref = refs[n_lhs + 1]
        scratch = refs[n_lhs + 2:]

    tn = o_ref.shape[1]

    def finish(acc, cols):
        if epilogue == "relu2":
            r = jnp.maximum(acc, 0.0)
            acc = r * r
        elif epilogue == "resid":
            acc = x_ref[:, cols] + g_ref[0, :, cols] * acc
        o_ref[:, cols] = acc.astype(o_ref.dtype)

    def product(cols):
        part, row = None, 0
        for a_ref in a_refs:
            kw = a_ref.shape[1]
            term = _dot(a_ref[...], w_ref[row:row + kw, cols].astype(BF16))
            part = term if part is None else part + term
            row += kw
        return part

    if nk == 1:
        finish(product(slice(0, tn)), slice(0, tn))
        return

    acc_ref = scratch[0]
    k = pl.program_id(2)
    chunks = [slice(c, c + MM_ACC_CHUNK) for c in range(0, tn, MM_ACC_CHUNK)]

    @pl.when(k == 0)
    def _():
        for cols in chunks:
            acc_ref[:, cols] = product(cols)

    @pl.when((k > 0) & (k < nk - 1))
    def _():
        for cols in chunks:
            acc_ref[:, cols] += product(cols)

    @pl.when(k == nk - 1)
    def _():
        for cols in chunks:
            finish(acc_ref[:, cols] + product(cols), cols)


def _matmul(a, w, *, out_dtype, w_layer=None, epilogue="none", x=None, gate=None,
            rows_per_batch=None, tm=None, tn=None, tk=None):
    lhs = a if isinstance(a, tuple) else (a,)
    m = lhs[0].shape[0]
    assert (w.ndim == 3) == (w_layer is not None)
    kd, n = w.shape[-2:]
    assert sum(z.shape[1] for z in lhs) == kd
    tm = tm or _pick(m, (1024, 512, 256, 128, 64, 32, 16, 8))
    tk = tk or (kd if kd <= 4096 else _pick(kd, (2048, 1024, 512)))
    tn = tn or _pick(n, (1024, 512, 384, 256, 128) if tk < kd else (512, 384, 256, 128))
    nk = kd // tk
    assert m % tm == 0 and n % tn == 0 and kd % tk == 0 and (nk == 1 or len(lhs) == 1)
    in_specs = [pl.BlockSpec((tm, z.shape[1] if nk == 1 else tk), lambda i, j, k: (i, k))
                for z in lhs]
    if w_layer is None:
        in_specs.append(pl.BlockSpec((tk, tn), lambda i, j, k: (k, j)))
    else:
        in_specs.append(pl.BlockSpec((None, tk, tn), lambda i, j, k: (w_layer, k, j)))
    args = [*lhs, w]
    if epilogue == "resid":
        assert rows_per_batch % tm == 0
        per = rows_per_batch // tm
        in_specs += [pl.BlockSpec((tm, tn), lambda i, j, k: (i, j)),
                     pl.BlockSpec((1, 1, tn), lambda i, j, k: (i // per, 0, j))]
        args += [x, gate.reshape(gate.shape[0], 1, n)]
    return pl.pallas_call(
        functools.partial(_mm_kernel, nk=nk, n_lhs=len(lhs), epilogue=epilogue),
        grid=(m // tm, n // tn, nk),
        in_specs=in_specs,
        out_specs=pl.BlockSpec((tm, tn), lambda i, j, k: (i, j)),
        out_shape=jax.ShapeDtypeStruct((m, n), out_dtype),
        scratch_shapes=[pltpu.VMEM((tm, tn), F32)] if nk > 1 else [],
        compiler_params=_cparams("parallel", "parallel", "arbitrary"),
        name="matmul_" + epilogue,
    )(*args)


def _norm_mod_kernel(x_ref, g_ref, sc_ref, sh_ref, o_ref):
    x = x_ref[...]
    ms = jnp.mean(x * x, axis=-1, keepdims=True)
    y = x * lax.rsqrt(ms + EPS) * g_ref[...]
    o_ref[...] = (y * (1.0 + sc_ref[0]) + sh_ref[0]).astype(o_ref.dtype)


def _norm_plain_kernel(x_ref, g_ref, o_ref):
    x = x_ref[...]
    ms = jnp.mean(x * x, axis=-1, keepdims=True)
    o_ref[...] = (x * lax.rsqrt(ms + EPS) * g_ref[...]).astype(o_ref.dtype)


def _norm_mod(x, g, scale, shift, rows_per_batch, out_dtype=BF16):
    m, d = x.shape
    tr = _pick(rows_per_batch, (256, 128, 64, 32, 16, 8))
    per = rows_per_batch // tr
    b = scale.shape[0]
    return pl.pallas_call(
        _norm_mod_kernel,
        grid=(m // tr,),
        in_specs=[pl.BlockSpec((tr, d), lambda i: (i, 0)),
                  pl.BlockSpec((1, d), lambda i: (0, 0)),
                  pl.BlockSpec((1, 1, d), lambda i: (i // per, 0, 0)),
                  pl.BlockSpec((1, 1, d), lambda i: (i // per, 0, 0))],
        out_specs=pl.BlockSpec((tr, d), lambda i: (i, 0)),
        out_shape=jax.ShapeDtypeStruct((m, d), out_dtype),
        compiler_params=_cparams("parallel"),
        name="norm_mod",
    )(x, g.reshape(1, d), scale.reshape(b, 1, d), shift.reshape(b, 1, d))


def _norm_plain(x, g, out_dtype):
    m, d = x.shape
    tr = _pick(m, (256, 128, 64, 32, 16, 8))
    return pl.pallas_call(
        _norm_plain_kernel,
        grid=(m // tr,),
        in_specs=[pl.BlockSpec((tr, d), lambda i: (i, 0)),
                  pl.BlockSpec((1, d), lambda i: (0, 0))],
        out_specs=pl.BlockSpec((tr, d), lambda i: (i, 0)),
        out_shape=jax.ShapeDtypeStruct((m, d), out_dtype),
        compiler_params=_cparams("parallel"),
        name="norm_plain",
    )(x, g.reshape(1, d))


A_PAD = A_LEFT_CHUNKS * CHUNK
A_WIN = A_PAD + Q_BLOCK
A_VARIANTS = A_PAD // Q_BLOCK + 1


def _band_bias_tiles(rel_bias):
    width = A_WIN + Q_BLOCK
    tiles = []
    for v in range(A_VARIANTS):
        rel = np.clip(v * Q_BLOCK + Q_BLOCK - 1 - np.arange(width), -A_MAX_REL, A_MAX_REL)
        diag = jnp.pad(rel_bias.astype(F32)[:, rel + A_MAX_REL] * LOG2_E, ((0, 0), (0, 1)))
        skew = jnp.tile(diag, (1, Q_BLOCK))[:, :Q_BLOCK * width].reshape(-1, Q_BLOCK, width)
        tiles.append(skew[:, :, Q_BLOCK - 1:Q_BLOCK - 1 + A_WIN])
    return jnp.stack(tiles, axis=1)


A_HEADS_PER_STEP = 4


def _attn_a_kernel(q_ref, k_ref, v_ref, b_ref, o_ref):
    i = pl.program_id(2)
    ks = pl.multiple_of(jnp.maximum(i * Q_BLOCK - A_PAD, 0), Q_BLOCK)
    qpos = i * Q_BLOCK + lax.broadcasted_iota(jnp.int32, (Q_BLOCK, A_WIN), 0)
    kpos = ks + lax.broadcasted_iota(jnp.int32, (Q_BLOCK, A_WIN), 1)
    chunk_start = (qpos // CHUNK) * CHUNK
    valid = (kpos >= chunk_start - A_PAD) & (kpos < chunk_start + CHUNK)
    cols = [slice(h * HEAD_DIM, (h + 1) * HEAD_DIM) for h in range(A_HEADS_PER_STEP)]
    vs = [v_ref[0, pl.ds(ks, A_WIN), c] for c in cols]
    s = [_dot_nt(q_ref[0, :, c], k_ref[0, pl.ds(ks, A_WIN), c]) + b_ref[h, 0]
         for h, c in enumerate(cols)]
    s = [jnp.where(valid, x, NEG_INF) for x in s]
    p = [jnp.exp2(x - jnp.max(x, axis=-1, keepdims=True)) for x in s]
    l = [jnp.sum(x, axis=-1, keepdims=True) for x in p]
    o = [_dot(x.astype(v.dtype), v) for x, v in zip(p, vs)]
    for c, x, y in zip(cols, o, l):
        o_ref[0, :, c] = (x / y).astype(o_ref.dtype)


def _attn_a(u, bias_tiles, heads, q_col, k_col, v_col):
    b, t, _ = u.shape
    hb = A_HEADS_PER_STEP
    assert t % Q_BLOCK == 0 and t >= A_WIN
    assert heads % hb == 0 and q_col % hb == 0 and k_col % hb == 0 and v_col % hb == 0
    nq = t // Q_BLOCK
    return pl.pallas_call(
        _attn_a_kernel,
        grid=(b, heads // hb, nq),
        in_specs=[pl.BlockSpec((1, Q_BLOCK, hb * HEAD_DIM),
                               lambda bi, h, i: (bi, i, q_col // hb + h)),
                  pl.BlockSpec((1, t, hb * HEAD_DIM), lambda bi, h, i: (bi, 0, k_col // hb + h)),
                  pl.BlockSpec((1, t, hb * HEAD_DIM), lambda bi, h, i: (bi, 0, v_col // hb + h)),
                  pl.BlockSpec((hb, 1, Q_BLOCK, A_WIN),
                               lambda bi, h, i: (h, jnp.minimum(i, A_VARIANTS - 1), 0, 0))],
        out_specs=pl.BlockSpec((1, Q_BLOCK, hb * HEAD_DIM), lambda bi, h, i: (bi, i, h)),
        out_shape=jax.ShapeDtypeStruct((b, t, heads * HEAD_DIM), BF16),
        compiler_params=_cparams("parallel", "parallel", "arbitrary"),
        name="band_attention",
    )(u, u, u, bias_tiles)


DSA_TK = 512


def _sortable_key(x):
    bits = lax.bitcast_convert_type(x, jnp.int32)
    return jnp.where(bits < 0, bits ^ jnp.int32(0x7FFFFFFF), bits)


DSA_GROUP = 8


def _dsa_kernel(qb_ref, qi_ref, wi_ref, k_ref, v_ref, ki_ref, o_ref,
                key_ref, qs_ref, m_ref, l_ref, acc_ref, *, heads, idx_heads, topk):
    i = pl.program_id(1)
    tk = DSA_TK
    n_kt = (i * Q_BLOCK + Q_BLOCK + tk - 1) // tk
    row = lax.broadcasted_iota(jnp.int32, (Q_BLOCK, 1), 0)
    limit = ((i * Q_BLOCK + row) // CHUNK + 1) * CHUNK

    def lane_fold(x):
        out = x[:, :LANES]
        for c in range(1, tk // LANES):
            out = out + x[:, c * LANES:(c + 1) * LANES]
        return out

    def score_tile(kt, carry):
        off = pl.multiple_of(kt * tk, tk)
        ki_t = ki_ref[0, pl.ds(off, tk), :]
        acc = jnp.zeros((Q_BLOCK, tk), F32)
        for h in range(idx_heads):
            logits = _dot_nt(qi_ref[0, :, h * IDX_DIM:(h + 1) * IDX_DIM], ki_t)
            acc = acc + wi_ref[0, :, h:h + 1] * jnp.maximum(logits, 0.0)
        kpos = off + lax.broadcasted_iota(jnp.int32, (Q_BLOCK, tk), 1)
        key_ref[:, pl.ds(off, tk)] = _sortable_key(jnp.where(kpos < limit, acc, NEG_INF))
        return carry

    lax.fori_loop(0, n_kt, score_tile, 0)

    def bit_step(bi, t):
        cand = t + lax.shift_left(jnp.int32(1), 31 - bi)

        def count_tile(kt, c):
            off = pl.multiple_of(kt * tk, tk)
            return c + lane_fold(jnp.where(key_ref[:, pl.ds(off, tk)] >= cand, 1.0, 0.0))

        part = lax.fori_loop(0, n_kt, count_tile, jnp.zeros((Q_BLOCK, LANES), F32))
        cnt = jnp.sum(part, axis=-1, keepdims=True)
        return jnp.where(cnt >= float(topk), cand, t)

    thr = lax.fori_loop(0, 32, bit_step, jnp.full((Q_BLOCK, 1), INT32_MIN, jnp.int32))

    def tie_tile(kt, carry):
        gt, eq = carry
        off = pl.multiple_of(kt * tk, tk)
        keys = key_ref[:, pl.ds(off, tk)]
        kpos = off + lax.broadcasted_iota(jnp.int32, (Q_BLOCK, tk), 1)
        gt = gt + lane_fold(jnp.where(keys > thr, 1.0, 0.0))
        eq = eq + lane_fold(jnp.where((keys == thr) & (kpos < limit), 1.0, 0.0))
        return gt, eq

    zeros = jnp.zeros((Q_BLOCK, LANES), F32)
    gt, eq = lax.fori_loop(0, n_kt, tie_tile, (zeros, zeros))
    need = float(topk) - jnp.sum(gt, axis=-1, keepdims=True)
    surplus = jnp.max(jnp.sum(eq, axis=-1, keepdims=True) - need)

    @pl.when(surplus > 0.0)
    def _():
        ki_ = lax.broadcasted_iota(jnp.int32, (tk, tk), 0)
        kj_ = lax.broadcasted_iota(jnp.int32, (tk, tk), 1)
        before = (ki_ < kj_).astype(BF16)

        def drop_tile(kt, seen):
            off = pl.multiple_of(kt * tk, tk)
            keys = key_ref[:, pl.ds(off, tk)]
            kpos = off + lax.broadcasted_iota(jnp.int32, (Q_BLOCK, tk), 1)
            is_eq = (keys == thr) & (kpos < limit)
            eq_f = jnp.where(is_eq, 1.0, 0.0)
            rank = seen + _dot(eq_f.astype(BF16), before)
            key_ref[:, pl.ds(off, tk)] = jnp.where(is_eq & (rank >= need), keys - 1, keys)
            return seen + jnp.sum(eq_f, axis=-1, keepdims=True)

        lax.fori_loop(0, n_kt, drop_tile, jnp.zeros((Q_BLOCK, 1), F32))

    for h in range(heads):
        qs_ref[h * Q_BLOCK:(h + 1) * Q_BLOCK, :] = qb_ref[0, :, h * HEAD_DIM:(h + 1) * HEAD_DIM]
    m_ref[...] = jnp.full(m_ref.shape, NEG_INF, F32)
    l_ref[...] = jnp.zeros(l_ref.shape, F32)
    acc_ref[...] = jnp.zeros(acc_ref.shape, F32)

    def attend_tile(kt, carry):
        off = pl.multiple_of(kt * tk, tk)
        k_t = k_ref[0, pl.ds(off, tk), :]
        v_t = v_ref[0, pl.ds(off, tk), :]
        kpos = off + lax.broadcasted_iota(jnp.int32, (Q_BLOCK, tk), 1)
        sel = (key_ref[:, pl.ds(off, tk)] >= thr) & (kpos < limit)
        sel = jnp.where(sel, 0.0, NEG_INF)
        for g in range(0, heads, DSA_GROUP):
            _flash_update(qs_ref, [h * Q_BLOCK for h in range(g, g + DSA_GROUP)], k_t, v_t, sel,
                          m_ref, l_ref, acc_ref, sum_on_mxu=True)
        return carry

    _loop_by_two(n_kt, lambda kt: attend_tile(kt, 0))
    for h in range(heads):
        rows = slice(h * Q_BLOCK, (h + 1) * Q_BLOCK)
        o_ref[0, :, h * HEAD_DIM:(h + 1) * HEAD_DIM] = (
            acc_ref[rows, :] / l_ref[rows, :]).astype(o_ref.dtype)


def _dsa(u_q, wi, kb, vb, ki, heads, idx_heads):
    b, t, _ = u_q.shape
    assert heads * HEAD_DIM == idx_heads * IDX_DIM and heads % DSA_GROUP == 0
    qw = heads * HEAD_DIM
    topk = min(TOPK_MAX, t // 4)
    assert t % DSA_TK == 0 and topk <= DSA_TK
    nq = t // Q_BLOCK
    return pl.pallas_call(
        functools.partial(_dsa_kernel, heads=heads, idx_heads=idx_heads, topk=topk),
        grid=(b, nq),
        in_specs=[pl.BlockSpec((1, Q_BLOCK, qw), lambda bi, i: (bi, i, 0)),
                  pl.BlockSpec((1, Q_BLOCK, qw), lambda bi, i: (bi, i, 1)),
                  pl.BlockSpec((1, Q_BLOCK, idx_heads), lambda bi, i: (bi, i, 0)),
                  pl.BlockSpec((1, t, HEAD_DIM), lambda bi, i: (bi, 0, 0)),
                  pl.BlockSpec((1, t, HEAD_DIM), lambda bi, i: (bi, 0, 0)),
                  pl.BlockSpec((1, t, IDX_DIM), lambda bi, i: (bi, 0, 0))],
        out_specs=pl.BlockSpec((1, Q_BLOCK, qw), lambda bi, i: (bi, i, 0)),
        out_shape=jax.ShapeDtypeStruct((b, t, qw), BF16),
        scratch_shapes=[pltpu.VMEM((Q_BLOCK, t), jnp.int32),
                        pltpu.VMEM((heads * Q_BLOCK, HEAD_DIM), BF16),
                        pltpu.VMEM((heads * Q_BLOCK, LANES), F32),
                        pltpu.VMEM((heads * Q_BLOCK, LANES), F32),
                        pltpu.VMEM((heads * Q_BLOCK, HEAD_DIM), F32)],
        compiler_params=_cparams("parallel", "arbitrary"),
        name="dsa_attention",
    )(u_q, u_q, wi, kb, vb, ki)


def _shifted(x, prev_row, first):
    rolled = pltpu.roll(x, 1, axis=0)
    head = jnp.where(first, jnp.zeros_like(prev_row), prev_row)
    rid = lax.broadcasted_iota(jnp.int32, x.shape, 0)
    return jnp.where(rid == 0, head, rolled)


def _sigmoid(x):
    return 1.0 / (1.0 + jnp.exp(-x))


PREV_ROWS = 16


def _rwkv_prep_kernel(rkv_ref, rkv_prev_ref, lo_ref, lo_prev_ref, mu_rkv_ref, mu_lo_ref,
                      w0_ref, a0_ref, wup_ref, aup_ref, gup_ref,
                      r_ref, k_ref, v_ref, lw_ref, a_ref, g_ref, *, blocks_per_seq, width):
    first = (pl.program_id(0) % blocks_per_seq) == 0
    x = rkv_ref[...].astype(F32)
    prev = _shifted(x, rkv_prev_ref[PREV_ROWS - 1:PREV_ROWS, :].astype(F32), first)
    x = x + mu_rkv_ref[...] * (prev - x)
    r_ref[...] = x[:, :width].astype(r_ref.dtype)
    k_ref[...] = x[:, width:2 * width].astype(k_ref.dtype)
    v_ref[...] = x[:, 2 * width:].astype(v_ref.dtype)

    lo = lo_ref[...]
    lo_prev = _shifted(lo, lo_prev_ref[PREV_ROWS - 1:PREV_ROWS, :], first)
    lo = lo + mu_lo_ref[...] * (lo_prev - lo)
    xw = lo[:, :DECAY_LORA]
    xa = lo[:, DECAY_LORA:DECAY_LORA + ICLR_LORA]
    xg = lo[:, DECAY_LORA + ICLR_LORA:]
    z = -(w0_ref[...] + _dot(jnp.tanh(xw).astype(BF16), wup_ref[...]))
    softplus = jnp.maximum(z, 0.0) + jnp.log(1.0 + jnp.exp(-jnp.abs(z)))
    logw = -softplus - 0.5
    lw_ref[...] = -jnp.exp(logw)
    a_ref[...] = _sigmoid(a0_ref[...] + _dot(xa.astype(BF16), aup_ref[...]))
    g_ref[...] = _dot(_sigmoid(xg).astype(BF16), gup_ref[...]).astype(g_ref.dtype)


def _rwkv_prep(u_rkv, u_lo, mu, w0, w_up, a0, a_up, g_up, rows_per_batch):
    m, w3 = u_rkv.shape
    width = w3 // 3
    lo_w = DECAY_LORA + ICLR_LORA + GATE_LORA
    tb = 128
    assert rows_per_batch % tb == 0
    blocks_per_seq = rows_per_batch // tb
    sub = tb // PREV_ROWS
    prev_map = lambda i: (jnp.maximum(i * sub - 1, 0), 0)
    row = lambda i: (i, 0)
    fixed = lambda i: (0, 0)
    wide = jax.ShapeDtypeStruct((m, width), F32)
    narrow = jax.ShapeDtypeStruct((m, width), BF16)
    return pl.pallas_call(
        functools.partial(_rwkv_prep_kernel, blocks_per_seq=blocks_per_seq, width=width),
        grid=(m // tb,),
        in_specs=[pl.BlockSpec((tb, w3), row),
                  pl.BlockSpec((PREV_ROWS, w3), prev_map),
                  pl.BlockSpec((tb, lo_w), row),
                  pl.BlockSpec((PREV_ROWS, lo_w), prev_map),
                  pl.BlockSpec((1, w3), fixed),
                  pl.BlockSpec((1, lo_w), fixed),
                  pl.BlockSpec((1, width), fixed),
                  pl.BlockSpec((1, width), fixed),
                  pl.BlockSpec((DECAY_LORA, width), fixed),
                  pl.BlockSpec((ICLR_LORA, width), fixed),
                  pl.BlockSpec((GATE_LORA, width), fixed)],
        out_specs=[pl.BlockSpec((tb, width), row)] * 6,
        out_shape=[narrow, narrow, narrow, wide, wide, narrow],
        compiler_params=_cparams("parallel"),
        name="rwkv_prep",
    )(u_rkv, u_rkv, u_lo, u_lo, mu[:w3].reshape(1, w3), mu[w3:].reshape(1, lo_w),
      w0.reshape(1, width), a0.reshape(1, width), w_up.astype(BF16), a_up.astype(BF16),
      g_up.astype(BF16))


WKV_L = 64
WKV_HEADS = 16


def _wkv_kernel(r_ref, k_ref, v_ref, lw_ref, a_ref, g_ref, kk_ref, ka_ref, rk_ref, lnw_ref,
                lnb_ref, o_ref, state_ref):
    L = WKV_L
    n = C_HEAD_DIM

    @pl.when(pl.program_id(2) == 0)
    def _():
        state_ref[...] = jnp.zeros(state_ref.shape, F32)

    ti = lax.broadcasted_iota(jnp.int32, (L, L), 0)
    si = lax.broadcasted_iota(jnp.int32, (L, L), 1)
    tri_ones = (ti >= si).astype(F32)
    P = 2 * L
    def block_masks(width):
        tp = lax.broadcasted_iota(jnp.int32, (P, width), 0)
        sp = lax.broadcasted_iota(jnp.int32, (P, width), 1)
        sp = jnp.where(sp >= P, sp - P, sp)
        same_head = ((tp >= L) & (sp >= L)) | ((tp < L) & (sp < L))
        return same_head & (tp > sp), same_head & (tp >= sp), tp == sp

    bd_strict, _, diag = block_masks(P)
    _, bd_incl2, _ = block_masks(2 * P)
    eye = diag.astype(F32)
    head0 = lax.broadcasted_iota(jnp.int32, (L, P), 1) < n

    pairs = range(WKV_HEADS // 2)
    cols = [slice(p * P, (p + 1) * P) for p in pairs]

    def per_head(full, fn):
        out = []
        for c in cols:
            x = full[:, c]
            s0 = jnp.sum(jnp.where(head0, x, 0.0), axis=-1, keepdims=True)
            s1 = jnp.sum(jnp.where(head0, 0.0, x), axis=-1, keepdims=True)
            out.append(jnp.where(head0, fn(s0), fn(s1)))
        return jnp.concatenate(out, axis=1)

    r = r_ref[0].astype(F32)
    k = k_ref[0].astype(F32)
    v = v_ref[0].astype(F32)
    lw = lw_ref[0]
    a_lr = a_ref[0]
    kk = k * kk_ref[...]
    kk = kk * per_head(kk * kk, lambda s: lax.rsqrt(jnp.maximum(s, 1e-24)))
    k = k * (1.0 + (a_lr - 1.0) * ka_ref[...])
    b_s = kk * a_lr
    cum = _dot_f32(tri_ones, lw)
    c_end = cum[L - 1:L, :]
    c_inv = jnp.exp(-cum)
    c_out = jnp.exp(c_end - cum)
    a_t = -kk * jnp.exp(cum - lw)
    r_t = r * jnp.exp(cum)
    b_t = (b_s * c_inv).astype(BF16)
    k_t = (k * c_inv).astype(BF16)
    b_o = b_s * c_out
    k_o = k * c_out
    s_decay = jnp.exp(c_end)

    def split(x):
        return [jnp.where(head0, x, 0.0).astype(BF16), jnp.where(head0, 0.0, x).astype(BF16)]

    ar4 = [jnp.concatenate(split(a_t[:, c]) + split(r_t[:, c]), axis=0) for c in cols]
    bk4 = [jnp.concatenate([b_t[:, c]] * 2 + [k_t[:, c]] * 2, axis=0) for c in cols]
    bko4 = [jnp.concatenate(split(b_o[:, c]) + split(k_o[:, c]), axis=0) for c in cols]
    vbd = [jnp.concatenate(split(v[:, c]), axis=0) for c in cols]
    prod = [_dot_nt(ar4[p], bk4[p]) for p in pairs]
    n_ab = [jnp.where(bd_strict, z[:P, :P], 0.0) for z in prod]
    n_ak = [jnp.where(bd_strict, z[:P, P:], 0.0).astype(BF16) for z in prod]
    m_r = [jnp.where(bd_incl2, z[P:, :], 0.0).astype(BF16) for z in prod]

    inv = [eye + z for z in n_ab]
    pw = n_ab
    span = 1
    while 2 * span < L:
        pw_b = [z.astype(BF16) for z in pw]
        pw = [_dot(z, z) for z in pw_b]
        inv = [inv[p] + _dot(inv[p].astype(BF16), pw[p].astype(BF16)) for p in pairs]
        span *= 2

    s0 = [state_ref[p] for p in pairs]
    ah = [_dot_nt(ar4[p], s0[p].astype(BF16)) for p in pairs]
    rhs = [ah[p][:P] + _dot(n_ak[p], vbd[p]) for p in pairs]
    u = [_dot(inv[p].astype(BF16), rhs[p].astype(BF16)) for p in pairs]
    uv = [jnp.concatenate([u[p].astype(BF16), vbd[p]], axis=0) for p in pairs]
    y_bd = [ah[p][P:] + _dot(m_r[p], uv[p]) for p in pairs]
    y = jnp.concatenate([z[:L] + z[L:] for z in y_bd], axis=1)
    for p in pairs:
        s_inc = lax.dot_general(uv[p], bko4[p], (((0,), (0,)), ((), ())),
                                preferred_element_type=F32)
        state_ref[p] = s0[p] * s_decay[:, cols[p]] + s_inc

    mean = per_head(y, lambda s: s * (1.0 / n))
    yc = y - mean
    var = per_head(yc * yc, lambda s: s * (1.0 / n))
    yn = yc * lax.rsqrt(var + C_GN_EPS) * lnw_ref[...] + lnb_ref[...]
    bonus = per_head(r * k * rk_ref[...], lambda s: s) * v
    o_ref[0] = ((yn + bonus) * g_ref[0]).astype(o_ref.dtype)


def _wkv(r, k, v, lw, a, g, k_k, k_a, r_k, ln_w, ln_b, batch):
    m, width = r.shape
    t = m // batch
    hb = WKV_HEADS * C_HEAD_DIM
    assert t % WKV_L == 0 and width % hb == 0
    seq = lambda z: z.reshape(batch, t, width)
    par = lambda z: z.reshape(1, width).astype(F32)
    tok = pl.BlockSpec((1, WKV_L, hb), lambda bi, hg, c: (bi, c, hg))
    vec = pl.BlockSpec((1, hb), lambda bi, hg, c: (0, hg))
    out = pl.pallas_call(
        _wkv_kernel,
        grid=(batch, width // hb, t // WKV_L),
        in_specs=[tok] * 6 + [vec] * 5,
        out_specs=tok,
        out_shape=jax.ShapeDtypeStruct((batch, t, width), BF16),
        scratch_shapes=[pltpu.VMEM((WKV_HEADS // 2, 2 * C_HEAD_DIM, 2 * C_HEAD_DIM), F32)],
        compiler_params=_cparams("parallel", "parallel", "arbitrary"),
        name="wkv7_chunked",
    )(seq(r), seq(k), seq(v), seq(lw), seq(a), seq(g), par(k_k), par(k_a), par(r_k),
      par(ln_w), par(ln_b))
    return out.reshape(m, width)


def _rope_tables(t):
    half = ROPE_DIM // 2
    freqs = ROPE_THETA ** (-jnp.arange(half, dtype=F32) / half)
    ang = jnp.arange(t, dtype=F32)[:, None] * freqs[None, :]
    cos, sin = jnp.cos(ang), jnp.sin(ang)
    zeros = jnp.zeros((t, LANES - ROPE_DIM), F32)
    return (jnp.concatenate([cos, cos, zeros], axis=-1),
            jnp.concatenate([-sin, sin, zeros], axis=-1))


def _rope128(x, cos, sin):
    half = ROPE_DIM // 2
    swapped = pltpu.roll(x, LANES - half, axis=1) + pltpu.roll(x, half, axis=1)
    return x * cos + swapped * sin


def _mla_prep_kernel(cq_ref, ckv_ref, kpe_ref, qn_ref, kvn_ref, cos_ref, sin_ref,
                     cq_o, ckv_o, kpe_o):
    cq = cq_ref[...]
    cq_o[...] = (cq * lax.rsqrt(jnp.mean(cq * cq, axis=-1, keepdims=True) + EPS)
                 * qn_ref[...]).astype(cq_o.dtype)
    ckv = ckv_ref[...]
    ckv_o[...] = (ckv * lax.rsqrt(jnp.mean(ckv * ckv, axis=-1, keepdims=True) + EPS)
                  * kvn_ref[...]).astype(ckv_o.dtype)
    lane = lax.broadcasted_iota(jnp.int32, kpe_ref.shape, 1)
    kpe = jnp.where(lane < ROPE_DIM, kpe_ref[...], 0.0)
    kpe_o[...] = _rope128(kpe, cos_ref[...], sin_ref[...]).astype(kpe_o.dtype)


def _mla_prep(u_lo, q_norm, kv_norm, cos, sin, q_lora, kv_lora, col0, rows_per_batch):
    m = u_lo.shape[0]
    tb = _pick(rows_per_batch, (256, 128, 64, 32, 16, 8))
    per = rows_per_batch // tb
    row = lambda i: (i, 0)
    fixed = lambda i: (0, 0)
    return pl.pallas_call(
        _mla_prep_kernel,
        grid=(m // tb,),
        in_specs=[pl.BlockSpec((tb, q_lora), row),
                  pl.BlockSpec((tb, kv_lora), row),
                  pl.BlockSpec((tb, LANES), row),
                  pl.BlockSpec((1, q_lora), fixed),
                  pl.BlockSpec((1, kv_lora), fixed),
                  pl.BlockSpec((tb, LANES), lambda i: (i % per, 0)),
                  pl.BlockSpec((tb, LANES), lambda i: (i % per, 0))],
        out_specs=[pl.BlockSpec((tb, q_lora), row),
                   pl.BlockSpec((tb, kv_lora), row),
                   pl.BlockSpec((tb, LANES), row)],
        out_shape=[jax.ShapeDtypeStruct((m, q_lora), BF16),
                   jax.ShapeDtypeStruct((m, kv_lora), BF16),
                   jax.ShapeDtypeStruct((m, LANES), BF16)],
        compiler_params=_cparams("parallel"),
        name="mla_prep",
    )(u_lo[:, col0:col0 + q_lora], u_lo[:, col0 + q_lora:col0 + q_lora + kv_lora],
      u_lo[:, col0 + q_lora + kv_lora:col0 + q_lora + kv_lora + LANES],
      q_norm.reshape(1, q_lora), kv_norm.reshape(1, kv_lora), cos, sin)


MLA_TQ = 1024
MLA_TK = 512


def _chunk_causal_bias(tq, tk):
    r = np.arange(tq)[None, :, None]
    c = np.arange(tk)[None, None, :]
    d = np.arange(tq // tk)[:, None, None]
    return np.where(d * tk + c < (r // CHUNK + 1) * CHUNK, 0.0, NEG_INF).astype(np.float32)


def _mla_kernel(q_ref, kv_ref, kpe_ref, cos_ref, sin_ref, bias_ref, o_ref, qc_ref, m_ref, l_ref,
                acc_ref):
    i = pl.program_id(2)
    tq, tk = MLA_TQ, MLA_TK
    n_full = (i * tq) // tk
    qc_ref[:, :NOPE_DIM] = q_ref[0, :, :NOPE_DIM]
    qc_ref[:, NOPE_DIM:] = _rope128(q_ref[0, :, NOPE_DIM:].astype(F32), cos_ref[...],
                                    sin_ref[...]).astype(BF16)
    m_ref[...] = jnp.full(m_ref.shape, NEG_INF, F32)
    l_ref[...] = jnp.zeros(l_ref.shape, F32)
    acc_ref[...] = jnp.zeros(acc_ref.shape, F32)

    def tile(kt, diag):
        off = pl.multiple_of(kt * tk, tk)
        k_cat = jnp.concatenate([kv_ref[0, pl.ds(off, tk), :NOPE_DIM],
                                 kpe_ref[0, pl.ds(off, tk), :]], axis=1)
        v_t = kv_ref[0, pl.ds(off, tk), NOPE_DIM:]
        mask = None
        starts = list(range(0, tq, FLASH_ROWS))
        if diag is not None:
            starts = [r for r in starts if r >= diag * tk]
            mask = [bias_ref[diag, r:r + FLASH_ROWS, :] for r in starts]
        _flash_update(qc_ref, starts, k_cat, v_t, mask, m_ref, l_ref, acc_ref)

    _loop_by_two(n_full, lambda kt: tile(kt, None))
    for d in range(tq // tk):
        tile(n_full + d, d)
    o_ref[0] = (acc_ref[...] / l_ref[...]).astype(o_ref.dtype)


def _mla(q, kv, kpe, cos, sin, heads):
    b, t, _ = q.shape
    assert t % MLA_TQ == 0 and MLA_TQ % MLA_TK == 0
    hw = NOPE_DIM + LANES
    bias = _chunk_causal_bias(MLA_TQ, MLA_TK)
    return pl.pallas_call(
        _mla_kernel,
        grid=(b, heads, t // MLA_TQ),
        in_specs=[pl.BlockSpec((1, MLA_TQ, hw), lambda bi, h, i: (bi, i, h)),
                  pl.BlockSpec((1, t, NOPE_DIM + V_DIM), lambda bi, h, i: (bi, 0, h)),
                  pl.BlockSpec((1, t, LANES), lambda bi, h, i: (bi, 0, 0)),
                  pl.BlockSpec((MLA_TQ, LANES), lambda bi, h, i: (i, 0)),
                  pl.BlockSpec((MLA_TQ, LANES), lambda bi, h, i: (i, 0)),
                  pl.BlockSpec(bias.shape, lambda bi, h, i: (0, 0, 0))],
        out_specs=pl.BlockSpec((1, MLA_TQ, V_DIM), lambda bi, h, i: (bi, i, h)),
        out_shape=jax.ShapeDtypeStruct((b, t, heads * V_DIM), BF16),
        scratch_shapes=[pltpu.VMEM((MLA_TQ, hw), BF16),
                        pltpu.VMEM((MLA_TQ, LANES), F32),
                        pltpu.VMEM((MLA_TQ, LANES), F32),
                        pltpu.VMEM((MLA_TQ, V_DIM), F32)],
        compiler_params=_cparams("parallel", "parallel", "arbitrary"),
        name="mla_attention",
    )(q, kv, kpe, cos, sin, jnp.asarray(bias))


def _pad_cols(w, n):
    return jnp.pad(w, ((0, 0), (0, n - w.shape[1])))


def _even_mixer(h, x, gate, w_in, w_out, e, rel_bias, batch):
    m, d = h.shape
    t = m // batch
    heads = d // (2 * HEAD_DIM)
    idx_heads = d // 128
    hw = heads * HEAD_DIM
    o_kb = 4 * hw
    o_qi = o_kb + 2 * HEAD_DIM
    o_ki = o_qi + idx_heads * IDX_DIM
    q_factor = LOG2_E * HEAD_DIM ** -0.5
    w_a = jnp.concatenate([w_in[:, :hw] * q_factor, w_in[:, hw:3 * hw]], axis=1)
    u_a = _matmul(h, w_a.astype(BF16), out_dtype=BF16)
    w_qb = w_in[:, 3 * hw:4 * hw] * q_factor
    u_q = _matmul(h, jnp.concatenate([w_qb, w_in[:, o_qi:o_ki]], axis=1).astype(BF16),
                  out_dtype=BF16)
    small = jnp.concatenate([w_in[:, o_kb:o_qi], w_in[:, o_ki:]], axis=1)
    u_s = _matmul(h, _pad_cols(small, 3 * LANES).astype(BF16), out_dtype=F32, tn=3 * LANES)
    u_s = u_s.reshape(batch, t, 3 * LANES)
    kb = u_s[..., :HEAD_DIM].astype(BF16)
    vb = u_s[..., HEAD_DIM:2 * HEAD_DIM].astype(BF16)
    ki = u_s[..., 2 * HEAD_DIM:2 * HEAD_DIM + IDX_DIM].astype(BF16)
    wi = u_s[..., 2 * HEAD_DIM + IDX_DIM:2 * HEAD_DIM + IDX_DIM + idx_heads]
    ya = _attn_a(u_a.reshape(batch, t, 3 * hw), _band_bias_tiles(rel_bias), heads,
                 0, heads, 2 * heads)
    yb = _dsa(u_q.reshape(batch, t, 2 * hw), wi, kb, vb, ki, heads, idx_heads)
    return _matmul((ya.reshape(m, hw), yb.reshape(m, hw)), w_out, w_layer=e, out_dtype=F32,
                   epilogue="resid", x=x, gate=gate, rows_per_batch=t)


def _odd_mixer(h, x, gate, w_in, w_out, o, c_mu, c_w0, c_w_up, c_a0, c_a_up, c_g_up, c_k_k, c_k_a,
               c_r_k, c_ln_w, c_ln_b, d_q_norm, d_kv_norm, d_w_uq, d_w_ukv, batch):
    m, d = h.shape
    t = m // batch
    width = c_w0.shape[0]
    heads = d // (2 * HEAD_DIM)
    q_lora = d_q_norm.shape[0]
    kv_lora = d_kv_norm.shape[0]
    lo_w = DECAY_LORA + ICLR_LORA + GATE_LORA
    w_in = w_in.astype(BF16)
    u_rkv = _matmul(h, w_in[:, :3 * width], out_dtype=BF16)
    rest = w_in[:, 3 * width:]
    rest_cols = -(-(rest.shape[1] + LANES - ROPE_DIM) // 512) * 512
    u_lo = _matmul(h, _pad_cols(rest, rest_cols), out_dtype=F32)

    r, k, v, lw, a, g = _rwkv_prep(u_rkv, u_lo, c_mu, c_w0, c_w_up, c_a0, c_a_up, c_g_up, t)
    yc = _wkv(r, k, v, lw, a, g, c_k_k, c_k_a, c_r_k, c_ln_w, c_ln_b, batch)

    cos, sin = _rope_tables(t)
    cq, ckv, kpe = _mla_prep(u_lo, d_q_norm, d_kv_norm, cos, sin, q_lora, kv_lora, lo_w, t)
    w_uq = d_w_uq.reshape(q_lora, heads, NOPE_DIM + ROPE_DIM)
    w_uq = jnp.pad(w_uq, ((0, 0), (0, 0), (0, LANES - ROPE_DIM))).reshape(q_lora, -1)
    w_uq = w_uq * (LOG2_E * (NOPE_DIM + ROPE_DIM) ** -0.5)
    q = _matmul(cq, w_uq.astype(BF16), out_dtype=BF16)
    kv = _matmul(ckv, d_w_ukv, w_layer=o, out_dtype=BF16)
    yd = _mla(q.reshape(batch, t, -1), kv.reshape(batch, t, -1), kpe.reshape(batch, t, LANES),
              cos, sin, heads)
    return _matmul((yc, yd.reshape(m, -1)), w_out, w_layer=o, out_dtype=F32,
                   epilogue="resid", x=x, gate=gate, rows_per_batch=t)


def kernel(x, c, ada_w, ada_table, norm_mix, norm_ffn, norm_final, ffn_w1, ffn_w2, ev_w_in,
           ev_w_out, a_rel_bias, od_w_in, od_w_out, c_mu, c_w0, c_w_up, c_a0, c_a_up, c_g_up,
           c_k_k, c_k_a, c_r_k, c_ln_w, c_ln_b, d_q_norm, d_kv_norm, d_w_uq, d_w_ukv):
    batch, t, d = x.shape
    depth = ada_table.shape[0]
    m = batch * t
    silu_c = (c * _sigmoid(c)).astype(BF16)
    silu_c = jnp.pad(silu_c, ((0, 8 - batch % 8 if batch % 8 else 0), (0, 0)))
    ada = _matmul(silu_c, ada_w, out_dtype=F32)[:batch].reshape(batch, 6, d)
    x = x.reshape(m, d)
    ev_w_out = ev_w_out.astype(BF16)
    od_w_out = od_w_out.astype(BF16)
    d_w_ukv = d_w_ukv.astype(BF16)
    for layer in range(depth):
        mod = ada + ada_table[layer]
        shift_m, scale_m, gate_m, shift_f, scale_f, gate_f = (mod[:, i, :] for i in range(6))
        h = _norm_mod(x, norm_mix[layer], scale_m, shift_m, t)
        if layer % 2 == 0:
            e = layer // 2
            x = _even_mixer(h, x, gate_m, ev_w_in[e], ev_w_out, e, a_rel_bias[e], batch)
        else:
            o = layer // 2
            x = _odd_mixer(h, x, gate_m, od_w_in[o], od_w_out, o, c_mu[o], c_w0[o], c_w_up[o],
                           c_a0[o], c_a_up[o], c_g_up[o], c_k_k[o], c_k_a[o], c_r_k[o],
                           c_ln_w[o], c_ln_b[o], d_q_norm[o], d_kv_norm[o], d_w_uq[o],
                           d_w_ukv, batch)
        h = _norm_mod(x, norm_ffn[layer], scale_f, shift_f, t)
        h1 = _matmul(h, ffn_w1, w_layer=layer, out_dtype=BF16, epilogue="relu2")
        x = _matmul(h1, ffn_w2, w_layer=layer, out_dtype=F32, epilogue="resid", x=x,
                    gate=gate_f, rows_per_batch=t)
    return _norm_plain(x, norm_final, F32).reshape(batch, t, d)
```

```python
import functools

import numpy as np
import jax
import jax.numpy as jnp
from jax import lax
from jax.experimental import pallas as pl
from jax.experimental.pallas import tpu as pltpu

F32 = jnp.float32
BF16 = jnp.bfloat16

CHUNK = 64
Q_BLOCK = 128
HEAD_DIM = 128
EPS = 1e-6
NEG_INF = -1e30
A_LEFT_CHUNKS = 8
A_MAX_REL = 256
IDX_DIM = 64
TOPK_MAX = 256
C_HEAD_DIM = 64
DECAY_LORA = 128
ICLR_LORA = 128
GATE_LORA = 256
C_GN_EPS = 64e-5
NOPE_DIM = 128
ROPE_DIM = 64
V_DIM = 128
ROPE_THETA = 10000.0

V7X_VMEM_LIMIT_BYTES = 56 * 1024 * 1024
LANES = 128
INT32_MIN = -(2 ** 31)
LOG2_E = 1.4426950408889634


def _cparams(*sem):
    return pltpu.CompilerParams(dimension_semantics=sem, vmem_limit_bytes=V7X_VMEM_LIMIT_BYTES)


def _dot_nt(a, b):
    return lax.dot_general(a, b, (((1,), (1,)), ((), ())), preferred_element_type=F32)


def _dot(a, b):
    return jnp.dot(a, b, preferred_element_type=F32)


def _dot_f32(a, b):
    return jnp.dot(a, b, preferred_element_type=F32, precision=lax.Precision.HIGHEST)


FLASH_ROWS = 128


def _flash_update(q_ref, row_starts, k_t, v_t, mask, m_ref, l_ref, acc_ref, sum_on_mxu=False):
    tk = k_t.shape[0]
    reps = tk // LANES
    rows = [slice(r, r + FLASH_ROWS) for r in row_starts]
    s = [_dot_nt(q_ref[r, :], k_t) for r in rows]
    if mask is not None:
        masks = mask if isinstance(mask, list) else [mask] * len(rows)
        s = [x + mk for x, mk in zip(s, masks)]
    m_old = [m_ref[r, :] for r in rows]
    m_new = [jnp.maximum(mo, jnp.max(x, axis=-1, keepdims=True)) for mo, x in zip(m_old, s)]
    p = [jnp.exp2(x - jnp.concatenate([mn] * reps, axis=1)) for x, mn in zip(s, m_new)]
    alpha = [jnp.exp2(mo - mn) for mo, mn in zip(m_old, m_new)]
    if sum_on_mxu:
        v_t = jnp.concatenate([v_t, jnp.ones((tk, LANES), v_t.dtype)], axis=1)
    pv = [_dot(x.astype(v_t.dtype), v_t) for x in p]
    dv = acc_ref.shape[1]
    for r, mn, al, x, y in zip(rows, m_new, alpha, p, pv):
        m_ref[r, :] = mn
        row_sum = y[:, dv:] if sum_on_mxu else jnp.sum(x, axis=-1, keepdims=True)
        l_ref[r, :] = al * l_ref[r, :] + row_sum
        acc_ref[r, :] = al * acc_ref[r, :] + y[:, :dv]


def _loop_by_two(n, body):
    def pair(j, carry):
        body(2 * j)
        body(2 * j + 1)
        return carry

    lax.fori_loop(0, n // 2, pair, 0)

    @pl.when(n % 2 == 1)
    def _():
        body(n - 1)


def _pick(n, prefs):
    for p in prefs:
        if n % p == 0:
            return p
    return n


MM_ACC_CHUNK = 256

def _mm_kernel(*refs, nk, n_lhs, epilogue):
    a_refs = refs[:n_lhs]
    w_ref = refs[n_lhs]
    if epilogue == "resid":
        x_ref, g_ref, o_ref = refs[n_lhs + 1:n_lhs + 4]
        scratch = refs[n_lhs + 4:]
    else:
        o_ref = refs[n_lhs + 1]
        scratch = refs[n_lhs + 2:]

    tn = o_ref.shape[1]

    def finish(acc, cols):
        if epilogue == "relu2":
            r = jnp.maximum(acc, 0.0)
            acc = r * r
        elif epilogue == "resid":
            acc = x_ref[:, cols] + g_ref[0, :, cols] * acc
        o_ref[:, cols] = acc.astype(o_ref.dtype)

    def product(cols):
        part, row = None, 0
        for a_ref in a_refs:
            kw = a_ref.shape[1]
            term = _dot(a_ref[...], w_ref[row:row + kw, cols].astype(BF16))
            part = term if part is None else part + term
            row += kw
        return part

    if nk == 1:
        finish(product(slice(0, tn)), slice(0, tn))
        return

    acc_ref = scratch[0]
    k = pl.program_id(2)
    chunks = [slice(c, c + MM_ACC_CHUNK) for c in range(0, tn, MM_ACC_CHUNK)]

    @pl.when(k == 0)
    def _():
        for cols in chunks:
            acc_ref[:, cols] = product(cols)

    @pl.when((k > 0) & (k < nk - 1))
    def _():
        for cols in chunks:
            acc_ref[:, cols] += product(cols)

    @pl.when(k == nk - 1)
    def _():
        for cols in chunks:
            finish(acc_ref[:, cols] + product(cols), cols)


def _matmul(a, w, *, out_dtype, w_layer=None, epilogue="none", x=None, gate=None,
            rows_per_batch=None, tm=None, tn=None, tk=None):
    lhs = a if isinstance(a, tuple) else (a,)
    m = lhs[0].shape[0]
    assert (w.ndim == 3) == (w_layer is not None)
    kd, n = w.shape[-2:]
    assert sum(z.shape[1] for z in lhs) == kd
    tm = tm or _pick(m, (1024, 512, 256, 128, 64, 32, 16, 8))
    tk = tk or (kd if kd <= 4096 else _pick(kd, (2048, 1024, 512)))
    tn = tn or _pick(n, (1024, 512, 384, 256, 128) if tk < kd else (512, 384, 256, 128))
    nk = kd // tk
    assert m % tm == 0 and n % tn == 0 and kd % tk == 0 and (nk == 1 or len(lhs) == 1)
    in_specs = [pl.BlockSpec((tm, z.shape[1] if nk == 1 else tk), lambda i, j, k: (i, k))
                for z in lhs]
    if w_layer is None:
        in_specs.append(pl.BlockSpec((tk, tn), lambda i, j, k: (k, j)))
    else:
        in_specs.append(pl.BlockSpec((None, tk, tn), lambda i, j, k: (w_layer, k, j)))
    args = [*lhs, w]
    if epilogue == "resid":
        assert rows_per_batch % tm == 0
        per = rows_per_batch // tm
        in_specs += [pl.BlockSpec((tm, tn), lambda i, j, k: (i, j)),
                     pl.BlockSpec((1, 1, tn), lambda i, j, k: (i // per, 0, j))]
        args += [x, gate.reshape(gate.shape[0], 1, n)]
    return pl.pallas_call(
        functools.partial(_mm_kernel, nk=nk, n_lhs=len(lhs), epilogue=epilogue),
        grid=(m // tm, n // tn, nk),
        in_specs=in_specs,
        out_specs=pl.BlockSpec((tm, tn), lambda i, j, k: (i, j)),
        out_shape=jax.ShapeDtypeStruct((m, n), out_dtype),
        scratch_shapes=[pltpu.VMEM((tm, tn), F32)] if nk > 1 else [],
        compiler_params=_cparams("parallel", "parallel", "arbitrary"),
        name="matmul_" + epilogue,
    )(*args)


def _norm_mod_kernel(x_ref, g_ref, sc_ref, sh_ref, o_ref):
    x = x_ref[...]
    ms = jnp.mean(x * x, axis=-1, keepdims=True)
    y = x * lax.rsqrt(ms + EPS) * g_ref[...]
    o_ref[...] = (y * (1.0 + sc_ref[0]) + sh_ref[0]).astype(o_ref.dtype)


def _norm_plain_kernel(x_ref, g_ref, o_ref):
    x = x_ref[...]
    ms = jnp.mean(x * x, axis=-1, keepdims=True)
    o_ref[...] = (x * lax.rsqrt(ms + EPS) * g_ref[...]).astype(o_ref.dtype)


def _norm_mod(x, g, scale, shift, rows_per_batch, out_dtype=BF16):
    m, d = x.shape
    tr = _pick(rows_per_batch, (256, 128, 64, 32, 16, 8))
    per = rows_per_batch // tr
    b = scale.shape[0]
    return pl.pallas_call(
        _norm_mod_kernel,
        grid=(m // tr,),
        in_specs=[pl.BlockSpec((tr, d), lambda i: (i, 0)),
                  pl.BlockSpec((1, d), lambda i: (0, 0)),
                  pl.BlockSpec((1, 1, d), lambda i: (i // per, 0, 0)),
                  pl.BlockSpec((1, 1, d), lambda i: (i // per, 0, 0))],
        out_specs=pl.BlockSpec((tr, d), lambda i: (i, 0)),
        out_shape=jax.ShapeDtypeStruct((m, d), out_dtype),
        compiler_params=_cparams("parallel"),
        name="norm_mod",
    )(x, g.reshape(1, d), scale.reshape(b, 1, d), shift.reshape(b, 1, d))


def _norm_plain(x, g, out_dtype):
    m, d = x.shape
    tr = _pick(m, (256, 128, 64, 32, 16, 8))
    return pl.pallas_call(
        _norm_plain_kernel,
        grid=(m // tr,),
        in_specs=[pl.BlockSpec((tr, d), lambda i: (i, 0)),
                  pl.BlockSpec((1, d), lambda i: (0, 0))],
        out_specs=pl.BlockSpec((tr, d), lambda i: (i, 0)),
        out_shape=jax.ShapeDtypeStruct((m, d), out_dtype),
        compiler_params=_cparams("parallel"),
        name="norm_plain",
    )(x, g.reshape(1, d))


A_PAD = A_LEFT_CHUNKS * CHUNK
A_WIN = A_PAD + Q_BLOCK
A_VARIANTS = A_PAD // Q_BLOCK + 1


def _band_bias_tiles(rel_bias):
    width = A_WIN + Q_BLOCK
    tiles = []
    for v in range(A_VARIANTS):
        rel = np.clip(v * Q_BLOCK + Q_BLOCK - 1 - np.arange(width), -A_MAX_REL, A_MAX_REL)
        diag = jnp.pad(rel_bias.astype(F32)[:, rel + A_MAX_REL] * LOG2_E, ((0, 0), (0, 1)))
        skew = jnp.tile(diag, (1, Q_BLOCK))[:, :Q_BLOCK * width].reshape(-1, Q_BLOCK, width)
        tiles.append(skew[:, :, Q_BLOCK - 1:Q_BLOCK - 1 + A_WIN])
    return jnp.stack(tiles, axis=1)


A_HEADS_PER_STEP = 8


def _attn_a_kernel(q_ref, k_ref, v_ref, b_ref, o_ref):
    i = pl.program_id(2)
    ks = pl.multiple_of(jnp.maximum(i * Q_BLOCK - A_PAD, 0), Q_BLOCK)
    qpos = i * Q_BLOCK + lax.broadcasted_iota(jnp.int32, (Q_BLOCK, A_WIN), 0)
    kpos = ks + lax.broadcasted_iota(jnp.int32, (Q_BLOCK, A_WIN), 1)
    chunk_start = (qpos // CHUNK) * CHUNK
    valid = (kpos >= chunk_start - A_PAD) & (kpos < chunk_start + CHUNK)
    cols = [slice(h * HEAD_DIM, (h + 1) * HEAD_DIM) for h in range(A_HEADS_PER_STEP)]
    vs = [v_ref[0, pl.ds(ks, A_WIN), c] for c in cols]
    s = [_dot_nt(q_ref[0, :, c], k_ref[0, pl.ds(ks, A_WIN), c]) + b_ref[h, 0]
         for h, c in enumerate(cols)]
    s = [jnp.where(valid, x, NEG_INF) for x in s]
    p = [jnp.exp2(x - jnp.max(x, axis=-1, keepdims=True)) for x in s]
    l = [jnp.sum(x, axis=-1, keepdims=True) for x in p]
    o = [_dot(x.astype(v.dtype), v) for x, v in zip(p, vs)]
    for c, x, y in zip(cols, o, l):
        o_ref[0, :, c] = (x / y).astype(o_ref.dtype)


def _attn_a(u, bias_tiles, heads, q_col, k_col, v_col):
    b, t, _ = u.shape
    hb = A_HEADS_PER_STEP
    assert t % Q_BLOCK == 0 and t >= A_WIN
    assert heads % hb == 0 and q_col % hb == 0 and k_col % hb == 0 and v_col % hb == 0
    nq = t // Q_BLOCK
    return pl.pallas_call(
        _attn_a_kernel,
        grid=(b, heads // hb, nq),
        in_specs=[pl.BlockSpec((1, Q_BLOCK, hb * HEAD_DIM),
                               lambda bi, h, i: (bi, i, q_col // hb + h)),
                  pl.BlockSpec((1, t, hb * HEAD_DIM), lambda bi, h, i: (bi, 0, k_col // hb + h)),
                  pl.BlockSpec((1, t, hb * HEAD_DIM), lambda bi, h, i: (bi, 0, v_col // hb + h)),
                  pl.BlockSpec((hb, 1, Q_BLOCK, A_WIN),
                               lambda bi, h, i: (h, jnp.minimum(i, A_VARIANTS - 1), 0, 0))],
        out_specs=pl.BlockSpec((1, Q_BLOCK, hb * HEAD_DIM), lambda bi, h, i: (bi, i, h)),
        out_shape=jax.ShapeDtypeStruct((b, t, heads * HEAD_DIM), BF16),
        compiler_params=_cparams("parallel", "parallel", "arbitrary"),
        name="band_attention",
    )(u, u, u, bias_tiles)


DSA_TK = 512


def _sortable_key(x):
    bits = lax.bitcast_convert_type(x, jnp.int32)
    return jnp.where(bits < 0, bits ^ jnp.int32(0x7FFFFFFF), bits)


DSA_GROUP = 8


def _dsa_kernel(qb_ref, qi_ref, wi_ref, k_ref, v_ref, ki_ref, o_ref,
                key_ref, qs_ref, m_ref, l_ref, acc_ref, *, heads, idx_heads, topk):
    i = pl.program_id(1)
    tk = DSA_TK
    n_kt = (i * Q_BLOCK + Q_BLOCK + tk - 1) // tk
    row = lax.broadcasted_iota(jnp.int32, (Q_BLOCK, 1), 0)
    limit = ((i * Q_BLOCK + row) // CHUNK + 1) * CHUNK

    def lane_fold(x):
        out = x[:, :LANES]
        for c in range(1, tk // LANES):
            out = out + x[:, c * LANES:(c + 1) * LANES]
        return out

    def score_tile(kt, carry):
        off = pl.multiple_of(kt * tk, tk)
        ki_t = ki_ref[0, pl.ds(off, tk), :]
        acc = jnp.zeros((Q_BLOCK, tk), F32)
        for h in range(idx_heads):
            logits = _dot_nt(qi_ref[0, :, h * IDX_DIM:(h + 1) * IDX_DIM], ki_t)
            acc = acc + wi_ref[0, :, h:h + 1] * jnp.maximum(logits, 0.0)
        kpos = off + lax.broadcasted_iota(jnp.int32, (Q_BLOCK, tk), 1)
        key_ref[:, pl.ds(off, tk)] = _sortable_key(jnp.where(kpos < limit, acc, NEG_INF))
        return carry

    lax.fori_loop(0, n_kt, score_tile, 0)

    def bit_step(bi, t):
        cand = t + lax.shift_left(jnp.int32(1), 31 - bi)

        def count_tile(kt, c):
            off = pl.multiple_of(kt * tk, tk)
            return c + lane_fold(jnp.where(key_ref[:, pl.ds(off, tk)] >= cand, 1.0, 0.0))

        part = lax.fori_loop(0, n_kt, count_tile, jnp.zeros((Q_BLOCK, LANES), F32))
        cnt = jnp.sum(part, axis=-1, keepdims=True)
        return jnp.where(cnt >= float(topk), cand, t)

    thr = lax.fori_loop(0, 32, bit_step, jnp.full((Q_BLOCK, 1), INT32_MIN, jnp.int32))

    def tie_tile(kt, carry):
        gt, eq = carry
        off = pl.multiple_of(kt * tk, tk)
        keys = key_ref[:, pl.ds(off, tk)]
        kpos = off + lax.broadcasted_iota(jnp.int32, (Q_BLOCK, tk), 1)
        gt = gt + lane_fold(jnp.where(keys > thr, 1.0, 0.0))
        eq = eq + lane_fold(jnp.where((keys == thr) & (kpos < limit), 1.0, 0.0))
        return gt, eq

    zeros = jnp.zeros((Q_BLOCK, LANES), F32)
    gt, eq = lax.fori_loop(0, n_kt, tie_tile, (zeros, zeros))
    need = float(topk) - jnp.sum(gt, axis=-1, keepdims=True)
    surplus = jnp.max(jnp.sum(eq, axis=-1, keepdims=True) - need)

    @pl.when(surplus > 0.0)
    def _():
        ki_ = lax.broadcasted_iota(jnp.int32, (tk, tk), 0)
        kj_ = lax.broadcasted_iota(jnp.int32, (tk, tk), 1)
        before = (ki_ < kj_).astype(BF16)

        def drop_tile(kt, seen):
            off = pl.multiple_of(kt * tk, tk)
            keys = key_ref[:, pl.ds(off, tk)]
            kpos = off + lax.broadcasted_iota(jnp.int32, (Q_BLOCK, tk), 1)
            is_eq = (keys == thr) & (kpos < limit)
            eq_f = jnp.where(is_eq, 1.0, 0.0)
            rank = seen + _dot(eq_f.astype(BF16), before)
            key_ref[:, pl.ds(off, tk)] = jnp.where(is_eq & (rank >= need), keys - 1, keys)
            return seen + jnp.sum(eq_f, axis=-1, keepdims=True)

        lax.fori_loop(0, n_kt, drop_tile, jnp.zeros((Q_BLOCK, 1), F32))

    for h in range(heads):
        qs_ref[h * Q_BLOCK:(h + 1) * Q_BLOCK, :] = qb_ref[0, :, h * HEAD_DIM:(h + 1) * HEAD_DIM]
    m_ref[...] = jnp.full(m_ref.shape, NEG_INF, F32)
    l_ref[...] = jnp.zeros(l_ref.shape, F32)
    acc_ref[...] = jnp.zeros(acc_ref.shape, F32)

    def attend_tile(kt, carry):
        off = pl.multiple_of(kt * tk, tk)
        k_t = k_ref[0, pl.ds(off, tk), :]
        v_t = v_ref[0, pl.ds(off, tk), :]
        kpos = off + lax.broadcasted_iota(jnp.int32, (Q_BLOCK, tk), 1)
        sel = (key_ref[:, pl.ds(off, tk)] >= thr) & (kpos < limit)
        sel = jnp.where(sel, 0.0, NEG_INF)
        for g in range(0, heads, DSA_GROUP):
            _flash_update(qs_ref, [h * Q_BLOCK for h in range(g, g + DSA_GROUP)], k_t, v_t, sel,
                          m_ref, l_ref, acc_ref, sum_on_mxu=True)
        return carry

    _loop_by_two(n_kt, lambda kt: attend_tile(kt, 0))
    for h in range(heads):
        rows = slice(h * Q_BLOCK, (h + 1) * Q_BLOCK)
        o_ref[0, :, h * HEAD_DIM:(h + 1) * HEAD_DIM] = (
            acc_ref[rows, :] / l_ref[rows, :]).astype(o_ref.dtype)


def _dsa(u_q, wi, kb, vb, ki, heads, idx_heads):
    b, t, _ = u_q.shape
    assert heads * HEAD_DIM == idx_heads * IDX_DIM and heads % DSA_GROUP == 0
    qw = heads * HEAD_DIM
    topk = min(TOPK_MAX, t // 4)
    assert t % DSA_TK == 0 and topk <= DSA_TK
    nq = t // Q_BLOCK
    return pl.pallas_call(
        functools.partial(_dsa_kernel, heads=heads, idx_heads=idx_heads, topk=topk),
        grid=(b, nq),
        in_specs=[pl.BlockSpec((1, Q_BLOCK, qw), lambda bi, i: (bi, i, 0)),
                  pl.BlockSpec((1, Q_BLOCK, qw), lambda bi, i: (bi, i, 1)),
                  pl.BlockSpec((1, Q_BLOCK, idx_heads), lambda bi, i: (bi, i, 0)),
                  pl.BlockSpec((1, t, HEAD_DIM), lambda bi, i: (bi, 0, 0)),
                  pl.BlockSpec((1, t, HEAD_DIM), lambda bi, i: (bi, 0, 0)),
                  pl.BlockSpec((1, t, IDX_DIM), lambda bi, i: (bi, 0, 0))],
        out_specs=pl.BlockSpec((1, Q_BLOCK, qw), lambda bi, i: (bi, i, 0)),
        out_shape=jax.ShapeDtypeStruct((b, t, qw), BF16),
        scratch_shapes=[pltpu.VMEM((Q_BLOCK, t), jnp.int32),
                        pltpu.VMEM((heads * Q_BLOCK, HEAD_DIM), BF16),
                        pltpu.VMEM((heads * Q_BLOCK, LANES), F32),
                        pltpu.VMEM((heads * Q_BLOCK, LANES), F32),
                        pltpu.VMEM((heads * Q_BLOCK, HEAD_DIM), F32)],
        compiler_params=_cparams("parallel", "arbitrary"),
        name="dsa_attention",
    )(u_q, u_q, wi, kb, vb, ki)


def _shifted(x, prev_row, first):
    rolled = pltpu.roll(x, 1, axis=0)
    head = jnp.where(first, jnp.zeros_like(prev_row), prev_row)
    rid = lax.broadcasted_iota(jnp.int32, x.shape, 0)
    return jnp.where(rid == 0, head, rolled)


def _sigmoid(x):
    return 1.0 / (1.0 + jnp.exp(-x))


PREV_ROWS = 16


def _rwkv_prep_kernel(rkv_ref, rkv_prev_ref, lo_ref, lo_prev_ref, mu_rkv_ref, mu_lo_ref,
                      w0_ref, a0_ref, wup_ref, aup_ref, gup_ref,
                      r_ref, k_ref, v_ref, lw_ref, a_ref, g_ref, *, blocks_per_seq, width):
    first = (pl.program_id(0) % blocks_per_seq) == 0
    x = rkv_ref[...].astype(F32)
    prev = _shifted(x, rkv_prev_ref[PREV_ROWS - 1:PREV_ROWS, :].astype(F32), first)
    x = x + mu_rkv_ref[...] * (prev - x)
    r_ref[...] = x[:, :width].astype(r_ref.dtype)
    k_ref[...] = x[:, width:2 * width].astype(k_ref.dtype)
    v_ref[...] = x[:, 2 * width:].astype(v_ref.dtype)

    lo = lo_ref[...]
    lo_prev = _shifted(lo, lo_prev_ref[PREV_ROWS - 1:PREV_ROWS, :], first)
    lo = lo + mu_lo_ref[...] * (lo_prev - lo)
    xw = lo[:, :DECAY_LORA]
    xa = lo[:, DECAY_LORA:DECAY_LORA + ICLR_LORA]
    xg = lo[:, DECAY_LORA + ICLR_LORA:]
    u = w0_ref[...] + _dot(jnp.tanh(xw).astype(BF16), wup_ref[...])
    lw_ref[...] = -float(np.exp(-0.5)) * _sigmoid(u)
    a_ref[...] = _sigmoid(a0_ref[...] + _dot(xa.astype(BF16), aup_ref[...]))
    g_ref[...] = _dot(_sigmoid(xg).astype(BF16), gup_ref[...]).astype(g_ref.dtype)


def _rwkv_prep(u_rkv, u_lo, mu, w0, w_up, a0, a_up, g_up, rows_per_batch):
    m, w3 = u_rkv.shape
    width = w3 // 3
    lo_w = DECAY_LORA + ICLR_LORA + GATE_LORA
    tb = 128
    assert rows_per_batch % tb == 0
    blocks_per_seq = rows_per_batch // tb
    sub = tb // PREV_ROWS
    prev_map = lambda i: (jnp.maximum(i * sub - 1, 0), 0)
    row = lambda i: (i, 0)
    fixed = lambda i: (0, 0)
    wide = jax.ShapeDtypeStruct((m, width), F32)
    narrow = jax.ShapeDtypeStruct((m, width), BF16)
    return pl.pallas_call(
        functools.partial(_rwkv_prep_kernel, blocks_per_seq=blocks_per_seq, width=width),
        grid=(m // tb,),
        in_specs=[pl.BlockSpec((tb, w3), row),
                  pl.BlockSpec((PREV_ROWS, w3), prev_map),
                  pl.BlockSpec((tb, lo_w), row),
                  pl.BlockSpec((PREV_ROWS, lo_w), prev_map),
                  pl.BlockSpec((1, w3), fixed),
                  pl.BlockSpec((1, lo_w), fixed),
                  pl.BlockSpec((1, width), fixed),
                  pl.BlockSpec((1, width), fixed),
                  pl.BlockSpec((DECAY_LORA, width), fixed),
                  pl.BlockSpec((ICLR_LORA, width), fixed),
                  pl.BlockSpec((GATE_LORA, width), fixed)],
        out_specs=[pl.BlockSpec((tb, width), row)] * 6,
        out_shape=[narrow, narrow, narrow, wide, wide, narrow],
        compiler_params=_cparams("parallel"),
        name="rwkv_prep",
    )(u_rkv, u_rkv, u_lo, u_lo, mu[:w3].reshape(1, w3), mu[w3:].reshape(1, lo_w),
      w0.reshape(1, width), a0.reshape(1, width), w_up.astype(BF16), a_up.astype(BF16),
      g_up.astype(BF16))


WKV_L = 64
WKV_HEADS = 16


def _wkv_kernel(r_ref, k_ref, v_ref, lw_ref, a_ref, g_ref, kk_ref, ka_ref, rk_ref, lnw_ref,
                lnb_ref, o_ref, state_ref):
    L = WKV_L
    n = C_HEAD_DIM

    @pl.when(pl.program_id(2) == 0)
    def _():
        state_ref[...] = jnp.zeros(state_ref.shape, F32)

    ti = lax.broadcasted_iota(jnp.int32, (L, L), 0)
    si = lax.broadcasted_iota(jnp.int32, (L, L), 1)
    tri_ones = (ti >= si).astype(F32)
    P = 2 * L
    def block_masks(width):
        tp = lax.broadcasted_iota(jnp.int32, (P, width), 0)
        sp = lax.broadcasted_iota(jnp.int32, (P, width), 1)
        sp = jnp.where(sp >= P, sp - P, sp)
        same_head = ((tp >= L) & (sp >= L)) | ((tp < L) & (sp < L))
        return same_head & (tp > sp), same_head & (tp >= sp), tp == sp

    bd_strict, _, diag = block_masks(P)
    _, bd_incl2, _ = block_masks(2 * P)
    eye = diag.astype(F32)
    head0 = lax.broadcasted_iota(jnp.int32, (L, P), 1) < n

    pairs = range(WKV_HEADS // 2)
    cols = [slice(p * P, (p + 1) * P) for p in pairs]

    def per_head(full, fn):
        out = []
        for c in cols:
            x = full[:, c]
            s0 = jnp.sum(jnp.where(head0, x, 0.0), axis=-1, keepdims=True)
            s1 = jnp.sum(jnp.where(head0, 0.0, x), axis=-1, keepdims=True)
            out.append(jnp.where(head0, fn(s0), fn(s1)))
        return jnp.concatenate(out, axis=1)

    r = r_ref[0].astype(F32)
    k = k_ref[0].astype(F32)
    v = v_ref[0].astype(F32)
    lw = lw_ref[0]
    a_lr = a_ref[0]
    kk = k * kk_ref[...]
    kk = kk * per_head(kk * kk, lambda s: lax.rsqrt(jnp.maximum(s, 1e-24)))
    k = k * (1.0 + (a_lr - 1.0) * ka_ref[...])
    b_s = kk * a_lr
    cum = _dot_f32(tri_ones, lw)
    c_end = cum[L - 1:L, :]
    c_inv = jnp.exp(-cum)
    c_out = jnp.exp(c_end - cum)
    a_t = -kk * jnp.exp(cum - lw)
    r_t = r * jnp.exp(cum)
    b_t = (b_s * c_inv).astype(BF16)
    k_t = (k * c_inv).astype(BF16)
    b_o = b_s * c_out
    k_o = k * c_out
    s_decay = jnp.exp(c_end)

    def split(x):
        return [jnp.where(head0, x, 0.0).astype(BF16), jnp.where(head0, 0.0, x).astype(BF16)]

    ar4 = [jnp.concatenate(split(a_t[:, c]) + split(r_t[:, c]), axis=0) for c in cols]
    bk4 = [jnp.concatenate([b_t[:, c]] * 2 + [k_t[:, c]] * 2, axis=0) for c in cols]
    bko4 = [jnp.concatenate(split(b_o[:, c]) + split(k_o[:, c]), axis=0) for c in cols]
    vbd = [jnp.concatenate(split(v[:, c]), axis=0) for c in cols]
    prod = [_dot_nt(ar4[p], bk4[p]) for p in pairs]
    n_ab = [jnp.where(bd_strict, z[:P, :P], 0.0) for z in prod]
    n_ak = [jnp.where(bd_strict, z[:P, P:], 0.0).astype(BF16) for z in prod]
    m_r = [jnp.where(bd_incl2, z[P:, :], 0.0).astype(BF16) for z in prod]

    inv = [eye + z for z in n_ab]
    pw = n_ab
    span = 1
    while 2 * span < L:
        pw_b = [z.astype(BF16) for z in pw]
        pw = [_dot(z, z) for z in pw_b]
        inv = [inv[p] + _dot(inv[p].astype(BF16), pw[p].astype(BF16)) for p in pairs]
        span *= 2

    s0 = [state_ref[p] for p in pairs]
    ah = [_dot_nt(ar4[p], s0[p].astype(BF16)) for p in pairs]
    rhs = [ah[p][:P] + _dot(n_ak[p], vbd[p]) for p in pairs]
    u = [_dot(inv[p].astype(BF16), rhs[p].astype(BF16)) for p in pairs]
    uv = [jnp.concatenate([u[p].astype(BF16), vbd[p]], axis=0) for p in pairs]
    y_bd = [ah[p][P:] + _dot(m_r[p], uv[p]) for p in pairs]
    y = jnp.concatenate([z[:L] + z[L:] for z in y_bd], axis=1)
    for p in pairs:
        s_inc = lax.dot_general(uv[p], bko4[p], (((0,), (0,)), ((), ())),
                                preferred_element_type=F32)
        state_ref[p] = s0[p] * s_decay[:, cols[p]] + s_inc

    mean = per_head(y, lambda s: s * (1.0 / n))
    yc = y - mean
    var = per_head(yc * yc, lambda s: s * (1.0 / n))
    yn = yc * lax.rsqrt(var + C_GN_EPS) * lnw_ref[...] + lnb_ref[...]
    bonus = per_head(r * k * rk_ref[...], lambda s: s) * v
    o_ref[0] = ((yn + bonus) * g_ref[0]).astype(o_ref.dtype)


def _wkv(r, k, v, lw, a, g, k_k, k_a, r_k, ln_w, ln_b, batch):
    m, width = r.shape
    t = m // batch
    hb = WKV_HEADS * C_HEAD_DIM
    assert t % WKV_L == 0 and width % hb == 0
    seq = lambda z: z.reshape(batch, t, width)
    par = lambda z: z.reshape(1, width).astype(F32)
    tok = pl.BlockSpec((1, WKV_L, hb), lambda bi, hg, c: (bi, c, hg))
    vec = pl.BlockSpec((1, hb), lambda bi, hg, c: (0, hg))
    out = pl.pallas_call(
        _wkv_kernel,
        grid=(batch, width // hb, t // WKV_L),
        in_specs=[tok] * 6 + [vec] * 5,
        out_specs=tok,
        out_shape=jax.ShapeDtypeStruct((batch, t, width), BF16),
        scratch_shapes=[pltpu.VMEM((WKV_HEADS // 2, 2 * C_HEAD_DIM, 2 * C_HEAD_DIM), F32)],
        compiler_params=_cparams("parallel", "parallel", "arbitrary"),
        name="wkv7_chunked",
    )(seq(r), seq(k), seq(v), seq(lw), seq(a), seq(g), par(k_k), par(k_a), par(r_k),
      par(ln_w), par(ln_b))
    return out.reshape(m, width)


def _rope_tables(t):
    half = ROPE_DIM // 2
    freqs = ROPE_THETA ** (-jnp.arange(half, dtype=F32) / half)
    ang = jnp.arange(t, dtype=F32)[:, None] * freqs[None, :]
    cos, sin = jnp.cos(ang), jnp.sin(ang)
    zeros = jnp.zeros((t, LANES - ROPE_DIM), F32)
    return (jnp.concatenate([cos, cos, zeros], axis=-1),
            jnp.concatenate([-sin, sin, zeros], axis=-1))


def _rope128(x, cos, sin):
    half = ROPE_DIM // 2
    swapped = pltpu.roll(x, LANES - half, axis=1) + pltpu.roll(x, half, axis=1)
    return x * cos + swapped * sin


def _mla_prep_kernel(cq_ref, ckv_ref, kpe_ref, qn_ref, kvn_ref, cos_ref, sin_ref,
                     cq_o, ckv_o, kpe_o):
    cq = cq_ref[...]
    cq_o[...] = (cq * lax.rsqrt(jnp.mean(cq * cq, axis=-1, keepdims=True) + EPS)
                 * qn_ref[...]).astype(cq_o.dtype)
    ckv = ckv_ref[...]
    ckv_o[...] = (ckv * lax.rsqrt(jnp.mean(ckv * ckv, axis=-1, keepdims=True) + EPS)
                  * kvn_ref[...]).astype(ckv_o.dtype)
    lane = lax.broadcasted_iota(jnp.int32, kpe_ref.shape, 1)
    kpe = jnp.where(lane < ROPE_DIM, kpe_ref[...], 0.0)
    kpe_o[...] = _rope128(kpe, cos_ref[...], sin_ref[...]).astype(kpe_o.dtype)


def _mla_prep(u_lo, q_norm, kv_norm, cos, sin, q_lora, kv_lora, col0, rows_per_batch):
    m = u_lo.shape[0]
    tb = _pick(rows_per_batch, (256, 128, 64, 32, 16, 8))
    per = rows_per_batch // tb
    row = lambda i: (i, 0)
    fixed = lambda i: (0, 0)
    return pl.pallas_call(
        _mla_prep_kernel,
        grid=(m // tb,),
        in_specs=[pl.BlockSpec((tb, q_lora), row),
                  pl.BlockSpec((tb, kv_lora), row),
                  pl.BlockSpec((tb, LANES), row),
                  pl.BlockSpec((1, q_lora), fixed),
                  pl.BlockSpec((1, kv_lora), fixed),
                  pl.BlockSpec((tb, LANES), lambda i: (i % per, 0)),
                  pl.BlockSpec((tb, LANES), lambda i: (i % per, 0))],
        out_specs=[pl.BlockSpec((tb, q_lora), row),
                   pl.BlockSpec((tb, kv_lora), row),
                   pl.BlockSpec((tb, LANES), row)],
        out_shape=[jax.ShapeDtypeStruct((m, q_lora), BF16),
                   jax.ShapeDtypeStruct((m, kv_lora), BF16),
                   jax.ShapeDtypeStruct((m, LANES), BF16)],
        compiler_params=_cparams("parallel"),
        name="mla_prep",
    )(u_lo[:, col0:col0 + q_lora], u_lo[:, col0 + q_lora:col0 + q_lora + kv_lora],
      u_lo[:, col0 + q_lora + kv_lora:col0 + q_lora + kv_lora + LANES],
      q_norm.reshape(1, q_lora), kv_norm.reshape(1, kv_lora), cos, sin)


MLA_TQ = 1024
MLA_TK = 512


def _chunk_causal_bias(tq, tk):
    r = np.arange(tq)[None, :, None]
    c = np.arange(tk)[None, None, :]
    d = np.arange(tq // tk)[:, None, None]
    return np.where(d * tk + c < (r // CHUNK + 1) * CHUNK, 0.0, NEG_INF).astype(np.float32)


def _mla_kernel(q_ref, kv_ref, kpe_ref, cos_ref, sin_ref, bias_ref, o_ref, qc_ref, m_ref, l_ref,
                acc_ref):
    i = pl.program_id(2)
    tq, tk = MLA_TQ, MLA_TK
    n_full = (i * tq) // tk
    qc_ref[:, :NOPE_DIM] = q_ref[0, :, :NOPE_DIM]
    qc_ref[:, NOPE_DIM:] = _rope128(q_ref[0, :, NOPE_DIM:].astype(F32), cos_ref[...],
                                    sin_ref[...]).astype(BF16)
    m_ref[...] = jnp.full(m_ref.shape, NEG_INF, F32)
    l_ref[...] = jnp.zeros(l_ref.shape, F32)
    acc_ref[...] = jnp.zeros(acc_ref.shape, F32)

    def tile(kt, diag):
        off = pl.multiple_of(kt * tk, tk)
        k_cat = jnp.concatenate([kv_ref[0, pl.ds(off, tk), :NOPE_DIM],
                                 kpe_ref[0, pl.ds(off, tk), :]], axis=1)
        v_t = kv_ref[0, pl.ds(off, tk), NOPE_DIM:]
        mask = None
        starts = list(range(0, tq, FLASH_ROWS))
        if diag is not None:
            starts = [r for r in starts if r >= diag * tk]
            mask = [bias_ref[diag, r:r + FLASH_ROWS, :] for r in starts]
        _flash_update(qc_ref, starts, k_cat, v_t, mask, m_ref, l_ref, acc_ref)

    _loop_by_two(n_full, lambda kt: tile(kt, None))
    for d in range(tq // tk):
        tile(n_full + d, d)
    o_ref[0] = (acc_ref[...] / l_ref[...]).astype(o_ref.dtype)


def _mla(q, kv, kpe, cos, sin, heads):
    b, t, _ = q.shape
    assert t % MLA_TQ == 0 and MLA_TQ % MLA_TK == 0
    hw = NOPE_DIM + LANES
    bias = _chunk_causal_bias(MLA_TQ, MLA_TK)
    return pl.pallas_call(
        _mla_kernel,
        grid=(b, heads, t // MLA_TQ),
        in_specs=[pl.BlockSpec((1, MLA_TQ, hw), lambda bi, h, i: (bi, i, h)),
                  pl.BlockSpec((1, t, NOPE_DIM + V_DIM), lambda bi, h, i: (bi, 0, h)),
                  pl.BlockSpec((1, t, LANES), lambda bi, h, i: (bi, 0, 0)),
                  pl.BlockSpec((MLA_TQ, LANES), lambda bi, h, i: (i, 0)),
                  pl.BlockSpec((MLA_TQ, LANES), lambda bi, h, i: (i, 0)),
                  pl.BlockSpec(bias.shape, lambda bi, h, i: (0, 0, 0))],
        out_specs=pl.BlockSpec((1, MLA_TQ, V_DIM), lambda bi, h, i: (bi, i, h)),
        out_shape=jax.ShapeDtypeStruct((b, t, heads * V_DIM), BF16),
        scratch_shapes=[pltpu.VMEM((MLA_TQ, hw), BF16),
                        pltpu.VMEM((MLA_TQ, LANES), F32),
                        pltpu.VMEM((MLA_TQ, LANES), F32),
                        pltpu.VMEM((MLA_TQ, V_DIM), F32)],
        compiler_params=_cparams("parallel", "parallel", "arbitrary"),
        name="mla_attention",
    )(q, kv, kpe, cos, sin, jnp.asarray(bias))


def _pad_cols(w, n):
    return jnp.pad(w, ((0, 0), (0, n - w.shape[1])))


def _even_mixer(h, x, gate, w_in, w_out, e, rel_bias, batch):
    m, d = h.shape
    t = m // batch
    heads = d // (2 * HEAD_DIM)
    idx_heads = d // 128
    hw = heads * HEAD_DIM
    o_kb = 4 * hw
    o_qi = o_kb + 2 * HEAD_DIM
    o_ki = o_qi + idx_heads * IDX_DIM
    q_factor = LOG2_E * HEAD_DIM ** -0.5
    w_a = jnp.concatenate([w_in[:, :hw] * q_factor, w_in[:, hw:3 * hw]], axis=1)
    u_a = _matmul(h, w_a.astype(BF16), out_dtype=BF16)
    w_qb = w_in[:, 3 * hw:4 * hw] * q_factor
    u_q = _matmul(h, jnp.concatenate([w_qb, w_in[:, o_qi:o_ki]], axis=1).astype(BF16),
                  out_dtype=BF16)
    small = jnp.concatenate([w_in[:, o_kb:o_qi], w_in[:, o_ki:]], axis=1)
    u_s = _matmul(h, _pad_cols(small, 3 * LANES).astype(BF16), out_dtype=F32, tn=3 * LANES)
    u_s = u_s.reshape(batch, t, 3 * LANES)
    kb = u_s[..., :HEAD_DIM].astype(BF16)
    vb = u_s[..., HEAD_DIM:2 * HEAD_DIM].astype(BF16)
    ki = u_s[..., 2 * HEAD_DIM:2 * HEAD_DIM + IDX_DIM].astype(BF16)
    wi = u_s[..., 2 * HEAD_DIM + IDX_DIM:2 * HEAD_DIM + IDX_DIM + idx_heads]
    ya = _attn_a(u_a.reshape(batch, t, 3 * hw), _band_bias_tiles(rel_bias), heads,
                 0, heads, 2 * heads)
    yb = _dsa(u_q.reshape(batch, t, 2 * hw), wi, kb, vb, ki, heads, idx_heads)
    return _matmul((ya.reshape(m, hw), yb.reshape(m, hw)), w_out, w_layer=e, out_dtype=F32,
                   epilogue="resid", x=x, gate=gate, rows_per_batch=t)


def _odd_mixer(h, x, gate, w_in, w_out, o, c_mu, c_w0, c_w_up, c_a0, c_a_up, c_g_up, c_k_k, c_k_a,
               c_r_k, c_ln_w, c_ln_b, d_q_norm, d_kv_norm, d_w_uq, d_w_ukv, batch):
    m, d = h.shape
    t = m // batch
    width = c_w0.shape[0]
    heads = d // (2 * HEAD_DIM)
    q_lora = d_q_norm.shape[0]
    kv_lora = d_kv_norm.shape[0]
    lo_w = DECAY_LORA + ICLR_LORA + GATE_LORA
    w_in = w_in.astype(BF16)
    u_rkv = _matmul(h, w_in[:, :3 * width], out_dtype=BF16)
    rest = w_in[:, 3 * width:]
    rest_cols = -(-(rest.shape[1] + LANES - ROPE_DIM) // 512) * 512
    u_lo = _matmul(h, _pad_cols(rest, rest_cols), out_dtype=F32)

    r, k, v, lw, a, g = _rwkv_prep(u_rkv, u_lo, c_mu, c_w0, c_w_up, c_a0, c_a_up, c_g_up, t)
    yc = _wkv(r, k, v, lw, a, g, c_k_k, c_k_a, c_r_k, c_ln_w, c_ln_b, batch)

    cos, sin = _rope_tables(t)
    cq, ckv, kpe = _mla_prep(u_lo, d_q_norm, d_kv_norm, cos, sin, q_lora, kv_lora, lo_w, t)
    w_uq = d_w_uq.reshape(q_lora, heads, NOPE_DIM + ROPE_DIM)
    w_uq = jnp.pad(w_uq, ((0, 0), (0, 0), (0, LANES - ROPE_DIM))).reshape(q_lora, -1)
    w_uq = w_uq * (LOG2_E * (NOPE_DIM + ROPE_DIM) ** -0.5)
    q = _matmul(cq, w_uq.astype(BF16), out_dtype=BF16)
    kv = _matmul(ckv, d_w_ukv, w_layer=o, out_dtype=BF16)
    yd = _mla(q.reshape(batch, t, -1), kv.reshape(batch, t, -1), kpe.reshape(batch, t, LANES),
              cos, sin, heads)
    return _matmul((yc, yd.reshape(m, -1)), w_out, w_layer=o, out_dtype=F32,
                   epilogue="resid", x=x, gate=gate, rows_per_batch=t)


def kernel(x, c, ada_w, ada_table, norm_mix, norm_ffn, norm_final, ffn_w1, ffn_w2, ev_w_in,
           ev_w_out, a_rel_bias, od_w_in, od_w_out, c_mu, c_w0, c_w_up, c_a0, c_a_up, c_g_up,
           c_k_k, c_k_a, c_r_k, c_ln_w, c_ln_b, d_q_norm, d_kv_norm, d_w_uq, d_w_ukv):
    batch, t, d = x.shape
    depth = ada_table.shape[0]
    m = batch * t
    silu_c = (c * _sigmoid(c)).astype(BF16)
    silu_c = jnp.pad(silu_c, ((0, 8 - batch % 8 if batch % 8 else 0), (0, 0)))
    ada = _matmul(silu_c, ada_w, out_dtype=F32)[:batch].reshape(batch, 6, d)
    x = x.reshape(m, d)
    ev_w_out = ev_w_out.astype(BF16)
    od_w_out = od_w_out.astype(BF16)
    d_w_ukv = d_w_ukv.astype(BF16)
    for layer in range(depth):
        mod = ada + ada_table[layer]
        shift_m, scale_m, gate_m, shift_f, scale_f, gate_f = (mod[:, i, :] for i in range(6))
        h = _norm_mod(x, norm_mix[layer], scale_m, shift_m, t)
        if layer % 2 == 0:
            e = layer // 2
            x = _even_mixer(h, x, gate_m, ev_w_in[e], ev_w_out, e, a_rel_bias[e], batch)
        else:
            o = layer // 2
            x = _odd_mixer(h, x, gate_m, od_w_in[o], od_w_out, o, c_mu[o], c_w0[o], c_w_up[o],
                           c_a0[o], c_a_up[o], c_g_up[o], c_k_k[o], c_k_a[o], c_r_k[o],
                           c_ln_w[o], c_ln_b[o], d_q_norm[o], d_kv_norm[o], d_w_uq[o],
                           d_w_ukv, batch)
        h = _norm_mod(x, norm_ffn[layer], scale_f, shift_f, t)
        h1 = _matmul(h, ffn_w1, w_layer=layer, out_dtype=BF16, epilogue="relu2")
        x = _matmul(h1, ffn_w2, w_layer=layer, out_dtype=F32, epilogue="resid", x=x,
                    gate=gate_f, rows_per_batch=t)
    return _norm_plain(x, norm_final, F32).reshape(batch, t, d)
```

```python
import functools

import numpy as np
import jax
import jax.numpy as jnp
from jax import lax
from jax.experimental import pallas as pl
from jax.experimental.pallas import tpu as pltpu

F32 = jnp.float32
BF16 = jnp.bfloat16

CHUNK = 64
Q_BLOCK = 128
HEAD_DIM = 128
EPS = 1e-6
NEG_INF = -1e30
A_LEFT_CHUNKS = 8
A_MAX_REL = 256
IDX_DIM = 64
TOPK_MAX = 256
C_HEAD_DIM = 64
DECAY_LORA = 128
ICLR_LORA = 128
GATE_LORA = 256
C_GN_EPS = 64e-5
NOPE_DIM = 128
ROPE_DIM = 64
V_DIM = 128
ROPE_THETA = 10000.0

V7X_VMEM_LIMIT_BYTES = 56 * 1024 * 1024
LANES = 128
INT32_MIN = -(2 ** 31)
LOG2_E = 1.4426950408889634


def _cparams(*sem):
    return pltpu.CompilerParams(dimension_semantics=sem, vmem_limit_bytes=V7X_VMEM_LIMIT_BYTES)


def _dot_nt(a, b):
    return lax.dot_general(a, b, (((1,), (1,)), ((), ())), preferred_element_type=F32)


def _dot(a, b):
    return jnp.dot(a, b, preferred_element_type=F32)


def _dot_f32(a, b):
    return jnp.dot(a, b, preferred_element_type=F32, precision=lax.Precision.HIGHEST)


FLASH_ROWS = 128


def _flash_update(q_ref, row_starts, k_t, v_t, mask, m_ref, l_ref, acc_ref, sum_on_mxu=False):
    tk = k_t.shape[0]
    reps = tk // LANES
    rows = [slice(r, r + FLASH_ROWS) for r in row_starts]
    s = [_dot_nt(q_ref[r, :], k_t) for r in rows]
    if mask is not None:
        masks = mask if isinstance(mask, list) else [mask] * len(rows)
        s = [x + mk for x, mk in zip(s, masks)]
    m_old = [m_ref[r, :] for r in rows]
    m_new = [jnp.maximum(mo, jnp.max(x, axis=-1, keepdims=True)) for mo, x in zip(m_old, s)]
    p = [jnp.exp2(x - jnp.concatenate([mn] * reps, axis=1)) for x, mn in zip(s, m_new)]
    alpha = [jnp.exp2(mo - mn) for mo, mn in zip(m_old, m_new)]
    if sum_on_mxu:
        v_t = jnp.concatenate([v_t, jnp.ones((tk, LANES), v_t.dtype)], axis=1)
    pv = [_dot(x.astype(v_t.dtype), v_t) for x in p]
    dv = acc_ref.shape[1]
    for r, mn, al, x, y in zip(rows, m_new, alpha, p, pv):
        m_ref[r, :] = mn
        row_sum = y[:, dv:] if sum_on_mxu else jnp.sum(x, axis=-1, keepdims=True)
        l_ref[r, :] = al * l_ref[r, :] + row_sum
        acc_ref[r, :] = al * acc_ref[r, :] + y[:, :dv]


def _loop_by_two(n, body):
    def pair(j, carry):
        body(2 * j)
        body(2 * j + 1)
        return carry

    lax.fori_loop(0, n // 2, pair, 0)

    @pl.when(n % 2 == 1)
    def _():
        body(n - 1)


def _pick(n, prefs):
    for p in prefs:
        if n % p == 0:
            return p
    return n


MM_ACC_CHUNK = 256

def _mm_kernel(*refs, nk, n_lhs, epilogue):
    a_refs = refs[:n_lhs]
    w_ref = refs[n_lhs]
    if epilogue == "resid":
        x_ref, g_ref, o_ref = refs[n_lhs + 1:n_lhs + 4]
        scratch = refs[n_lhs + 4:]
    else:
        o_ref = refs[n_lhs + 1]
        scratch = refs[n_lhs + 2:]

    tn = o_ref.shape[1]

    def finish(acc, cols):
        if epilogue == "relu2":
            r = jnp.maximum(acc, 0.0)
            acc = r * r
        elif epilogue == "resid":
            acc = x_ref[:, cols] + g_ref[0, :, cols] * acc
        o_ref[:, cols] = acc.astype(o_ref.dtype)

    def product(cols):
        part, row = None, 0
        for a_ref in a_refs:
            kw = a_ref.shape[1]
            term = _dot(a_ref[...], w_ref[row:row + kw, cols].astype(BF16))
            part = term if part is None else part + term
            row += kw
        return part

    if nk == 1:
        finish(product(slice(0, tn)), slice(0, tn))
        return

    acc_ref = scratch[0]
    k = pl.program_id(2)
    chunks = [slice(c, c + MM_ACC_CHUNK) for c in range(0, tn, MM_ACC_CHUNK)]

    @pl.when(k == 0)
    def _():
        for cols in chunks:
            acc_ref[:, cols] = product(cols)

    @pl.when((k > 0) & (k < nk - 1))
    def _():
        for cols in chunks:
            acc_ref[:, cols] += product(cols)

    @pl.when(k == nk - 1)
    def _():
        for cols in chunks:
            finish(acc_ref[:, cols] + product(cols), cols)


def _matmul(a, w, *, out_dtype, w_layer=None, epilogue="none", x=None, gate=None,
            rows_per_batch=None, tm=None, tn=None, tk=None):
    lhs = a if isinstance(a, tuple) else (a,)
    m = lhs[0].shape[0]
    assert (w.ndim == 3) == (w_layer is not None)
    kd, n = w.shape[-2:]
    assert sum(z.shape[1] for z in lhs) == kd
    tm = tm or _pick(m, (1024, 512, 256, 128, 64, 32, 16, 8))
    tk = tk or (kd if kd <= 4096 else _pick(kd, (2048, 1024, 512)))
    tn = tn or _pick(n, (1024, 512, 384, 256, 128) if tk < kd else (512, 384, 256, 128))
    nk = kd // tk
    assert m % tm == 0 and n % tn == 0 and kd % tk == 0 and (nk == 1 or len(lhs) == 1)
    in_specs = [pl.BlockSpec((tm, z.shape[1] if nk == 1 else tk), lambda i, j, k: (i, k))
                for z in lhs]
    if w_layer is None:
        in_specs.append(pl.BlockSpec((tk, tn), lambda i, j, k: (k, j)))
    else:
        in_specs.append(pl.BlockSpec((None, tk, tn), lambda i, j, k: (w_layer, k, j)))
    args = [*lhs, w]
    if epilogue == "resid":
        assert rows_per_batch % tm == 0
        per = rows_per_batch // tm
        in_specs += [pl.BlockSpec((tm, tn), lambda i, j, k: (i, j)),
                     pl.BlockSpec((1, 1, tn), lambda i, j, k: (i // per, 0, j))]
        args += [x, gate.reshape(gate.shape[0], 1, n)]
    return pl.pallas_call(
        functools.partial(_mm_kernel, nk=nk, n_lhs=len(lhs), epilogue=epilogue),
        grid=(m // tm, n // tn, nk),
        in_specs=in_specs,
        out_specs=pl.BlockSpec((tm, tn), lambda i, j, k: (i, j)),
        out_shape=jax.ShapeDtypeStruct((m, n), out_dtype),
        scratch_shapes=[pltpu.VMEM((tm, tn), F32)] if nk > 1 else [],
        compiler_params=_cparams("parallel", "parallel", "arbitrary"),
        name="matmul_" + epilogue,
    )(*args)


def _norm_mod_kernel(x_ref, g_ref, sc_ref, sh_ref, o_ref):
    x = x_ref[...]
    ms = jnp.mean(x * x, axis=-1, keepdims=True)
    y = x * lax.rsqrt(ms + EPS) * g_ref[...]
    o_ref[...] = (y * (1.0 + sc_ref[0]) + sh_ref[0]).astype(o_ref.dtype)


def _norm_plain_kernel(x_ref, g_ref, o_ref):
    x = x_ref[...]
    ms = jnp.mean(x * x, axis=-1, keepdims=True)
    o_ref[...] = (x * lax.rsqrt(ms + EPS) * g_ref[...]).astype(o_ref.dtype)


def _norm_mod(x, g, scale, shift, rows_per_batch, out_dtype=BF16):
    m, d = x.shape
    tr = _pick(rows_per_batch, (256, 128, 64, 32, 16, 8))
    per = rows_per_batch // tr
    b = scale.shape[0]
    return pl.pallas_call(
        _norm_mod_kernel,
        grid=(m // tr,),
        in_specs=[pl.BlockSpec((tr, d), lambda i: (i, 0)),
                  pl.BlockSpec((1, d), lambda i: (0, 0)),
                  pl.BlockSpec((1, 1, d), lambda i: (i // per, 0, 0)),
                  pl.BlockSpec((1, 1, d), lambda i: (i // per, 0, 0))],
        out_specs=pl.BlockSpec((tr, d), lambda i: (i, 0)),
        out_shape=jax.ShapeDtypeStruct((m, d), out_dtype),
        compiler_params=_cparams("parallel"),
        name="norm_mod",
    )(x, g.reshape(1, d), scale.reshape(b, 1, d), shift.reshape(b, 1, d))


def _norm_plain(x, g, out_dtype):
    m, d = x.shape
    tr = _pick(m, (256, 128, 64, 32, 16, 8))
    return pl.pallas_call(
        _norm_plain_kernel,
        grid=(m // tr,),
        in_specs=[pl.BlockSpec((tr, d), lambda i: (i, 0)),
                  pl.BlockSpec((1, d), lambda i: (0, 0))],
        out_specs=pl.BlockSpec((tr, d), lambda i: (i, 0)),
        out_shape=jax.ShapeDtypeStruct((m, d), out_dtype),
        compiler_params=_cparams("parallel"),
        name="norm_plain",
    )(x, g.reshape(1, d))


A_PAD = A_LEFT_CHUNKS * CHUNK
A_WIN = A_PAD + Q_BLOCK
A_VARIANTS = A_PAD // Q_BLOCK + 1


def _band_bias_tiles(rel_bias):
    width = A_WIN + Q_BLOCK
    tiles = []
    for v in range(A_VARIANTS):
        rel = np.clip(v * Q_BLOCK + Q_BLOCK - 1 - np.arange(width), -A_MAX_REL, A_MAX_REL)
        diag = jnp.pad(rel_bias.astype(F32)[:, rel + A_MAX_REL] * LOG2_E, ((0, 0), (0, 1)))
        skew = jnp.tile(diag, (1, Q_BLOCK))[:, :Q_BLOCK * width].reshape(-1, Q_BLOCK, width)
        tiles.append(skew[:, :, Q_BLOCK - 1:Q_BLOCK - 1 + A_WIN])
    return jnp.stack(tiles, axis=1)


A_HEADS_PER_STEP = 8


def _attn_a_kernel(q_ref, k_ref, v_ref, b_ref, o_ref):
    i = pl.program_id(2)
    ks = pl.multiple_of(jnp.maximum(i * Q_BLOCK - A_PAD, 0), Q_BLOCK)
    qpos = i * Q_BLOCK + lax.broadcasted_iota(jnp.int32, (Q_BLOCK, A_WIN), 0)
    kpos = ks + lax.broadcasted_iota(jnp.int32, (Q_BLOCK, A_WIN), 1)
    chunk_start = (qpos // CHUNK) * CHUNK
    valid = (kpos >= chunk_start - A_PAD) & (kpos < chunk_start + CHUNK)
    cols = [slice(h * HEAD_DIM, (h + 1) * HEAD_DIM) for h in range(A_HEADS_PER_STEP)]
    vs = [v_ref[0, pl.ds(ks, A_WIN), c] for c in cols]
    s = [_dot_nt(q_ref[0, :, c], k_ref[0, pl.ds(ks, A_WIN), c]) + b_ref[h, 0]
         for h, c in enumerate(cols)]
    s = [jnp.where(valid, x, NEG_INF) for x in s]
    p = [jnp.exp2(x - jnp.max(x, axis=-1, keepdims=True)) for x in s]
    l = [jnp.sum(x, axis=-1, keepdims=True) for x in p]
    o = [_dot(x.astype(v.dtype), v) for x, v in zip(p, vs)]
    for c, x, y in zip(cols, o, l):
        o_ref[0, :, c] = (x / y).astype(o_ref.dtype)


def _attn_a(u, bias_tiles, heads, q_col, k_col, v_col):
    b, t, _ = u.shape
    hb = A_HEADS_PER_STEP
    assert t % Q_BLOCK == 0 and t >= A_WIN
    assert heads % hb == 0 and q_col % hb == 0 and k_col % hb == 0 and v_col % hb == 0
    nq = t // Q_BLOCK
    return pl.pallas_call(
        _attn_a_kernel,
        grid=(b, heads // hb, nq),
        in_specs=[pl.BlockSpec((1, Q_BLOCK, hb * HEAD_DIM),
                               lambda bi, h, i: (bi, i, q_col // hb + h)),
                  pl.BlockSpec((1, t, hb * HEAD_DIM), lambda bi, h, i: (bi, 0, k_col // hb + h)),
                  pl.BlockSpec((1, t, hb * HEAD_DIM), lambda bi, h, i: (bi, 0, v_col // hb + h)),
                  pl.BlockSpec((hb, 1, Q_BLOCK, A_WIN),
                               lambda bi, h, i: (h, jnp.minimum(i, A_VARIANTS - 1), 0, 0))],
        out_specs=pl.BlockSpec((1, Q_BLOCK, hb * HEAD_DIM), lambda bi, h, i: (bi, i, h)),
        out_shape=jax.ShapeDtypeStruct((b, t, heads * HEAD_DIM), BF16),
        compiler_params=_cparams("parallel", "parallel", "arbitrary"),
        name="band_attention",
    )(u, u, u, bias_tiles)


DSA_TK = 512


def _sortable_key(x):
    bits = lax.bitcast_convert_type(x, jnp.int32)
    return jnp.where(bits < 0, bits ^ jnp.int32(0x7FFFFFFF), bits)


DSA_GROUP = 8


def _dsa_kernel(qb_ref, qi_ref, wi_ref, k_ref, v_ref, ki_ref, o_ref,
                key_ref, qs_ref, m_ref, l_ref, acc_ref, *, heads, idx_heads, topk):
    i = pl.program_id(1)
    tk = DSA_TK
    n_kt = (i * Q_BLOCK + Q_BLOCK + tk - 1) // tk
    row = lax.broadcasted_iota(jnp.int32, (Q_BLOCK, 1), 0)
    limit = ((i * Q_BLOCK + row) // CHUNK + 1) * CHUNK

    def lane_fold(x):
        out = x[:, :LANES]
        for c in range(1, tk // LANES):
            out = out + x[:, c * LANES:(c + 1) * LANES]
        return out

    def score_tile(kt, carry):
        off = pl.multiple_of(kt * tk, tk)
        ki_t = ki_ref[0, pl.ds(off, tk), :]
        acc = jnp.zeros((Q_BLOCK, tk), F32)
        for h in range(idx_heads):
            logits = _dot_nt(qi_ref[0, :, h * IDX_DIM:(h + 1) * IDX_DIM], ki_t)
            acc = acc + wi_ref[0, :, h:h + 1] * jnp.maximum(logits, 0.0)
        kpos = off + lax.broadcasted_iota(jnp.int32, (Q_BLOCK, tk), 1)
        key_ref[:, pl.ds(off, tk)] = _sortable_key(jnp.where(kpos < limit, acc, NEG_INF))
        return carry

    lax.fori_loop(0, n_kt, score_tile, 0)

    def bit_step(bi, t):
        cand = t + lax.shift_left(jnp.int32(1), 31 - bi)

        def count_tile(kt, c):
            off = pl.multiple_of(kt * tk, tk)
            return c + lane_fold(jnp.where(key_ref[:, pl.ds(off, tk)] >= cand, 1.0, 0.0))

        part = lax.fori_loop(0, n_kt, count_tile, jnp.zeros((Q_BLOCK, LANES), F32))
        cnt = jnp.sum(part, axis=-1, keepdims=True)
        return jnp.where(cnt >= float(topk), cand, t)

    thr = lax.fori_loop(0, 32, bit_step, jnp.full((Q_BLOCK, 1), INT32_MIN, jnp.int32))

    def tie_tile(kt, carry):
        gt, eq = carry
        off = pl.multiple_of(kt * tk, tk)
        keys = key_ref[:, pl.ds(off, tk)]
        kpos = off + lax.broadcasted_iota(jnp.int32, (Q_BLOCK, tk), 1)
        gt = gt + lane_fold(jnp.where(keys > thr, 1.0, 0.0))
        eq = eq + lane_fold(jnp.where((keys == thr) & (kpos < limit), 1.0, 0.0))
        return gt, eq

    zeros = jnp.zeros((Q_BLOCK, LANES), F32)
    gt, eq = lax.fori_loop(0, n_kt, tie_tile, (zeros, zeros))
    need = float(topk) - jnp.sum(gt, axis=-1, keepdims=True)
    surplus = jnp.max(jnp.sum(eq, axis=-1, keepdims=True) - need)

    @pl.when(surplus > 0.0)
    def _():
        ki_ = lax.broadcasted_iota(jnp.int32, (tk, tk), 0)
        kj_ = lax.broadcasted_iota(jnp.int32, (tk, tk), 1)
        before = (ki_ < kj_).astype(BF16)

        def drop_tile(kt, seen):
            off = pl.multiple_of(kt * tk, tk)
            keys = key_ref[:, pl.ds(off, tk)]
            kpos = off + lax.broadcasted_iota(jnp.int32, (Q_BLOCK, tk), 1)
            is_eq = (keys == thr) & (kpos < limit)
            eq_f = jnp.where(is_eq, 1.0, 0.0)
            rank = seen + _dot(eq_f.astype(BF16), before)
            key_ref[:, pl.ds(off, tk)] = jnp.where(is_eq & (rank >= need), keys - 1, keys)
            return seen + jnp.sum(eq_f, axis=-1, keepdims=True)

        lax.fori_loop(0, n_kt, drop_tile, jnp.zeros((Q_BLOCK, 1), F32))

    for h in range(heads):
        qs_ref[h * Q_BLOCK:(h + 1) * Q_BLOCK, :] = qb_ref[0, :, h * HEAD_DIM:(h + 1) * HEAD_DIM]
    m_ref[...] = jnp.full(m_ref.shape, NEG_INF, F32)
    l_ref[...] = jnp.zeros(l_ref.shape, F32)
    acc_ref[...] = jnp.zeros(acc_ref.shape, F32)

    def attend_tile(kt, carry):
        off = pl.multiple_of(kt * tk, tk)
        k_t = k_ref[0, pl.ds(off, tk), :]
        v_t = v_ref[0, pl.ds(off, tk), :]
        kpos = off + lax.broadcasted_iota(jnp.int32, (Q_BLOCK, tk), 1)
        sel = (key_ref[:, pl.ds(off, tk)] >= thr) & (kpos < limit)
        sel = jnp.where(sel, 0.0, NEG_INF)
        for g in range(0, heads, DSA_GROUP):
            _flash_update(qs_ref, [h * Q_BLOCK for h in range(g, g + DSA_GROUP)], k_t, v_t, sel,
                          m_ref, l_ref, acc_ref, sum_on_mxu=True)
        return carry

    _loop_by_two(n_kt, lambda kt: attend_tile(kt, 0))
    for h in range(heads):
        rows = slice(h * Q_BLOCK, (h + 1) * Q_BLOCK)
        o_ref[0, :, h * HEAD_DIM:(h + 1) * HEAD_DIM] = (
            acc_ref[rows, :] / l_ref[rows, :]).astype(o_ref.dtype)


def _dsa(u_q, wi, kb, vb, ki, heads, idx_heads):
    b, t, _ = u_q.shape
    assert heads * HEAD_DIM == idx_heads * IDX_DIM and heads % DSA_GROUP == 0
    qw = heads * HEAD_DIM
    topk = min(TOPK_MAX, t // 4)
    assert t % DSA_TK == 0 and topk <= DSA_TK
    nq = t // Q_BLOCK
    return pl.pallas_call(
        functools.partial(_dsa_kernel, heads=heads, idx_heads=idx_heads, topk=topk),
        grid=(b, nq),
        in_specs=[pl.BlockSpec((1, Q_BLOCK, qw), lambda bi, i: (bi, i, 0)),
                  pl.BlockSpec((1, Q_BLOCK, qw), lambda bi, i: (bi, i, 1)),
                  pl.BlockSpec((1, Q_BLOCK, idx_heads), lambda bi, i: (bi, i, 0)),
                  pl.BlockSpec((1, t, HEAD_DIM), lambda bi, i: (bi, 0, 0)),
                  pl.BlockSpec((1, t, HEAD_DIM), lambda bi, i: (bi, 0, 0)),
                  pl.BlockSpec((1, t, IDX_DIM), lambda bi, i: (bi, 0, 0))],
        out_specs=pl.BlockSpec((1, Q_BLOCK, qw), lambda bi, i: (bi, i, 0)),
        out_shape=jax.ShapeDtypeStruct((b, t, qw), BF16),
        scratch_shapes=[pltpu.VMEM((Q_BLOCK, t), jnp.int32),
                        pltpu.VMEM((heads * Q_BLOCK, HEAD_DIM), BF16),
                        pltpu.VMEM((heads * Q_BLOCK, LANES), F32),
                        pltpu.VMEM((heads * Q_BLOCK, LANES), F32),
                        pltpu.VMEM((heads * Q_BLOCK, HEAD_DIM), F32)],
        compiler_params=_cparams("parallel", "arbitrary"),
        name="dsa_attention",
    )(u_q, u_q, wi, kb, vb, ki)


def _shifted(x, prev_row, first):
    rolled = pltpu.roll(x, 1, axis=0)
    head = jnp.where(first, jnp.zeros_like(prev_row), prev_row)
    rid = lax.broadcasted_iota(jnp.int32, x.shape, 0)
    return jnp.where(rid == 0, head, rolled)


def _sigmoid(x):
    return 1.0 / (1.0 + jnp.exp(-x))


PREV_ROWS = 16


def _rwkv_prep_kernel(rkv_ref, rkv_prev_ref, lo_ref, lo_prev_ref, mu_rkv_ref, mu_lo_ref,
                      w0_ref, a0_ref, wup_ref, aup_ref, gup_ref,
                      r_ref, k_ref, v_ref, lw_ref, a_ref, g_ref, *, blocks_per_seq, width):
    first = (pl.program_id(0) % blocks_per_seq) == 0
    x = rkv_ref[...].astype(F32)
    prev = _shifted(x, rkv_prev_ref[PREV_ROWS - 1:PREV_ROWS, :].astype(F32), first)
    x = x + mu_rkv_ref[...] * (prev - x)
    r_ref[...] = x[:, :width].astype(r_ref.dtype)
    k_ref[...] = x[:, width:2 * width].astype(k_ref.dtype)
    v_ref[...] = x[:, 2 * width:].astype(v_ref.dtype)

    lo = lo_ref[...]
    lo_prev = _shifted(lo, lo_prev_ref[PREV_ROWS - 1:PREV_ROWS, :], first)
    lo = lo + mu_lo_ref[...] * (lo_prev - lo)
    xw = lo[:, :DECAY_LORA]
    xa = lo[:, DECAY_LORA:DECAY_LORA + ICLR_LORA]
    xg = lo[:, DECAY_LORA + ICLR_LORA:]
    u = w0_ref[...] + _dot(jnp.tanh(xw).astype(BF16), wup_ref[...])
    lw_ref[...] = -float(np.exp(-0.5)) * _sigmoid(u)
    a_ref[...] = _sigmoid(a0_ref[...] + _dot(xa.astype(BF16), aup_ref[...]))
    g_ref[...] = _dot(_sigmoid(xg).astype(BF16), gup_ref[...]).astype(g_ref.dtype)


def _rwkv_prep(u_rkv, u_lo, mu, w0, w_up, a0, a_up, g_up, rows_per_batch):
    m, w3 = u_rkv.shape
    width = w3 // 3
    lo_w = DECAY_LORA + ICLR_LORA + GATE_LORA
    tb = 128
    assert rows_per_batch % tb == 0
    blocks_per_seq = rows_per_batch // tb
    sub = tb // PREV_ROWS
    prev_map = lambda i: (jnp.maximum(i * sub - 1, 0), 0)
    row = lambda i: (i, 0)
    fixed = lambda i: (0, 0)
    wide = jax.ShapeDtypeStruct((m, width), F32)
    narrow = jax.ShapeDtypeStruct((m, width), BF16)
    return pl.pallas_call(
        functools.partial(_rwkv_prep_kernel, blocks_per_seq=blocks_per_seq, width=width),
        grid=(m // tb,),
        in_specs=[pl.BlockSpec((tb, w3), row),
                  pl.BlockSpec((PREV_ROWS, w3), prev_map),
                  pl.BlockSpec((tb, lo_w), row),
                  pl.BlockSpec((PREV_ROWS, lo_w), prev_map),
                  pl.BlockSpec((1, w3), fixed),
                  pl.BlockSpec((1, lo_w), fixed),
                  pl.BlockSpec((1, width), fixed),
                  pl.BlockSpec((1, width), fixed),
                  pl.BlockSpec((DECAY_LORA, width), fixed),
                  pl.BlockSpec((ICLR_LORA, width), fixed),
                  pl.BlockSpec((GATE_LORA, width), fixed)],
        out_specs=[pl.BlockSpec((tb, width), row)] * 6,
        out_shape=[narrow, narrow, narrow, wide, wide, narrow],
        compiler_params=_cparams("parallel"),
        name="rwkv_prep",
    )(u_rkv, u_rkv, u_lo, u_lo, mu[:w3].reshape(1, w3), mu[w3:].reshape(1, lo_w),
      w0.reshape(1, width), a0.reshape(1, width), w_up.astype(BF16), a_up.astype(BF16),
      g_up.astype(BF16))


WKV_L = 64
WKV_HEADS = 32


def _wkv_kernel(r_ref, k_ref, v_ref, lw_ref, a_ref, g_ref, kk_ref, ka_ref, rk_ref, lnw_ref,
                lnb_ref, o_ref, state_ref):
    L = WKV_L
    n = C_HEAD_DIM

    @pl.when(pl.program_id(2) == 0)
    def _():
        state_ref[...] = jnp.zeros(state_ref.shape, F32)

    ti = lax.broadcasted_iota(jnp.int32, (L, L), 0)
    si = lax.broadcasted_iota(jnp.int32, (L, L), 1)
    tri_ones = (ti >= si).astype(F32)
    P = 2 * L
    def block_masks(width):
        tp = lax.broadcasted_iota(jnp.int32, (P, width), 0)
        sp = lax.broadcasted_iota(jnp.int32, (P, width), 1)
        sp = jnp.where(sp >= P, sp - P, sp)
        same_head = ((tp >= L) & (sp >= L)) | ((tp < L) & (sp < L))
        return same_head & (tp > sp), same_head & (tp >= sp), tp == sp

    bd_strict, _, diag = block_masks(P)
    _, bd_incl2, _ = block_masks(2 * P)
    eye = diag.astype(F32)
    head0 = lax.broadcasted_iota(jnp.int32, (L, P), 1) < n

    pairs = range(WKV_HEADS // 2)
    cols = [slice(p * P, (p + 1) * P) for p in pairs]

    def per_head(full, fn):
        out = []
        for c in cols:
            x = full[:, c]
            s0 = jnp.sum(jnp.where(head0, x, 0.0), axis=-1, keepdims=True)
            s1 = jnp.sum(jnp.where(head0, 0.0, x), axis=-1, keepdims=True)
            out.append(jnp.where(head0, fn(s0), fn(s1)))
        return jnp.concatenate(out, axis=1)

    r = r_ref[0].astype(F32)
    k = k_ref[0].astype(F32)
    v = v_ref[0].astype(F32)
    lw = lw_ref[0]
    a_lr = a_ref[0]
    kk = k * kk_ref[...]
    kk = kk * per_head(kk * kk, lambda s: lax.rsqrt(jnp.maximum(s, 1e-24)))
    k = k * (1.0 + (a_lr - 1.0) * ka_ref[...])
    b_s = kk * a_lr
    cum = _dot_f32(tri_ones, lw)
    c_end = cum[L - 1:L, :]
    c_inv = jnp.exp(-cum)
    c_out = jnp.exp(c_end - cum)
    a_t = -kk * jnp.exp(cum - lw)
    r_t = r * jnp.exp(cum)
    b_t = (b_s * c_inv).astype(BF16)
    k_t = (k * c_inv).astype(BF16)
    b_o = b_s * c_out
    k_o = k * c_out
    s_decay = jnp.exp(c_end)

    def split(x):
        return [jnp.where(head0, x, 0.0).astype(BF16), jnp.where(head0, 0.0, x).astype(BF16)]

    ar4 = [jnp.concatenate(split(a_t[:, c]) + split(r_t[:, c]), axis=0) for c in cols]
    bk4 = [jnp.concatenate([b_t[:, c]] * 2 + [k_t[:, c]] * 2, axis=0) for c in cols]
    bko4 = [jnp.concatenate(split(b_o[:, c]) + split(k_o[:, c]), axis=0) for c in cols]
    vbd = [jnp.concatenate(split(v[:, c]), axis=0) for c in cols]
    prod = [_dot_nt(ar4[p], bk4[p]) for p in pairs]
    n_ab = [jnp.where(bd_strict, z[:P, :P], 0.0) for z in prod]
    n_ak = [jnp.where(bd_strict, z[:P, P:], 0.0).astype(BF16) for z in prod]
    m_r = [jnp.where(bd_incl2, z[P:, :], 0.0).astype(BF16) for z in prod]

    inv = [eye + z for z in n_ab]
    pw = n_ab
    span = 1
    while 2 * span < L:
        pw_b = [z.astype(BF16) for z in pw]
        pw = [_dot(z, z) for z in pw_b]
        inv = [inv[p] + _dot(inv[p].astype(BF16), pw[p].astype(BF16)) for p in pairs]
        span *= 2

    s0 = [state_ref[p] for p in pairs]
    ah = [_dot_nt(ar4[p], s0[p].astype(BF16)) for p in pairs]
    rhs = [ah[p][:P] + _dot(n_ak[p], vbd[p]) for p in pairs]
    u = [_dot(inv[p].astype(BF16), rhs[p].astype(BF16)) for p in pairs]
    uv = [jnp.concatenate([u[p].astype(BF16), vbd[p]], axis=0) for p in pairs]
    y_bd = [ah[p][P:] + _dot(m_r[p], uv[p]) for p in pairs]
    y = jnp.concatenate([z[:L] + z[L:] for z in y_bd], axis=1)
    for p in pairs:
        s_inc = lax.dot_general(uv[p], bko4[p], (((0,), (0,)), ((), ())),
                                preferred_element_type=F32)
        state_ref[p] = s0[p] * s_decay[:, cols[p]] + s_inc

    mean = per_head(y, lambda s: s * (1.0 / n))
    yc = y - mean
    var = per_head(yc * yc, lambda s: s * (1.0 / n))
    yn = yc * lax.rsqrt(var + C_GN_EPS) * lnw_ref[...] + lnb_ref[...]
    bonus = per_head(r * k * rk_ref[...], lambda s: s) * v
    o_ref[0] = ((yn + bonus) * g_ref[0]).astype(o_ref.dtype)


def _wkv(r, k, v, lw, a, g, k_k, k_a, r_k, ln_w, ln_b, batch):
    m, width = r.shape
    t = m // batch
    hb = WKV_HEADS * C_HEAD_DIM
    assert t % WKV_L == 0 and width % hb == 0
    seq = lambda z: z.reshape(batch, t, width)
    par = lambda z: z.reshape(1, width).astype(F32)
    tok = pl.BlockSpec((1, WKV_L, hb), lambda bi, hg, c: (bi, c, hg))
    vec = pl.BlockSpec((1, hb), lambda bi, hg, c: (0, hg))
    out = pl.pallas_call(
        _wkv_kernel,
        grid=(batch, width // hb, t // WKV_L),
        in_specs=[tok] * 6 + [vec] * 5,
        out_specs=tok,
        out_shape=jax.ShapeDtypeStruct((batch, t, width), BF16),
        scratch_shapes=[pltpu.VMEM((WKV_HEADS // 2, 2 * C_HEAD_DIM, 2 * C_HEAD_DIM), F32)],
        compiler_params=_cparams("parallel", "parallel", "arbitrary"),
        name="wkv7_chunked",
    )(seq(r), seq(k), seq(v), seq(lw), seq(a), seq(g), par(k_k), par(k_a), par(r_k),
      par(ln_w), par(ln_b))
    return out.reshape(m, width)


def _rope_tables(t):
    half = ROPE_DIM // 2
    freqs = ROPE_THETA ** (-jnp.arange(half, dtype=F32) / half)
    ang = jnp.arange(t, dtype=F32)[:, None] * freqs[None, :]
    cos, sin = jnp.cos(ang), jnp.sin(ang)
    zeros = jnp.zeros((t, LANES - ROPE_DIM), F32)
    return (jnp.concatenate([cos, cos, zeros], axis=-1),
            jnp.concatenate([-sin, sin, zeros], axis=-1))


def _rope128(x, cos, sin):
    half = ROPE_DIM // 2
    swapped = pltpu.roll(x, LANES - half, axis=1) + pltpu.roll(x, half, axis=1)
    return x * cos + swapped * sin


def _mla_prep_kernel(cq_ref, ckv_ref, kpe_ref, qn_ref, kvn_ref, cos_ref, sin_ref,
                     cq_o, ckv_o, kpe_o):
    cq = cq_ref[...]
    cq_o[...] = (cq * lax.rsqrt(jnp.mean(cq * cq, axis=-1, keepdims=True) + EPS)
                 * qn_ref[...]).astype(cq_o.dtype)
    ckv = ckv_ref[...]
    ckv_o[...] = (ckv * lax.rsqrt(jnp.mean(ckv * ckv, axis=-1, keepdims=True) + EPS)
                  * kvn_ref[...]).astype(ckv_o.dtype)
    lane = lax.broadcasted_iota(jnp.int32, kpe_ref.shape, 1)
    kpe = jnp.where(lane < ROPE_DIM, kpe_ref[...], 0.0)
    kpe_o[...] = _rope128(kpe, cos_ref[...], sin_ref[...]).astype(kpe_o.dtype)


def _mla_prep(u_lo, q_norm, kv_norm, cos, sin, q_lora, kv_lora, col0, rows_per_batch):
    m = u_lo.shape[0]
    tb = _pick(rows_per_batch, (256, 128, 64, 32, 16, 8))
    per = rows_per_batch // tb
    row = lambda i: (i, 0)
    fixed = lambda i: (0, 0)
    return pl.pallas_call(
        _mla_prep_kernel,
        grid=(m // tb,),
        in_specs=[pl.BlockSpec((tb, q_lora), row),
                  pl.BlockSpec((tb, kv_lora), row),
                  pl.BlockSpec((tb, LANES), row),
                  pl.BlockSpec((1, q_lora), fixed),
                  pl.BlockSpec((1, kv_lora), fixed),
                  pl.BlockSpec((tb, LANES), lambda i: (i % per, 0)),
                  pl.BlockSpec((tb, LANES), lambda i: (i % per, 0))],
        out_specs=[pl.BlockSpec((tb, q_lora), row),
                   pl.BlockSpec((tb, kv_lora), row),
                   pl.BlockSpec((tb, LANES), row)],
        out_shape=[jax.ShapeDtypeStruct((m, q_lora), BF16),
                   jax.ShapeDtypeStruct((m, kv_lora), BF16),
                   jax.ShapeDtypeStruct((m, LANES), BF16)],
        compiler_params=_cparams("parallel"),
        name="mla_prep",
    )(u_lo[:, col0:col0 + q_lora], u_lo[:, col0 + q_lora:col0 + q_lora + kv_lora],
      u_lo[:, col0 + q_lora + kv_lora:col0 + q_lora + kv_lora + LANES],
      q_norm.reshape(1, q_lora), kv_norm.reshape(1, kv_lora), cos, sin)


MLA_TQ = 1024
MLA_TK = 512


def _chunk_causal_bias(tq, tk):
    r = np.arange(tq)[None, :, None]
    c = np.arange(tk)[None, None, :]
    d = np.arange(tq // tk)[:, None, None]
    return np.where(d * tk + c < (r // CHUNK + 1) * CHUNK, 0.0, NEG_INF).astype(np.float32)


def _mla_kernel(q_ref, kv_ref, kpe_ref, cos_ref, sin_ref, bias_ref, o_ref, qc_ref, m_ref, l_ref,
                acc_ref):
    i = pl.program_id(2)
    tq, tk = MLA_TQ, MLA_TK
    n_full = (i * tq) // tk
    qc_ref[:, :NOPE_DIM] = q_ref[0, :, :NOPE_DIM]
    qc_ref[:, NOPE_DIM:] = _rope128(q_ref[0, :, NOPE_DIM:].astype(F32), cos_ref[...],
                                    sin_ref[...]).astype(BF16)
    m_ref[...] = jnp.full(m_ref.shape, NEG_INF, F32)
    l_ref[...] = jnp.zeros(l_ref.shape, F32)
    acc_ref[...] = jnp.zeros(acc_ref.shape, F32)

    def tile(kt, diag):
        off = pl.multiple_of(kt * tk, tk)
        k_cat = jnp.concatenate([kv_ref[0, pl.ds(off, tk), :NOPE_DIM],
                                 kpe_ref[0, pl.ds(off, tk), :]], axis=1)
        v_t = kv_ref[0, pl.ds(off, tk), NOPE_DIM:]
        mask = None
        starts = list(range(0, tq, FLASH_ROWS))
        if diag is not None:
            starts = [r for r in starts if r >= diag * tk]
            mask = [bias_ref[diag, r:r + FLASH_ROWS, :] for r in starts]
        _flash_update(qc_ref, starts, k_cat, v_t, mask, m_ref, l_ref, acc_ref)

    _loop_by_two(n_full, lambda kt: tile(kt, None))
    for d in range(tq // tk):
        tile(n_full + d, d)
    o_ref[0] = (acc_ref[...] / l_ref[...]).astype(o_ref.dtype)


def _mla(q, kv, kpe, cos, sin, heads):
    b, t, _ = q.shape
    assert t % MLA_TQ == 0 and MLA_TQ % MLA_TK == 0
    hw = NOPE_DIM + LANES
    bias = _chunk_causal_bias(MLA_TQ, MLA_TK)
    return pl.pallas_call(
        _mla_kernel,
        grid=(b, heads, t // MLA_TQ),
        in_specs=[pl.BlockSpec((1, MLA_TQ, hw), lambda bi, h, i: (bi, i, h)),
                  pl.BlockSpec((1, t, NOPE_DIM + V_DIM), lambda bi, h, i: (bi, 0, h)),
                  pl.BlockSpec((1, t, LANES), lambda bi, h, i: (bi, 0, 0)),
                  pl.BlockSpec((MLA_TQ, LANES), lambda bi, h, i: (i, 0)),
                  pl.BlockSpec((MLA_TQ, LANES), lambda bi, h, i: (i, 0)),
                  pl.BlockSpec(bias.shape, lambda bi, h, i: (0, 0, 0))],
        out_specs=pl.BlockSpec((1, MLA_TQ, V_DIM), lambda bi, h, i: (bi, i, h)),
        out_shape=jax.ShapeDtypeStruct((b, t, heads * V_DIM), BF16),
        scratch_shapes=[pltpu.VMEM((MLA_TQ, hw), BF16),
                        pltpu.VMEM((MLA_TQ, LANES), F32),
                        pltpu.VMEM((MLA_TQ, LANES), F32),
                        pltpu.VMEM((MLA_TQ, V_DIM), F32)],
        compiler_params=_cparams("parallel", "parallel", "arbitrary"),
        name="mla_attention",
    )(q, kv, kpe, cos, sin, jnp.asarray(bias))


def _pad_cols(w, n):
    return jnp.pad(w, ((0, 0), (0, n - w.shape[1])))


def _even_mixer(h, x, gate, w_in, w_out, e, rel_bias, batch):
    m, d = h.shape
    t = m // batch
    heads = d // (2 * HEAD_DIM)
    idx_heads = d // 128
    hw = heads * HEAD_DIM
    o_kb = 4 * hw
    o_qi = o_kb + 2 * HEAD_DIM
    o_ki = o_qi + idx_heads * IDX_DIM
    q_factor = LOG2_E * HEAD_DIM ** -0.5
    w_a = jnp.concatenate([w_in[:, :hw] * q_factor, w_in[:, hw:3 * hw]], axis=1)
    u_a = _matmul(h, w_a.astype(BF16), out_dtype=BF16)
    w_qb = w_in[:, 3 * hw:4 * hw] * q_factor
    u_q = _matmul(h, jnp.concatenate([w_qb, w_in[:, o_qi:o_ki]], axis=1).astype(BF16),
                  out_dtype=BF16)
    small = jnp.concatenate([w_in[:, o_kb:o_qi], w_in[:, o_ki:]], axis=1)
    u_s = _matmul(h, _pad_cols(small, 3 * LANES).astype(BF16), out_dtype=F32, tn=3 * LANES)
    u_s = u_s.reshape(batch, t, 3 * LANES)
    kb = u_s[..., :HEAD_DIM].astype(BF16)
    vb = u_s[..., HEAD_DIM:2 * HEAD_DIM].astype(BF16)
    ki = u_s[..., 2 * HEAD_DIM:2 * HEAD_DIM + IDX_DIM].astype(BF16)
    wi = u_s[..., 2 * HEAD_DIM + IDX_DIM:2 * HEAD_DIM + IDX_DIM + idx_heads]
    ya = _attn_a(u_a.reshape(batch, t, 3 * hw), _band_bias_tiles(rel_bias), heads,
                 0, heads, 2 * heads)
    yb = _dsa(u_q.reshape(batch, t, 2 * hw), wi, kb, vb, ki, heads, idx_heads)
    return _matmul((ya.reshape(m, hw), yb.reshape(m, hw)), w_out, w_layer=e, out_dtype=F32,
                   epilogue="resid", x=x, gate=gate, rows_per_batch=t)


def _odd_mixer(h, x, gate, w_in, w_out, o, c_mu, c_w0, c_w_up, c_a0, c_a_up, c_g_up, c_k_k, c_k_a,
               c_r_k, c_ln_w, c_ln_b, d_q_norm, d_kv_norm, d_w_uq, d_w_ukv, batch):
    m, d = h.shape
    t = m // batch
    width = c_w0.shape[0]
    heads = d // (2 * HEAD_DIM)
    q_lora = d_q_norm.shape[0]
    kv_lora = d_kv_norm.shape[0]
    lo_w = DECAY_LORA + ICLR_LORA + GATE_LORA
    w_in = w_in.astype(BF16)
    u_rkv = _matmul(h, w_in[:, :3 * width], out_dtype=BF16)
    rest = w_in[:, 3 * width:]
    rest_cols = -(-(rest.shape[1] + LANES - ROPE_DIM) // 512) * 512
    u_lo = _matmul(h, _pad_cols(rest, rest_cols), out_dtype=F32)

    r, k, v, lw, a, g = _rwkv_prep(u_rkv, u_lo, c_mu, c_w0, c_w_up, c_a0, c_a_up, c_g_up, t)
    yc = _wkv(r, k, v, lw, a, g, c_k_k, c_k_a, c_r_k, c_ln_w, c_ln_b, batch)

    cos, sin = _rope_tables(t)
    cq, ckv, kpe = _mla_prep(u_lo, d_q_norm, d_kv_norm, cos, sin, q_lora, kv_lora, lo_w, t)
    w_uq = d_w_uq.reshape(q_lora, heads, NOPE_DIM + ROPE_DIM)
    w_uq = jnp.pad(w_uq, ((0, 0), (0, 0), (0, LANES - ROPE_DIM))).reshape(q_lora, -1)
    w_uq = w_uq * (LOG2_E * (NOPE_DIM + ROPE_DIM) ** -0.5)
    q = _matmul(cq, w_uq.astype(BF16), out_dtype=BF16)
    kv = _matmul(ckv, d_w_ukv, w_layer=o, out_dtype=BF16)
    yd = _mla(q.reshape(batch, t, -1), kv.reshape(batch, t, -1), kpe.reshape(batch, t, LANES),
              cos, sin, heads)
    return _matmul((yc, yd.reshape(m, -1)), w_out, w_layer=o, out_dtype=F32,
                   epilogue="resid", x=x, gate=gate, rows_per_batch=t)


def kernel(x, c, ada_w, ada_table, norm_mix, norm_ffn, norm_final, ffn_w1, ffn_w2, ev_w_in,
           ev_w_out, a_rel_bias, od_w_in, od_w_out, c_mu, c_w0, c_w_up, c_a0, c_a_up, c_g_up,
           c_k_k, c_k_a, c_r_k, c_ln_w, c_ln_b, d_q_norm, d_kv_norm, d_w_uq, d_w_ukv):
    batch, t, d = x.shape
    depth = ada_table.shape[0]
    m = batch * t
    silu_c = (c * _sigmoid(c)).astype(BF16)
    silu_c = jnp.pad(silu_c, ((0, 8 - batch % 8 if batch % 8 else 0), (0, 0)))
    ada = _matmul(silu_c, ada_w, out_dtype=F32)[:batch].reshape(batch, 6, d)
    x = x.reshape(m, d)
    ev_w_out = ev_w_out.astype(BF16)
    od_w_out = od_w_out.astype(BF16)
    d_w_ukv = d_w_ukv.astype(BF16)
    for layer in range(depth):
        mod = ada + ada_table[layer]
        shift_m, scale_m, gate_m, shift_f, scale_f, gate_f = (mod[:, i, :] for i in range(6))
        h = _norm_mod(x, norm_mix[layer], scale_m, shift_m, t)
        if layer % 2 == 0:
            e = layer // 2
            x = _even_mixer(h, x, gate_m, ev_w_in[e], ev_w_out, e, a_rel_bias[e], batch)
        else:
            o = layer // 2
            x = _odd_mixer(h, x, gate_m, od_w_in[o], od_w_out, o, c_mu[o], c_w0[o], c_w_up[o],
                           c_a0[o], c_a_up[o], c_g_up[o], c_k_k[o], c_k_a[o], c_r_k[o],
                           c_ln_w[o], c_ln_b[o], d_q_norm[o], d_kv_norm[o], d_w_uq[o],
                           d_w_ukv, batch)
        h = _norm_mod(x, norm_ffn[layer], scale_f, shift_f, t)
        h1 = _matmul(h, ffn_w1, w_layer=layer, out_dtype=BF16, epilogue="relu2")
        x = _matmul(h1, ffn_w2, w_layer=layer, out_dtype=F32, epilogue="resid", x=x,
                    gate=gate_f, rows_per_batch=t)
    return _norm_plain(x, norm_final, F32).reshape(batch, t, d)
```

```python
import functools

import numpy as np
import jax
import jax.numpy as jnp
from jax import lax
from jax.experimental import pallas as pl
from jax.experimental.pallas import tpu as pltpu

F32 = jnp.float32
BF16 = jnp.bfloat16

CHUNK = 64
Q_BLOCK = 128
HEAD_DIM = 128
EPS = 1e-6
NEG_INF = -1e30
A_LEFT_CHUNKS = 8
A_MAX_REL = 256
IDX_DIM = 64
TOPK_MAX = 256
C_HEAD_DIM = 64
DECAY_LORA = 128
ICLR_LORA = 128
GATE_LORA = 256
C_GN_EPS = 64e-5
NOPE_DIM = 128
ROPE_DIM = 64
V_DIM = 128
ROPE_THETA = 10000.0

V7X_VMEM_LIMIT_BYTES = 56 * 1024 * 1024
LANES = 128
INT32_MIN = -(2 ** 31)
LOG2_E = 1.4426950408889634


def _cparams(*sem):
    return pltpu.CompilerParams(dimension_semantics=sem, vmem_limit_bytes=V7X_VMEM_LIMIT_BYTES)


def _dot_nt(a, b):
    return lax.dot_general(a, b, (((1,), (1,)), ((), ())), preferred_element_type=F32)


def _dot(a, b):
    return jnp.dot(a, b, preferred_element_type=F32)


def _dot_f32(a, b):
    return jnp.dot(a, b, preferred_element_type=F32, precision=lax.Precision.HIGHEST)


FLASH_ROWS = 128


def _flash_update(q_ref, row_starts, k_t, v_t, mask, m_ref, l_ref, acc_ref, sum_on_mxu=False):
    tk = k_t.shape[0]
    reps = tk // LANES
    rows = [slice(r, r + FLASH_ROWS) for r in row_starts]
    s = [_dot_nt(q_ref[r, :], k_t) for r in rows]
    if mask is not None:
        masks = mask if isinstance(mask, list) else [mask] * len(rows)
        s = [x + mk for x, mk in zip(s, masks)]
    m_old = [m_ref[r, :] for r in rows]
    m_new = [jnp.maximum(mo, jnp.max(x, axis=-1, keepdims=True)) for mo, x in zip(m_old, s)]
    p = [jnp.exp2(x - jnp.concatenate([mn] * reps, axis=1)) for x, mn in zip(s, m_new)]
    alpha = [jnp.exp2(mo - mn) for mo, mn in zip(m_old, m_new)]
    if sum_on_mxu:
        v_t = jnp.concatenate([v_t, jnp.ones((tk, LANES), v_t.dtype)], axis=1)
    pv = [_dot(x.astype(v_t.dtype), v_t) for x in p]
    dv = acc_ref.shape[1]
    for r, mn, al, x, y in zip(rows, m_new, alpha, p, pv):
        m_ref[r, :] = mn
        row_sum = y[:, dv:] if sum_on_mxu else jnp.sum(x, axis=-1, keepdims=True)
        l_ref[r, :] = al * l_ref[r, :] + row_sum
        acc_ref[r, :] = al * acc_ref[r, :] + y[:, :dv]


def _loop_by_two(n, body):
    def pair(j, carry):
        body(2 * j)
        body(2 * j + 1)
        return carry

    lax.fori_loop(0, n // 2, pair, 0)

    @pl.when(n % 2 == 1)
    def _():
        body(n - 1)


def _pick(n, prefs):
    for p in prefs:
        if n % p == 0:
            return p
    return n


MM_ACC_CHUNK = 256

def _mm_kernel(*refs, nk, n_lhs, epilogue):
    a_refs = refs[:n_lhs]
    w_ref = refs[n_lhs]
    if epilogue == "resid":
        x_ref, g_ref, o_ref = refs[n_lhs + 1:n_lhs + 4]
        scratch = refs[n_lhs + 4:]
    else:
        o_ref = refs[n_lhs + 1]
        scratch = refs[n_lhs + 2:]

    tn = o_ref.shape[1]

    def finish(acc, cols):
        if epilogue == "relu2":
            r = jnp.maximum(acc, 0.0)
            acc = r * r
        elif epilogue == "resid":
            acc = x_ref[:, cols] + g_ref[0, :, cols] * acc
        o_ref[:, cols] = acc.astype(o_ref.dtype)

    def product(cols):
        part, row = None, 0
        for a_ref in a_refs:
            kw = a_ref.shape[1]
            term = _dot(a_ref[...], w_ref[row:row + kw, cols].astype(BF16))
            part = term if part is None else part + term
            row += kw
        return part

    if nk == 1:
        finish(product(slice(0, tn)), slice(0, tn))
        return

    acc_ref = scratch[0]
    k = pl.program_id(2)
    chunks = [slice(c, c + MM_ACC_CHUNK) for c in range(0, tn, MM_ACC_CHUNK)]

    @pl.when(k == 0)
    def _():
        for cols in chunks:
            acc_ref[:, cols] = product(cols)

    @pl.when((k > 0) & (k < nk - 1))
    def _():
        for cols in chunks:
            acc_ref[:, cols] += product(cols)

    @pl.when(k == nk - 1)
    def _():
        for cols in chunks:
            finish(acc_ref[:, cols] + product(cols), cols)


def _matmul(a, w, *, out_dtype, w_layer=None, epilogue="none", x=None, gate=None,
            rows_per_batch=None, tm=None, tn=None, tk=None):
    lhs = a if isinstance(a, tuple) else (a,)
    m = lhs[0].shape[0]
    assert (w.ndim == 3) == (w_layer is not None)
    kd, n = w.shape[-2:]
    assert sum(z.shape[1] for z in lhs) == kd
    tm = tm or _pick(m, (1024, 512, 256, 128, 64, 32, 16, 8))
    tk = tk or (kd if kd <= 4096 else _pick(kd, (2048, 1024, 512)))
    tn = tn or _pick(n, (1024, 512, 384, 256, 128) if tk < kd else (512, 384, 256, 128))
    nk = kd // tk
    assert m % tm == 0 and n % tn == 0 and kd % tk == 0 and (nk == 1 or len(lhs) == 1)
    in_specs = [pl.BlockSpec((tm, z.shape[1] if nk == 1 else tk), lambda i, j, k: (i, k))
                for z in lhs]
    if w_layer is None:
        in_specs.append(pl.BlockSpec((tk, tn), lambda i, j, k: (k, j)))
    else:
        in_specs.append(pl.BlockSpec((None, tk, tn), lambda i, j, k: (w_layer, k, j)))
    args = [*lhs, w]
    if epilogue == "resid":
        assert rows_per_batch % tm == 0
        per = rows_per_batch // tm
        in_specs += [pl.BlockSpec((tm, tn), lambda i, j, k: (i, j)),
                     pl.BlockSpec((1, 1, tn), lambda i, j, k: (i // per, 0, j))]
        args += [x, gate.reshape(gate.shape[0], 1, n)]
    return pl.pallas_call(
        functools.partial(_mm_kernel, nk=nk, n_lhs=len(lhs), epilogue=epilogue),
        grid=(m // tm, n // tn, nk),
        in_specs=in_specs,
        out_specs=pl.BlockSpec((tm, tn), lambda i, j, k: (i, j)),
        out_shape=jax.ShapeDtypeStruct((m, n), out_dtype),
        scratch_shapes=[pltpu.VMEM((tm, tn), F32)] if nk > 1 else [],
        compiler_params=_cparams("parallel", "parallel", "arbitrary"),
        name="matmul_" + epilogue,
    )(*args)


def _norm_mod_kernel(x_ref, g_ref, sc_ref, sh_ref, o_ref):
    x = x_ref[...]
    ms = jnp.mean(x * x, axis=-1, keepdims=True)
    y = x * lax.rsqrt(ms + EPS) * g_ref[...]
    o_ref[...] = (y * (1.0 + sc_ref[0]) + sh_ref[0]).astype(o_ref.dtype)


def _norm_plain_kernel(x_ref, g_ref, o_ref):
    x = x_ref[...]
    ms = jnp.mean(x * x, axis=-1, keepdims=True)
    o_ref[...] = (x * lax.rsqrt(ms + EPS) * g_ref[...]).astype(o_ref.dtype)


def _norm_mod(x, g, scale, shift, rows_per_batch, out_dtype=BF16):
    m, d = x.shape
    tr = _pick(rows_per_batch, (256, 128, 64, 32, 16, 8))
    per = rows_per_batch // tr
    b = scale.shape[0]
    return pl.pallas_call(
        _norm_mod_kernel,
        grid=(m // tr,),
        in_specs=[pl.BlockSpec((tr, d), lambda i: (i, 0)),
                  pl.BlockSpec((1, d), lambda i: (0, 0)),
                  pl.BlockSpec((1, 1, d), lambda i: (i // per, 0, 0)),
                  pl.BlockSpec((1, 1, d), lambda i: (i // per, 0, 0))],
        out_specs=pl.BlockSpec((tr, d), lambda i: (i, 0)),
        out_shape=jax.ShapeDtypeStruct((m, d), out_dtype),
        compiler_params=_cparams("parallel"),
        name="norm_mod",
    )(x, g.reshape(1, d), scale.reshape(b, 1, d), shift.reshape(b, 1, d))


def _norm_plain(x, g, out_dtype):
    m, d = x.shape
    tr = _pick(m, (256, 128, 64, 32, 16, 8))
    return pl.pallas_call(
        _norm_plain_kernel,
        grid=(m // tr,),
        in_specs=[pl.BlockSpec((tr, d), lambda i: (i, 0)),
                  pl.BlockSpec((1, d), lambda i: (0, 0))],
        out_specs=pl.BlockSpec((tr, d), lambda i: (i, 0)),
        out_shape=jax.ShapeDtypeStruct((m, d), out_dtype),
        compiler_params=_cparams("parallel"),
        name="norm_plain",
    )(x, g.reshape(1, d))


A_PAD = A_LEFT_CHUNKS * CHUNK
A_WIN = A_PAD + Q_BLOCK
A_VARIANTS = A_PAD // Q_BLOCK + 1


def _band_bias_tiles(rel_bias):
    width = A_WIN + Q_BLOCK
    tiles = []
    for v in range(A_VARIANTS):
        rel = np.clip(v * Q_BLOCK + Q_BLOCK - 1 - np.arange(width), -A_MAX_REL, A_MAX_REL)
        diag = jnp.pad(rel_bias.astype(F32)[:, rel + A_MAX_REL] * LOG2_E, ((0, 0), (0, 1)))
        skew = jnp.tile(diag, (1, Q_BLOCK))[:, :Q_BLOCK * width].reshape(-1, Q_BLOCK, width)
        tiles.append(skew[:, :, Q_BLOCK - 1:Q_BLOCK - 1 + A_WIN])
    return jnp.stack(tiles, axis=1)


A_HEADS_PER_STEP = 8


def _attn_a_kernel(q_ref, k_ref, v_ref, b_ref, o_ref):
    i = pl.program_id(2)
    ks = pl.multiple_of(jnp.maximum(i * Q_BLOCK - A_PAD, 0), Q_BLOCK)
    qpos = i * Q_BLOCK + lax.broadcasted_iota(jnp.int32, (Q_BLOCK, A_WIN), 0)
    kpos = ks + lax.broadcasted_iota(jnp.int32, (Q_BLOCK, A_WIN), 1)
    chunk_start = (qpos // CHUNK) * CHUNK
    valid = (kpos >= chunk_start - A_PAD) & (kpos < chunk_start + CHUNK)
    cols = [slice(h * HEAD_DIM, (h + 1) * HEAD_DIM) for h in range(A_HEADS_PER_STEP)]
    vs = [v_ref[0, pl.ds(ks, A_WIN), c] for c in cols]
    s = [_dot_nt(q_ref[0, :, c], k_ref[0, pl.ds(ks, A_WIN), c]) + b_ref[h, 0]
         for h, c in enumerate(cols)]
    s = [jnp.where(valid, x, NEG_INF) for x in s]
    p = [jnp.exp2(x - jnp.max(x, axis=-1, keepdims=True)) for x in s]
    l = [jnp.sum(x, axis=-1, keepdims=True) for x in p]
    o = [_dot(x.astype(v.dtype), v) for x, v in zip(p, vs)]
    for c, x, y in zip(cols, o, l):
        o_ref[0, :, c] = (x / y).astype(o_ref.dtype)


def _attn_a(u, bias_tiles, heads, q_col, k_col, v_col):
    b, t, _ = u.shape
    hb = A_HEADS_PER_STEP
    assert t % Q_BLOCK == 0 and t >= A_WIN
    assert heads % hb == 0 and q_col % hb == 0 and k_col % hb == 0 and v_col % hb == 0
    nq = t // Q_BLOCK
    return pl.pallas_call(
        _attn_a_kernel,
        grid=(b, heads // hb, nq),
        in_specs=[pl.BlockSpec((1, Q_BLOCK, hb * HEAD_DIM),
                               lambda bi, h, i: (bi, i, q_col // hb + h)),
                  pl.BlockSpec((1, t, hb * HEAD_DIM), lambda bi, h, i: (bi, 0, k_col // hb + h)),
                  pl.BlockSpec((1, t, hb * HEAD_DIM), lambda bi, h, i: (bi, 0, v_col // hb + h)),
                  pl.BlockSpec((hb, 1, Q_BLOCK, A_WIN),
                               lambda bi, h, i: (h, jnp.minimum(i, A_VARIANTS - 1), 0, 0))],
        out_specs=pl.BlockSpec((1, Q_BLOCK, hb * HEAD_DIM), lambda bi, h, i: (bi, i, h)),
        out_shape=jax.ShapeDtypeStruct((b, t, heads * HEAD_DIM), BF16),
        compiler_params=_cparams("parallel", "parallel", "arbitrary"),
        name="band_attention",
    )(u, u, u, bias_tiles)


DSA_TK = 512


def _sortable_key(x):
    bits = lax.bitcast_convert_type(x, jnp.int32)
    return jnp.where(bits < 0, bits ^ jnp.int32(0x7FFFFFFF), bits)


DSA_GROUP = 8


def _dsa_kernel(qb_ref, qi_ref, wi_ref, k_ref, v_ref, ki_ref, o_ref,
                key_ref, qs_ref, m_ref, l_ref, acc_ref, *, heads, idx_heads, topk):
    i = pl.program_id(1)
    tk = DSA_TK
    n_kt = (i * Q_BLOCK + Q_BLOCK + tk - 1) // tk
    row = lax.broadcasted_iota(jnp.int32, (Q_BLOCK, 1), 0)
    limit = ((i * Q_BLOCK + row) // CHUNK + 1) * CHUNK

    def lane_fold(x):
        out = x[:, :LANES]
        for c in range(1, tk // LANES):
            out = out + x[:, c * LANES:(c + 1) * LANES]
        return out

    def score_tile(kt, carry):
        off = pl.multiple_of(kt * tk, tk)
        ki_t = ki_ref[0, pl.ds(off, tk), :]
        acc = jnp.zeros((Q_BLOCK, tk), F32)
        for h in range(idx_heads):
            logits = _dot_nt(qi_ref[0, :, h * IDX_DIM:(h + 1) * IDX_DIM], ki_t)
            acc = acc + wi_ref[0, :, h:h + 1] * jnp.maximum(logits, 0.0)
        kpos = off + lax.broadcasted_iota(jnp.int32, (Q_BLOCK, tk), 1)
        key_ref[:, pl.ds(off, tk)] = _sortable_key(jnp.where(kpos < limit, acc, NEG_INF))
        return carry

    lax.fori_loop(0, n_kt, score_tile, 0)

    def bit_step(bi, t):
        cand = t + lax.shift_left(jnp.int32(1), 31 - bi)

        def count_tile(kt, c):
            off = pl.multiple_of(kt * tk, tk)
            return c + lane_fold(jnp.where(key_ref[:, pl.ds(off, tk)] >= cand, 1.0, 0.0))

        part = lax.fori_loop(0, n_kt, count_tile, jnp.zeros((Q_BLOCK, LANES), F32))
        cnt = jnp.sum(part, axis=-1, keepdims=True)
        return jnp.where(cnt >= float(topk), cand, t)

    thr = lax.fori_loop(0, 32, bit_step, jnp.full((Q_BLOCK, 1), INT32_MIN, jnp.int32))

    def tie_tile(kt, carry):
        gt, eq = carry
        off = pl.multiple_of(kt * tk, tk)
        keys = key_ref[:, pl.ds(off, tk)]
        kpos = off + lax.broadcasted_iota(jnp.int32, (Q_BLOCK, tk), 1)
        gt = gt + lane_fold(jnp.where(keys > thr, 1.0, 0.0))
        eq = eq + lane_fold(jnp.where((keys == thr) & (kpos < limit), 1.0, 0.0))
        return gt, eq

    zeros = jnp.zeros((Q_BLOCK, LANES), F32)
    gt, eq = lax.fori_loop(0, n_kt, tie_tile, (zeros, zeros))
    need = float(topk) - jnp.sum(gt, axis=-1, keepdims=True)
    surplus = jnp.max(jnp.sum(eq, axis=-1, keepdims=True) - need)

    @pl.when(surplus > 0.0)
    def _():
        ki_ = lax.broadcasted_iota(jnp.int32, (tk, tk), 0)
        kj_ = lax.broadcasted_iota(jnp.int32, (tk, tk), 1)
        before = (ki_ < kj_).astype(BF16)

        def drop_tile(kt, seen):
            off = pl.multiple_of(kt * tk, tk)
            keys = key_ref[:, pl.ds(off, tk)]
            kpos = off + lax.broadcasted_iota(jnp.int32, (Q_BLOCK, tk), 1)
            is_eq = (keys == thr) & (kpos < limit)
            eq_f = jnp.where(is_eq, 1.0, 0.0)
            rank = seen + _dot(eq_f.astype(BF16), before)
            key_ref[:, pl.ds(off, tk)] = jnp.where(is_eq & (rank >= need), keys - 1, keys)
            return seen + jnp.sum(eq_f, axis=-1, keepdims=True)

        lax.fori_loop(0, n_kt, drop_tile, jnp.zeros((Q_BLOCK, 1), F32))

    for h in range(heads):
        qs_ref[h * Q_BLOCK:(h + 1) * Q_BLOCK, :] = qb_ref[0, :, h * HEAD_DIM:(h + 1) * HEAD_DIM]
    m_ref[...] = jnp.full(m_ref.shape, NEG_INF, F32)
    l_ref[...] = jnp.zeros(l_ref.shape, F32)
    acc_ref[...] = jnp.zeros(acc_ref.shape, F32)

    def attend_tile(kt, carry):
        off = pl.multiple_of(kt * tk, tk)
        k_t = k_ref[0, pl.ds(off, tk), :]
        v_t = v_ref[0, pl.ds(off, tk), :]
        kpos = off + lax.broadcasted_iota(jnp.int32, (Q_BLOCK, tk), 1)
        sel = (key_ref[:, pl.ds(off, tk)] >= thr) & (kpos < limit)
        sel = jnp.where(sel, 0.0, NEG_INF)
        for g in range(0, heads, DSA_GROUP):
            _flash_update(qs_ref, [h * Q_BLOCK for h in range(g, g + DSA_GROUP)], k_t, v_t, sel,
                          m_ref, l_ref, acc_ref, sum_on_mxu=True)
        return carry

    _loop_by_two(n_kt, lambda kt: attend_tile(kt, 0))
    for h in range(heads):
        rows = slice(h * Q_BLOCK, (h + 1) * Q_BLOCK)
        o_ref[0, :, h * HEAD_DIM:(h + 1) * HEAD_DIM] = (
            acc_ref[rows, :] / l_ref[rows, :]).astype(o_ref.dtype)


def _dsa(u_q, wi, kb, vb, ki, heads, idx_heads):
    b, t, _ = u_q.shape
    assert heads * HEAD_DIM == idx_heads * IDX_DIM and heads % DSA_GROUP == 0
    qw = heads * HEAD_DIM
    topk = min(TOPK_MAX, t // 4)
    assert t % DSA_TK == 0 and topk <= DSA_TK
    nq = t // Q_BLOCK
    return pl.pallas_call(
        functools.partial(_dsa_kernel, heads=heads, idx_heads=idx_heads, topk=topk),
        grid=(b, nq),
        in_specs=[pl.BlockSpec((1, Q_BLOCK, qw), lambda bi, i: (bi, i, 0)),
                  pl.BlockSpec((1, Q_BLOCK, qw), lambda bi, i: (bi, i, 1)),
                  pl.BlockSpec((1, Q_BLOCK, idx_heads), lambda bi, i: (bi, i, 0)),
                  pl.BlockSpec((1, t, HEAD_DIM), lambda bi, i: (bi, 0, 0)),
                  pl.BlockSpec((1, t, HEAD_DIM), lambda bi, i: (bi, 0, 0)),
                  pl.BlockSpec((1, t, IDX_DIM), lambda bi, i: (bi, 0, 0))],
        out_specs=pl.BlockSpec((1, Q_BLOCK, qw), lambda bi, i: (bi, i, 0)),
        out_shape=jax.ShapeDtypeStruct((b, t, qw), BF16),
        scratch_shapes=[pltpu.VMEM((Q_BLOCK, t), jnp.int32),
                        pltpu.VMEM((heads * Q_BLOCK, HEAD_DIM), BF16),
                        pltpu.VMEM((heads * Q_BLOCK, LANES), F32),
                        pltpu.VMEM((heads * Q_BLOCK, LANES), F32),
                        pltpu.VMEM((heads * Q_BLOCK, HEAD_DIM), F32)],
        compiler_params=_cparams("parallel", "arbitrary"),
        name="dsa_attention",
    )(u_q, u_q, wi, kb, vb, ki)


def _shifted(x, prev_row, first):
    rolled = pltpu.roll(x, 1, axis=0)
    head = jnp.where(first, jnp.zeros_like(prev_row), prev_row)
    rid = lax.broadcasted_iota(jnp.int32, x.shape, 0)
    return jnp.where(rid == 0, head, rolled)


def _sigmoid(x):
    return 1.0 / (1.0 + jnp.exp(-x))


PREV_ROWS = 16


def _rwkv_prep_kernel(rkv_ref, rkv_prev_ref, lo_ref, lo_prev_ref, mu_rkv_ref, mu_lo_ref,
                      w0_ref, a0_ref, wup_ref, aup_ref, gup_ref,
                      r_ref, k_ref, v_ref, lw_ref, a_ref, g_ref, *, blocks_per_seq, width):
    first = (pl.program_id(0) % blocks_per_seq) == 0
    x = rkv_ref[...].astype(F32)
    prev = _shifted(x, rkv_prev_ref[PREV_ROWS - 1:PREV_ROWS, :].astype(F32), first)
    x = x + mu_rkv_ref[...] * (prev - x)
    r_ref[...] = x[:, :width].astype(r_ref.dtype)
    k_ref[...] = x[:, width:2 * width].astype(k_ref.dtype)
    v_ref[...] = x[:, 2 * width:].astype(v_ref.dtype)

    lo = lo_ref[...]
    lo_prev = _shifted(lo, lo_prev_ref[PREV_ROWS - 1:PREV_ROWS, :], first)
    lo = lo + mu_lo_ref[...] * (lo_prev - lo)
    xw = lo[:, :DECAY_LORA]
    xa = lo[:, DECAY_LORA:DECAY_LORA + ICLR_LORA]
    xg = lo[:, DECAY_LORA + ICLR_LORA:]
    u = w0_ref[...] + _dot(jnp.tanh(xw).astype(BF16), wup_ref[...])
    lw_ref[...] = -float(np.exp(-0.5)) * _sigmoid(u)
    a_ref[...] = _sigmoid(a0_ref[...] + _dot(xa.astype(BF16), aup_ref[...]))
    g_ref[...] = _dot(_sigmoid(xg).astype(BF16), gup_ref[...]).astype(g_ref.dtype)


def _rwkv_prep(u_rkv, u_lo, mu, w0, w_up, a0, a_up, g_up, rows_per_batch):
    m, w3 = u_rkv.shape
    width = w3 // 3
    lo_w = DECAY_LORA + ICLR_LORA + GATE_LORA
    tb = 128
    assert rows_per_batch % tb == 0
    blocks_per_seq = rows_per_batch // tb
    sub = tb // PREV_ROWS
    prev_map = lambda i: (jnp.maximum(i * sub - 1, 0), 0)
    row = lambda i: (i, 0)
    fixed = lambda i: (0, 0)
    wide = jax.ShapeDtypeStruct((m, width), F32)
    narrow = jax.ShapeDtypeStruct((m, width), BF16)
    return pl.pallas_call(
        functools.partial(_rwkv_prep_kernel, blocks_per_seq=blocks_per_seq, width=width),
        grid=(m // tb,),
        in_specs=[pl.BlockSpec((tb, w3), row),
                  pl.BlockSpec((PREV_ROWS, w3), prev_map),
                  pl.BlockSpec((tb, lo_w), row),
                  pl.BlockSpec((PREV_ROWS, lo_w), prev_map),
                  pl.BlockSpec((1, w3), fixed),
                  pl.BlockSpec((1, lo_w), fixed),
                  pl.BlockSpec((1, width), fixed),
                  pl.BlockSpec((1, width), fixed),
                  pl.BlockSpec((DECAY_LORA, width), fixed),
                  pl.BlockSpec((ICLR_LORA, width), fixed),
                  pl.BlockSpec((GATE_LORA, width), fixed)],
        out_specs=[pl.BlockSpec((tb, width), row)] * 6,
        out_shape=[narrow, narrow, narrow, wide, wide, narrow],
        compiler_params=_cparams("parallel"),
        name="rwkv_prep",
    )(u_rkv, u_rkv, u_lo, u_lo, mu[:w3].reshape(1, w3), mu[w3:].reshape(1, lo_w),
      w0.reshape(1, width), a0.reshape(1, width), w_up.astype(BF16), a_up.astype(BF16),
      g_up.astype(BF16))


WKV_L = 64
WKV_HEADS = 32


def _wkv_kernel(r_ref, k_ref, v_ref, lw_ref, a_ref, g_ref, kk_ref, ka_ref, rk_ref, lnw_ref,
                lnb_ref, o_ref, state_ref):
    L = WKV_L
    n = C_HEAD_DIM

    @pl.when(pl.program_id(2) == 0)
    def _():
        state_ref[...] = jnp.zeros(state_ref.shape, F32)

    ti = lax.broadcasted_iota(jnp.int32, (L, L), 0)
    si = lax.broadcasted_iota(jnp.int32, (L, L), 1)
    tri_ones = (ti >= si).astype(F32)
    P = 2 * L
    def block_masks(width):
        tp = lax.broadcasted_iota(jnp.int32, (P, width), 0)
        sp = lax.broadcasted_iota(jnp.int32, (P, width), 1)
        sp = jnp.where(sp >= P, sp - P, sp)
        same_head = ((tp >= L) & (sp >= L)) | ((tp < L) & (sp < L))
        return same_head & (tp > sp), same_head & (tp >= sp), tp == sp

    bd_strict, _, diag = block_masks(P)
    _, bd_incl2, _ = block_masks(2 * P)
    eye = diag.astype(F32)
    head0 = lax.broadcasted_iota(jnp.int32, (L, P), 1) < n

    pairs = range(WKV_HEADS // 2)
    cols = [slice(p * P, (p + 1) * P) for p in pairs]

    def per_head(full, fn):
        out = []
        for c in cols:
            x = full[:, c]
            s0 = jnp.sum(jnp.where(head0, x, 0.0), axis=-1, keepdims=True)
            s1 = jnp.sum(jnp.where(head0, 0.0, x), axis=-1, keepdims=True)
            out.append(jnp.where(head0, fn(s0), fn(s1)))
        return jnp.concatenate(out, axis=1)

    r = r_ref[0].astype(F32)
    k = k_ref[0].astype(F32)
    v = v_ref[0].astype(F32)
    lw = lw_ref[0]
    a_lr = a_ref[0]
    kk = k * kk_ref[...]
    kk = kk * per_head(kk * kk, lambda s: lax.rsqrt(jnp.maximum(s, 1e-24)))
    k = k * (1.0 + (a_lr - 1.0) * ka_ref[...])
    b_s = kk * a_lr
    cum = _dot_f32(tri_ones, lw)
    c_end = cum[L - 1:L, :]
    c_inv = jnp.exp(-cum)
    c_out = jnp.exp(c_end - cum)
    a_t = -kk * jnp.exp(cum - lw)
    r_t = r * jnp.exp(cum)
    b_t = (b_s * c_inv).astype(BF16)
    k_t = (k * c_inv).astype(BF16)
    b_o = b_s * c_out
    k_o = k * c_out
    s_decay = jnp.exp(c_end)

    def split(x):
        return [jnp.where(head0, x, 0.0).astype(BF16), jnp.where(head0, 0.0, x).astype(BF16)]

    ar4 = [jnp.concatenate(split(a_t[:, c]) + split(r_t[:, c]), axis=0) for c in cols]
    bk4 = [jnp.concatenate([b_t[:, c]] * 2 + [k_t[:, c]] * 2, axis=0) for c in cols]
    bko4 = [jnp.concatenate(split(b_o[:, c]) + split(k_o[:, c]), axis=0) for c in cols]
    vbd = [jnp.concatenate(split(v[:, c]), axis=0) for c in cols]
    prod = [_dot_nt(ar4[p], bk4[p]) for p in pairs]
    n_ab = [jnp.where(bd_strict, z[:P, :P], 0.0) for z in prod]
    n_ak = [jnp.where(bd_strict, z[:P, P:], 0.0).astype(BF16) for z in prod]
    m_r = [jnp.where(bd_incl2, z[P:, :], 0.0).astype(BF16) for z in prod]

    inv = [eye + z for z in n_ab]
    pw = n_ab
    span = 1
    while 2 * span < L:
        pw_b = [z.astype(BF16) for z in pw]
        pw = [_dot(z, z) for z in pw_b]
        inv = [inv[p] + _dot(inv[p].astype(BF16), pw[p].astype(BF16)) for p in pairs]
        span *= 2

    s0 = [state_ref[p] for p in pairs]
    ah = [_dot_nt(ar4[p], s0[p].astype(BF16)) for p in pairs]
    rhs = [ah[p][:P] + _dot(n_ak[p], vbd[p]) for p in pairs]
    u = [_dot(inv[p].astype(BF16), rhs[p].astype(BF16)) for p in pairs]
    uv = [jnp.concatenate([u[p].astype(BF16), vbd[p]], axis=0) for p in pairs]
    y_bd = [ah[p][P:] + _dot(m_r[p], uv[p]) for p in pairs]
    y = jnp.concatenate([z[:L] + z[L:] for z in y_bd], axis=1)
    for p in pairs:
        s_inc = lax.dot_general(uv[p], bko4[p], (((0,), (0,)), ((), ())),
                                preferred_element_type=F32)
        state_ref[p] = s0[p] * s_decay[:, cols[p]] + s_inc

    mean = per_head(y, lambda s: s * (1.0 / n))
    yc = y - mean
    var = per_head(yc * yc, lambda s: s * (1.0 / n))
    yn = yc * lax.rsqrt(var + C_GN_EPS) * lnw_ref[...] + lnb_ref[...]
    bonus = per_head(r * k * rk_ref[...], lambda s: s) * v
    o_ref[0] = ((yn + bonus) * g_ref[0]).astype(o_ref.dtype)


def _wkv(r, k, v, lw, a, g, k_k, k_a, r_k, ln_w, ln_b, batch):
    m, width = r.shape
    t = m // batch
    hb = WKV_HEADS * C_HEAD_DIM
    assert t % WKV_L == 0 and width % hb == 0
    seq = lambda z: z.reshape(batch, t, width)
    par = lambda z: z.reshape(1, width).astype(F32)
    tok = pl.BlockSpec((1, WKV_L, hb), lambda bi, hg, c: (bi, c, hg))
    vec = pl.BlockSpec((1, hb), lambda bi, hg, c: (0, hg))
    out = pl.pallas_call(
        _wkv_kernel,
        grid=(batch, width // hb, t // WKV_L),
        in_specs=[tok] * 6 + [vec] * 5,
        out_specs=tok,
        out_shape=jax.ShapeDtypeStruct((batch, t, width), BF16),
        scratch_shapes=[pltpu.VMEM((WKV_HEADS // 2, 2 * C_HEAD_DIM, 2 * C_HEAD_DIM), F32)],
        compiler_params=_cparams("parallel", "parallel", "arbitrary"),
        name="wkv7_chunked",
    )(seq(r), seq(k), seq(v), seq(lw), seq(a), seq(g), par(k_k), par(k_a), par(r_k),
      par(ln_w), par(ln_b))
    return out.reshape(m, width)


def _rope_tables(t):
    half = ROPE_DIM // 2
    freqs = ROPE_THETA ** (-jnp.arange(half, dtype=F32) / half)
    ang = jnp.arange(t, dtype=F32)[:, None] * freqs[None, :]
    cos, sin = jnp.cos(ang), jnp.sin(ang)
    zeros = jnp.zeros((t, LANES - ROPE_DIM), F32)
    return (jnp.concatenate([cos, cos, zeros], axis=-1),
            jnp.concatenate([-sin, sin, zeros], axis=-1))


def _rope128(x, cos, sin):
    half = ROPE_DIM // 2
    swapped = pltpu.roll(x, LANES - half, axis=1) + pltpu.roll(x, half, axis=1)
    return x * cos + swapped * sin


def _mla_prep_kernel(cq_ref, ckv_ref, kpe_ref, qn_ref, kvn_ref, cos_ref, sin_ref,
                     cq_o, ckv_o, kpe_o):
    cq = cq_ref[...]
    cq_o[...] = (cq * lax.rsqrt(jnp.mean(cq * cq, axis=-1, keepdims=True) + EPS)
                 * qn_ref[...]).astype(cq_o.dtype)
    ckv = ckv_ref[...]
    ckv_o[...] = (ckv * lax.rsqrt(jnp.mean(ckv * ckv, axis=-1, keepdims=True) + EPS)
                  * kvn_ref[...]).astype(ckv_o.dtype)
    lane = lax.broadcasted_iota(jnp.int32, kpe_ref.shape, 1)
    kpe = jnp.where(lane < ROPE_DIM, kpe_ref[...], 0.0)
    kpe_o[...] = _rope128(kpe, cos_ref[...], sin_ref[...]).astype(kpe_o.dtype)


def _mla_prep(u_lo, q_norm, kv_norm, cos, sin, q_lora, kv_lora, col0, rows_per_batch):
    m = u_lo.shape[0]
    tb = _pick(rows_per_batch, (256, 128, 64, 32, 16, 8))
    per = rows_per_batch // tb
    row = lambda i: (i, 0)
    fixed = lambda i: (0, 0)
    return pl.pallas_call(
        _mla_prep_kernel,
        grid=(m // tb,),
        in_specs=[pl.BlockSpec((tb, q_lora), row),
                  pl.BlockSpec((tb, kv_lora), row),
                  pl.BlockSpec((tb, LANES), row),
                  pl.BlockSpec((1, q_lora), fixed),
                  pl.BlockSpec((1, kv_lora), fixed),
                  pl.BlockSpec((tb, LANES), lambda i: (i % per, 0)),
                  pl.BlockSpec((tb, LANES), lambda i: (i % per, 0))],
        out_specs=[pl.BlockSpec((tb, q_lora), row),
                   pl.BlockSpec((tb, kv_lora), row),
                   pl.BlockSpec((tb, LANES), row)],
        out_shape=[jax.ShapeDtypeStruct((m, q_lora), BF16),
                   jax.ShapeDtypeStruct((m, kv_lora), BF16),
                   jax.ShapeDtypeStruct((m, LANES), BF16)],
        compiler_params=_cparams("parallel"),
        name="mla_prep",
    )(u_lo[:, col0:col0 + q_lora], u_lo[:, col0 + q_lora:col0 + q_lora + kv_lora],
      u_lo[:, col0 + q_lora + kv_lora:col0 + q_lora + kv_lora + LANES],
      q_norm.reshape(1, q_lora), kv_norm.reshape(1, kv_lora), cos, sin)


MLA_TQ = 1024
MLA_TK = 512


def _chunk_causal_bias(tq, tk):
    r = np.arange(tq)[None, :, None]
    c = np.arange(tk)[None, None, :]
    d = np.arange(tq // tk)[:, None, None]
    return np.where(d * tk + c < (r // CHUNK + 1) * CHUNK, 0.0, NEG_INF).astype(np.float32)


MLA_HEADS_PER_STEP = 2


def _mla_kernel(q_ref, kv_ref, kpe_ref, cos_ref, sin_ref, bias_ref, o_ref, qc_ref, m_ref, l_ref,
                acc_ref):
    i = pl.program_id(2)
    tq, tk = MLA_TQ, MLA_TK
    hw, kvw = NOPE_DIM + LANES, NOPE_DIM + V_DIM
    n_full = (i * tq) // tk
    for h in range(MLA_HEADS_PER_STEP):
        rows = slice(h * tq, (h + 1) * tq)
        qc_ref[rows, :NOPE_DIM] = q_ref[0, :, h * hw:h * hw + NOPE_DIM]
        qc_ref[rows, NOPE_DIM:] = _rope128(
            q_ref[0, :, h * hw + NOPE_DIM:(h + 1) * hw].astype(F32), cos_ref[...],
            sin_ref[...]).astype(BF16)
    m_ref[...] = jnp.full(m_ref.shape, NEG_INF, F32)
    l_ref[...] = jnp.zeros(l_ref.shape, F32)
    acc_ref[...] = jnp.zeros(acc_ref.shape, F32)

    def tile(kt, diag):
        off = pl.multiple_of(kt * tk, tk)
        k_pe = kpe_ref[0, pl.ds(off, tk), :]
        mask = None
        starts = list(range(0, tq, FLASH_ROWS))
        if diag is not None:
            starts = [r for r in starts if r >= diag * tk]
            mask = [bias_ref[diag, r:r + FLASH_ROWS, :] for r in starts]
        for h in range(MLA_HEADS_PER_STEP):
            k_cat = jnp.concatenate([kv_ref[0, pl.ds(off, tk), h * kvw:h * kvw + NOPE_DIM], k_pe],
                                    axis=1)
            v_t = kv_ref[0, pl.ds(off, tk), h * kvw + NOPE_DIM:(h + 1) * kvw]
            _flash_update(qc_ref, [h * tq + r for r in starts], k_cat, v_t, mask, m_ref, l_ref,
                          acc_ref)

    _loop_by_two(n_full, lambda kt: tile(kt, None))
    for d in range(tq // tk):
        tile(n_full + d, d)
    for h in range(MLA_HEADS_PER_STEP):
        rows = slice(h * tq, (h + 1) * tq)
        o_ref[0, :, h * V_DIM:(h + 1) * V_DIM] = (acc_ref[rows, :] / l_ref[rows, :]).astype(
            o_ref.dtype)


def _mla(q, kv, kpe, cos, sin, heads):
    b, t, _ = q.shape
    hp = MLA_HEADS_PER_STEP
    assert t % MLA_TQ == 0 and MLA_TQ % MLA_TK == 0 and heads % hp == 0
    hw = NOPE_DIM + LANES
    bias = _chunk_causal_bias(MLA_TQ, MLA_TK)
    return pl.pallas_call(
        _mla_kernel,
        grid=(b, heads // hp, t // MLA_TQ),
        in_specs=[pl.BlockSpec((1, MLA_TQ, hp * hw), lambda bi, h, i: (bi, i, h)),
                  pl.BlockSpec((1, t, hp * (NOPE_DIM + V_DIM)), lambda bi, h, i: (bi, 0, h)),
                  pl.BlockSpec((1, t, LANES), lambda bi, h, i: (bi, 0, 0)),
                  pl.BlockSpec((MLA_TQ, LANES), lambda bi, h, i: (i, 0)),
                  pl.BlockSpec((MLA_TQ, LANES), lambda bi, h, i: (i, 0)),
                  pl.BlockSpec(bias.shape, lambda bi, h, i: (0, 0, 0))],
        out_specs=pl.BlockSpec((1, MLA_TQ, hp * V_DIM), lambda bi, h, i: (bi, i, h)),
        out_shape=jax.ShapeDtypeStruct((b, t, heads * V_DIM), BF16),
        scratch_shapes=[pltpu.VMEM((hp * MLA_TQ, hw), BF16),
                        pltpu.VMEM((hp * MLA_TQ, LANES), F32),
                        pltpu.VMEM((hp * MLA_TQ, LANES), F32),
                        pltpu.VMEM((hp * MLA_TQ, V_DIM), F32)],
        compiler_params=_cparams("parallel", "parallel", "arbitrary"),
        name="mla_attention",
    )(q, kv, kpe, cos, sin, jnp.asarray(bias))


def _pad_cols(w, n):
    return jnp.pad(w, ((0, 0), (0, n - w.shape[1])))


def _even_mixer(h, x, gate, w_in, w_out, e, rel_bias, batch):
    m, d = h.shape
    t = m // batch
    heads = d // (2 * HEAD_DIM)
    idx_heads = d // 128
    hw = heads * HEAD_DIM
    o_kb = 4 * hw
    o_qi = o_kb + 2 * HEAD_DIM
    o_ki = o_qi + idx_heads * IDX_DIM
    q_factor = LOG2_E * HEAD_DIM ** -0.5
    w_a = jnp.concatenate([w_in[:, :hw] * q_factor, w_in[:, hw:3 * hw]], axis=1)
    u_a = _matmul(h, w_a.astype(BF16), out_dtype=BF16)
    w_qb = w_in[:, 3 * hw:4 * hw] * q_factor
    u_q = _matmul(h, jnp.concatenate([w_qb, w_in[:, o_qi:o_ki]], axis=1).astype(BF16),
                  out_dtype=BF16)
    small = jnp.concatenate([w_in[:, o_kb:o_qi], w_in[:, o_ki:]], axis=1)
    u_s = _matmul(h, _pad_cols(small, 3 * LANES).astype(BF16), out_dtype=F32, tn=3 * LANES)
    u_s = u_s.reshape(batch, t, 3 * LANES)
    kb = u_s[..., :HEAD_DIM].astype(BF16)
    vb = u_s[..., HEAD_DIM:2 * HEAD_DIM].astype(BF16)
    ki = u_s[..., 2 * HEAD_DIM:2 * HEAD_DIM + IDX_DIM].astype(BF16)
    wi = u_s[..., 2 * HEAD_DIM + IDX_DIM:2 * HEAD_DIM + IDX_DIM + idx_heads]
    ya = _attn_a(u_a.reshape(batch, t, 3 * hw), _band_bias_tiles(rel_bias), heads,
                 0, heads, 2 * heads)
    yb = _dsa(u_q.reshape(batch, t, 2 * hw), wi, kb, vb, ki, heads, idx_heads)
    return _matmul((ya.reshape(m, hw), yb.reshape(m, hw)), w_out, w_layer=e, out_dtype=F32,
                   epilogue="resid", x=x, gate=gate, rows_per_batch=t)


def _odd_mixer(h, x, gate, w_in, w_out, o, c_mu, c_w0, c_w_up, c_a0, c_a_up, c_g_up, c_k_k, c_k_a,
               c_r_k, c_ln_w, c_ln_b, d_q_norm, d_kv_norm, d_w_uq, d_w_ukv, batch):
    m, d = h.shape
    t = m // batch
    width = c_w0.shape[0]
    heads = d // (2 * HEAD_DIM)
    q_lora = d_q_norm.shape[0]
    kv_lora = d_kv_norm.shape[0]
    lo_w = DECAY_LORA + ICLR_LORA + GATE_LORA
    w_in = w_in.astype(BF16)
    u_rkv = _matmul(h, w_in[:, :3 * width], out_dtype=BF16)
    rest = w_in[:, 3 * width:]
    rest_cols = -(-(rest.shape[1] + LANES - ROPE_DIM) // 512) * 512
    u_lo = _matmul(h, _pad_cols(rest, rest_cols), out_dtype=F32)

    r, k, v, lw, a, g = _rwkv_prep(u_rkv, u_lo, c_mu, c_w0, c_w_up, c_a0, c_a_up, c_g_up, t)
    yc = _wkv(r, k, v, lw, a, g, c_k_k, c_k_a, c_r_k, c_ln_w, c_ln_b, batch)

    cos, sin = _rope_tables(t)
    cq, ckv, kpe = _mla_prep(u_lo, d_q_norm, d_kv_norm, cos, sin, q_lora, kv_lora, lo_w, t)
    w_uq = d_w_uq.reshape(q_lora, heads, NOPE_DIM + ROPE_DIM)
    w_uq = jnp.pad(w_uq, ((0, 0), (0, 0), (0, LANES - ROPE_DIM))).reshape(q_lora, -1)
    w_uq = w_uq * (LOG2_E * (NOPE_DIM + ROPE_DIM) ** -0.5)
    q = _matmul(cq, w_uq.astype(BF16), out_dtype=BF16)
    kv = _matmul(ckv, d_w_ukv, w_layer=o, out_dtype=BF16)
    yd = _mla(q.reshape(batch, t, -1), kv.reshape(batch, t, -1), kpe.reshape(batch, t, LANES),
              cos, sin, heads)
    return _matmul((yc, yd.reshape(m, -1)), w_out, w_layer=o, out_dtype=F32,
                   epilogue="resid", x=x, gate=gate, rows_per_batch=t)


def kernel(x, c, ada_w, ada_table, norm_mix, norm_ffn, norm_final, ffn_w1, ffn_w2, ev_w_in,
           ev_w_out, a_rel_bias, od_w_in, od_w_out, c_mu, c_w0, c_w_up, c_a0, c_a_up, c_g_up,
           c_k_k, c_k_a, c_r_k, c_ln_w, c_ln_b, d_q_norm, d_kv_norm, d_w_uq, d_w_ukv):
    batch, t, d = x.shape
    depth = ada_table.shape[0]
    m = batch * t
    silu_c = (c * _sigmoid(c)).astype(BF16)
    silu_c = jnp.pad(silu_c, ((0, 8 - batch % 8 if batch % 8 else 0), (0, 0)))
    ada = _matmul(silu_c, ada_w, out_dtype=F32)[:batch].reshape(batch, 6, d)
    x = x.reshape(m, d)
    ev_w_out = ev_w_out.astype(BF16)
    od_w_out = od_w_out.astype(BF16)
    d_w_ukv = d_w_ukv.astype(BF16)
    for layer in range(depth):
        mod = ada + ada_table[layer]
        shift_m, scale_m, gate_m, shift_f, scale_f, gate_f = (mod[:, i, :] for i in range(6))
        h = _norm_mod(x, norm_mix[layer], scale_m, shift_m, t)
        if layer % 2 == 0:
            e = layer // 2
            x = _even_mixer(h, x, gate_m, ev_w_in[e], ev_w_out, e, a_rel_bias[e], batch)
        else:
            o = layer // 2
            x = _odd_mixer(h, x, gate_m, od_w_in[o], od_w_out, o, c_mu[o], c_w0[o], c_w_up[o],
                           c_a0[o], c_a_up[o], c_g_up[o], c_k_k[o], c_k_a[o], c_r_k[o],
                           c_ln_w[o], c_ln_b[o], d_q_norm[o], d_kv_norm[o], d_w_uq[o],
                           d_w_ukv, batch)
        h = _norm_mod(x, norm_ffn[layer], scale_f, shift_f, t)
        h1 = _matmul(h, ffn_w1, w_layer=layer, out_dtype=BF16, epilogue="relu2")
        x = _matmul(h1, ffn_w2, w_layer=layer, out_dtype=F32, epilogue="resid", x=x,
                    gate=gate_f, rows_per_batch=t)
    return _norm_plain(x, norm_final, F32).reshape(batch, t, d)
```
